```python
import jax, jax.numpy as jnp
from jax import lax
import numpy as np

D_MODEL = 2048
BATCH = 1
SEQ = 8192
DEPTH = 2

N_HEADS_A = D_MODEL // 256
HEAD_DIM_A = 128
WIDTH_A = N_HEADS_A * HEAD_DIM_A
MOBA_BLOCK = 256
MOBA_TOPK = 3
Q_CHUNK = 128
SGU_WIDTH = D_MODEL // 2
SGU_GROUPS = 8
SGU_GROUP_DIM = SGU_WIDTH // SGU_GROUPS
SGU_CHUNK = 128
N_MEM = 256
N_HEADS_M = 4
HEAD_DIM_M = D_MODEL // 8
WIDTH_M = N_HEADS_M * HEAD_DIM_M
D_FF = -(-8 * D_MODEL // (3 * 256)) * 256
N_BRANCH = 3
W_IN_COLS = 3 * WIDTH_A + 2 * SGU_WIDTH + WIDTH_M + N_BRANCH * D_MODEL
SPLITS = [WIDTH_A, 2 * WIDTH_A, 3 * WIDTH_A, 3 * WIDTH_A + SGU_WIDTH,
          3 * WIDTH_A + 2 * SGU_WIDTH, 3 * WIDTH_A + 2 * SGU_WIDTH + WIDTH_M]
NEG_INF = -1e30
EPS = 1e-6

kernel_name = 'hybrid_moba_gmlp_memxattn_block'


def rmsnorm(x, g):
    xf = x.astype(jnp.float32)
    y = xf * lax.rsqrt(jnp.mean(xf * xf, axis=-1, keepdims=True) + EPS)
    return (y * g.astype(jnp.float32)).astype(x.dtype)


def alibi_slopes(n_heads):
    return 2.0 ** (-8.0 * jnp.arange(1, n_heads + 1, dtype=jnp.float32) / n_heads)


def moba_attention(q, k, v):
    B, H, S, dh = q.shape
    nb = -(-S // MOBA_BLOCK)
    pad = nb * MOBA_BLOCK - S
    kb = jnp.pad(k, ((0, 0), (0, 0), (0, pad), (0, 0))).reshape(B, H, nb, MOBA_BLOCK, dh)
    vb = jnp.pad(v, ((0, 0), (0, 0), (0, pad), (0, 0))).reshape(B, H, nb, MOBA_BLOCK, dh)
    slopes = alibi_slopes(H)
    q = q * dh ** -0.5
    kmean = jnp.mean(kb.astype(jnp.float32), axis=3)
    gate = jnp.einsum('bhtd,bhnd->bhtn', q.astype(jnp.float32), kmean)
    q_block = jnp.arange(S) // MOBA_BLOCK
    past = jnp.arange(nb)[None, :] < q_block[:, None]
    gate = jnp.where(past, gate, NEG_INF)
    n_sel = min(MOBA_TOPK, nb)
    _, sel = lax.top_k(gate, n_sel)
    nq = S // Q_CHUNK
    qc = jnp.moveaxis(q.reshape(B, H, nq, Q_CHUNK, dh), 2, 0)
    selc = jnp.moveaxis(sel.reshape(B, H, nq, Q_CHUNK, n_sel), 2, 0)
    bi = jnp.arange(B)[:, None, None, None]
    hi = jnp.arange(H)[None, :, None, None]

    def one_chunk(args):
        qi, si, c = args
        t = c * Q_CHUNK + jnp.arange(Q_CHUNK)
        own = (c * Q_CHUNK) // MOBA_BLOCK
        k_sel = kb[bi, hi, si]
        v_sel = vb[bi, hi, si]
        s_pos = si[..., None] * MOBA_BLOCK + jnp.arange(MOBA_BLOCK)
        sc_sel = jnp.einsum('bhqd,bhqnkd->bhqnk', qi, k_sel).astype(jnp.float32)
        dist_sel = (t[None, None, :, None, None] - s_pos).astype(jnp.float32)
        sc_sel = sc_sel - slopes[None, :, None, None, None] * dist_sel
        valid = jnp.arange(n_sel) < own
        sc_sel = jnp.where(valid[:, None], sc_sel, NEG_INF)
        k_own = lax.dynamic_index_in_dim(kb, own, axis=2, keepdims=False)
        v_own = lax.dynamic_index_in_dim(vb, own, axis=2, keepdims=False)
        o_pos = own * MOBA_BLOCK + jnp.arange(MOBA_BLOCK)
        dist_own = (t[:, None] - o_pos[None, :]).astype(jnp.float32)
        sc_own = jnp.einsum('bhqd,bhkd->bhqk', qi, k_own).astype(jnp.float32)
        sc_own = jnp.where(dist_own >= 0, sc_own - slopes[None, :, None, None] * dist_own, NEG_INF)
        sc = jnp.concatenate([sc_sel.reshape(B, H, Q_CHUNK, n_sel * MOBA_BLOCK), sc_own], axis=-1)
        p = jax.nn.softmax(sc, axis=-1).astype(v.dtype)
        p_sel = p[..., :n_sel * MOBA_BLOCK].reshape(B, H, Q_CHUNK, n_sel, MOBA_BLOCK)
        p_own = p[..., n_sel * MOBA_BLOCK:]
        return (jnp.einsum('bhqnk,bhqnkd->bhqd', p_sel, v_sel)
                + jnp.einsum('bhqk,bhkd->bhqd', p_own, v_own))

    out = lax.map(one_chunk, (qc, selc, jnp.arange(nq)))
    return jnp.moveaxis(out, 0, 2).reshape(B, H, S, dh)


def spatial_gating(u, v, g, w_s, b_s):
    B, S, _ = u.shape
    u = jax.nn.gelu(u)
    v = rmsnorm(jax.nn.gelu(v), g)
    nc = S // SGU_CHUNK
    vc = v.reshape(B, nc, SGU_CHUNK, SGU_GROUPS, SGU_GROUP_DIM)
    w = w_s * jnp.tril(jnp.ones((SGU_CHUNK, SGU_CHUNK), dtype=w_s.dtype))
    mixed = jnp.einsum('gts,bnsgc->bntgc', w, vc) + b_s.T[None, None, :, :, None]
    return u * mixed.reshape(B, S, SGU_WIDTH)


def memory_attention(qm, km, vm):
    sc = jnp.einsum('bshd,bmhd->bhsm', qm, km).astype(jnp.float32) * HEAD_DIM_M ** -0.5
    p = jax.nn.softmax(sc, axis=-1).astype(vm.dtype)
    return jnp.einsum('bhsm,bmhd->bshd', p, vm)


def setup_inputs(seed: int = 0) -> dict:
    key = jax.random.key(seed)
    ks = jax.random.split(key, 20)
    f32 = jnp.float32

    def nrm(k, shape, fan_in):
        return jax.random.normal(k, shape, f32) * fan_in ** -0.5

    def gain(k, shape):
        return 1.0 + 0.01 * jax.random.normal(k, shape, f32)

    return {
        'x': jax.random.normal(ks[0], (BATCH, SEQ, D_MODEL), f32),
        'mem': jax.random.normal(ks[1], (BATCH, N_MEM, D_MODEL), f32),
        'g_mix': gain(ks[2], (DEPTH, D_MODEL)),
        'w_in': nrm(ks[3], (DEPTH, D_MODEL, W_IN_COLS), D_MODEL),
        'gq_a': gain(ks[4], (DEPTH, HEAD_DIM_A)),
        'gk_a': gain(ks[5], (DEPTH, HEAD_DIM_A)),
        'g_sgu': gain(ks[6], (DEPTH, SGU_WIDTH)),
        'w_sgu': nrm(ks[7], (DEPTH, SGU_GROUPS, SGU_CHUNK, SGU_CHUNK), SGU_CHUNK),
        'b_sgu': gain(ks[8], (DEPTH, SGU_GROUPS, SGU_CHUNK)),
        'gq_m': gain(ks[9], (DEPTH, HEAD_DIM_M)),
        'gk_m': gain(ks[10], (DEPTH, HEAD_DIM_M)),
        'g_mem': gain(ks[11], (DEPTH, D_MODEL)),
        'w_mem_kv': nrm(ks[12], (DEPTH, D_MODEL, 2 * WIDTH_M), D_MODEL),
        'w_branch': nrm(ks[13], (DEPTH, N_BRANCH, WIDTH_A, D_MODEL), WIDTH_A),
        'w_out': nrm(ks[14], (DEPTH, D_MODEL, D_MODEL), D_MODEL),
        'g_ffn': gain(ks[15], (DEPTH, D_MODEL)),
        'w_gate_up': nrm(ks[16], (DEPTH, D_MODEL, 2 * D_FF), D_MODEL),
        'w_down': nrm(ks[17], (DEPTH, D_FF, D_MODEL), D_FF),
    }


def reference(x, mem, g_mix, w_in, gq_a, gk_a, g_sgu, w_sgu, b_sgu, gq_m, gk_m, g_mem,
              w_mem_kv, w_branch, w_out, g_ffn, w_gate_up, w_down):
    B, S, _ = x.shape
    M = mem.shape[1]
    for l in range(DEPTH):
        h = rmsnorm(x, g_mix[l])
        z = h @ w_in[l]
        qa, ka, va, ub, vb, qm, gates = jnp.split(z, SPLITS, axis=-1)
        qa = rmsnorm(qa.reshape(B, S, N_HEADS_A, HEAD_DIM_A), gq_a[l]).transpose(0, 2, 1, 3)
        ka = rmsnorm(ka.reshape(B, S, N_HEADS_A, HEAD_DIM_A), gk_a[l]).transpose(0, 2, 1, 3)
        va = va.reshape(B, S, N_HEADS_A, HEAD_DIM_A).transpose(0, 2, 1, 3)
        ya = moba_attention(qa, ka, va).transpose(0, 2, 1, 3).reshape(B, S, WIDTH_A)
        yb = spatial_gating(ub, vb, g_sgu[l], w_sgu[l], b_sgu[l])
        kvm = rmsnorm(mem, g_mem[l]) @ w_mem_kv[l]
        km, vm = jnp.split(kvm, 2, axis=-1)
        km = rmsnorm(km.reshape(B, M, N_HEADS_M, HEAD_DIM_M), gk_m[l])
        vm = vm.reshape(B, M, N_HEADS_M, HEAD_DIM_M)
        qm = rmsnorm(qm.reshape(B, S, N_HEADS_M, HEAD_DIM_M), gq_m[l])
        ym = memory_attention(qm, km, vm).reshape(B, S, WIDTH_M)
        ga, gb, gm = jnp.split(jax.nn.sigmoid(gates), N_BRANCH, axis=-1)
        merged = (ga * (ya @ w_branch[l, 0]) + gb * (yb @ w_branch[l, 1])
                  + gm * (ym @ w_branch[l, 2]))
        x = x + merged @ w_out[l]
        hf = rmsnorm(x, g_ffn[l])
        gt, up = jnp.split(hf @ w_gate_up[l], 2, axis=-1)
        x = x + (jax.nn.silu(gt) * up) @ w_down[l]
    return x
```

```python
import functools

import jax
import jax.numpy as jnp
from jax import lax
from jax.experimental import pallas as pl
from jax.experimental.pallas import tpu as pltpu

F32 = jnp.float32
BF16 = jnp.bfloat16

D_MODEL = 2048
SEQ = 8192
DEPTH = 2
N_HEADS_A = 8
HEAD_DIM_A = 128
WIDTH_A = N_HEADS_A * HEAD_DIM_A
MOBA_BLOCK = 256
MOBA_TOPK = 3
N_BLOCKS = SEQ // MOBA_BLOCK
SGU_WIDTH = 1024
SGU_GROUPS = 8
SGU_CHUNK = 128
N_MEM = 256
N_HEADS_M = 4
HEAD_DIM_M = 256
WIDTH_M = N_HEADS_M * HEAD_DIM_M
D_FF = 5632
NEG_INF = -1e30
EPS = 1e-6

COL_QKV = 0
COL_UV = 3 * WIDTH_A
COL_QM = COL_UV + 2 * SGU_WIDTH
COL_GATES = COL_QM + WIDTH_M

VMEM_LIMIT_BYTES = 56 * 1024 * 1024

TM = 1024
TN = 1024


def _params(*semantics):
    return pltpu.CompilerParams(dimension_semantics=semantics,
                                vmem_limit_bytes=VMEM_LIMIT_BYTES)


def _cast_weight_once(w_ref, wbf_ref):
    @pl.when(pl.program_id(1) == 0)
    def _():
        wbf_ref[...] = w_ref[...].astype(BF16)


def _group_rmsnorm(a, g, width):
    outs = []
    for s in range(0, a.shape[1], width):
        blk = a[:, s:s + width]
        ms = jnp.mean(blk * blk, axis=-1, keepdims=True)
        outs.append(blk * lax.rsqrt(ms + EPS) * g)
    return jnp.concatenate(outs, axis=1)


def _norm_kernel(x_ref, g_ref, o_ref):
    x = x_ref[...]
    ms = jnp.mean(x * x, axis=-1, keepdims=True)
    o_ref[...] = (x * lax.rsqrt(ms + EPS) * g_ref[...]).astype(o_ref.dtype)


def _rmsnorm_bf16(x, g_all, layer):
    tm = 512
    return pl.pallas_call(
        _norm_kernel,
        out_shape=jax.ShapeDtypeStruct(x.shape, BF16),
        grid=(x.shape[0] // tm,),
        in_specs=[pl.BlockSpec((tm, D_MODEL), lambda m: (m, 0)),
                  pl.BlockSpec((None, 1, D_MODEL), lambda m: (layer, 0, 0))],
        out_specs=pl.BlockSpec((tm, D_MODEL), lambda m: (m, 0)),
        compiler_params=_params("arbitrary"),
        name="rmsnorm",
    )(x, g_all)


def _qkv_kernel(h_ref, w_ref, gq_ref, gk_ref, o_ref, kmean_ref, wbf_ref, acc_ref):
    n = pl.program_id(0)
    _cast_weight_once(w_ref, wbf_ref)
    acc_ref[...] = jnp.dot(h_ref[...], wbf_ref[...], preferred_element_type=F32)

    def emit(per_head):
        for c in range(0, TN, HEAD_DIM_A):
            a = per_head(acc_ref[:, c:c + HEAD_DIM_A])
            o_ref[:, c:c + HEAD_DIM_A] = a.astype(o_ref.dtype)
            kmean_ref[:, c:c + HEAD_DIM_A] = jnp.mean(
                a.reshape(TM // MOBA_BLOCK, MOBA_BLOCK, HEAD_DIM_A), axis=1)

    def head_norm(a, g):
        ms = jnp.mean(a * a, axis=-1, keepdims=True)
        return a * lax.rsqrt(ms + EPS) * g

    @pl.when(n == 0)
    def _():
        emit(lambda a: head_norm(a, gq_ref[...]) * HEAD_DIM_A ** -0.5)

    @pl.when(n == 1)
    def _():
        emit(lambda a: head_norm(a, gk_ref[...]))

    @pl.when(n == 2)
    def _():
        emit(lambda a: a)


def _qkv_proj(h, w_in, gq_a, gk_a, layer):
    s = h.shape[0]
    return pl.pallas_call(
        _qkv_kernel,
        out_shape=(jax.ShapeDtypeStruct((3, s, WIDTH_A), BF16),
                   jax.ShapeDtypeStruct((3, s // TM, TM // MOBA_BLOCK, WIDTH_A), F32)),
        grid=(3, s // TM),
        in_specs=[pl.BlockSpec((TM, D_MODEL), lambda n, m: (m, 0)),
                  pl.BlockSpec((None, D_MODEL, TN), lambda n, m: (layer, 0, n)),
                  pl.BlockSpec((None, 1, HEAD_DIM_A), lambda n, m: (layer, 0, 0)),
                  pl.BlockSpec((None, 1, HEAD_DIM_A), lambda n, m: (layer, 0, 0))],
        out_specs=(pl.BlockSpec((None, TM, WIDTH_A), lambda n, m: (n, m, 0)),
                   pl.BlockSpec((None, None, TM // MOBA_BLOCK, WIDTH_A),
                                lambda n, m: (n, m, 0, 0))),
        scratch_shapes=[pltpu.VMEM((D_MODEL, TN), BF16), pltpu.VMEM((TM, TN), F32)],
        compiler_params=_params("arbitrary", "arbitrary"),
        name="qkv_proj",
    )(h, w_in, gq_a, gk_a)


def _uv_kernel(h_ref, w_ref, g_ref, o_ref, wbf_ref, acc_ref):
    n = pl.program_id(0)
    _cast_weight_once(w_ref, wbf_ref)
    acc_ref[...] = jax.nn.gelu(
        jnp.dot(h_ref[...], wbf_ref[...], preferred_element_type=F32))

    @pl.when(n == 0)
    def _():
        o_ref[...] = acc_ref[...]

    @pl.when(n == 1)
    def _():
        a = acc_ref[...]
        ms = jnp.mean(a * a, axis=-1, keepdims=True)
        o_ref[...] = a * lax.rsqrt(ms + EPS) * g_ref[...]


def _uv_proj(h, w_in, g_sgu, layer):
    s = h.shape[0]
    off = COL_UV // TN
    return pl.pallas_call(
        _uv_kernel,
        out_shape=jax.ShapeDtypeStruct((2, s, SGU_WIDTH), F32),
        grid=(2, s // TM),
        in_specs=[pl.BlockSpec((TM, D_MODEL), lambda n, m: (m, 0)),
                  pl.BlockSpec((None, D_MODEL, TN), lambda n, m: (layer, 0, off + n)),
                  pl.BlockSpec((None, 1, SGU_WIDTH), lambda n, m: (layer, 0, 0))],
        out_specs=pl.BlockSpec((None, TM, SGU_WIDTH), lambda n, m: (n, m, 0)),
        scratch_shapes=[pltpu.VMEM((D_MODEL, TN), BF16), pltpu.VMEM((TM, TN), F32)],
        compiler_params=_params("arbitrary", "arbitrary"),
        name="uv_proj",
    )(h, w_in, g_sgu)


def _memkv_kernel(mem_ref, gmem_ref, w_ref, gk_ref, o_ref):
    n = pl.program_id(0)
    x = mem_ref[...]
    ms = jnp.mean(x * x, axis=-1, keepdims=True)
    hm = (x * lax.rsqrt(ms + EPS) * gmem_ref[...]).astype(BF16)
    acc = jnp.dot(hm, w_ref[...].astype(BF16), preferred_element_type=F32)

    @pl.when(n == 0)
    def _():
        o_ref[...] = _group_rmsnorm(acc, gk_ref[...], HEAD_DIM_M).astype(o_ref.dtype)

    @pl.when(n == 1)
    def _():
        o_ref[...] = acc.astype(o_ref.dtype)


def _memkv_proj(mem, g_mem, w_mem_kv, gk_m, layer):
    return pl.pallas_call(
        _memkv_kernel,
        out_shape=jax.ShapeDtypeStruct((2, N_MEM, WIDTH_M), BF16),
        grid=(2,),
        in_specs=[pl.BlockSpec((N_MEM, D_MODEL), lambda n: (0, 0)),
                  pl.BlockSpec((None, 1, D_MODEL), lambda n: (layer, 0, 0)),
                  pl.BlockSpec((None, D_MODEL, WIDTH_M), lambda n: (layer, 0, n)),
                  pl.BlockSpec((None, 1, HEAD_DIM_M), lambda n: (layer, 0, 0))],
        out_specs=pl.BlockSpec((None, N_MEM, WIDTH_M), lambda n: (n, 0, 0)),
        compiler_params=_params("arbitrary"),
        name="memkv_proj",
    )(mem, g_mem, w_mem_kv, gk_m)


def _qm_kernel(h_ref, w_ref, gq_ref, km_ref, vm_ref, o_ref, wbf_ref):
    _cast_weight_once(w_ref, wbf_ref)
    acc = jnp.dot(h_ref[...], wbf_ref[...], preferred_element_type=F32)
    outs = []
    for hh in range(N_HEADS_M):
        cols = slice(hh * HEAD_DIM_M, (hh + 1) * HEAD_DIM_M)
        qh = acc[:, cols]
        ms = jnp.mean(qh * qh, axis=-1, keepdims=True)
        qn = (qh * lax.rsqrt(ms + EPS) * gq_ref[...]).astype(BF16)
        sc = lax.dot_general(qn, km_ref[:, cols], (((1,), (1,)), ((), ())),
                             preferred_element_type=F32) * HEAD_DIM_M ** -0.5
        mx = jnp.max(sc, axis=-1, keepdims=True)
        e = jnp.exp(sc - mx)
        p = (e / jnp.sum(e, axis=-1, keepdims=True)).astype(BF16)
        outs.append(jnp.dot(p, vm_ref[:, cols], preferred_element_type=F32))
    o_ref[...] = jnp.concatenate(outs, axis=1).astype(o_ref.dtype)


def _qm_memattn(h, w_in, gq_m, kvm, layer):
    s = h.shape[0]
    off = COL_QM // TN
    return pl.pallas_call(
        _qm_kernel,
        out_shape=jax.ShapeDtypeStruct((s, WIDTH_M), BF16),
        grid=(1, s // TM),
        in_specs=[pl.BlockSpec((TM, D_MODEL), lambda n, m: (m, 0)),
                  pl.BlockSpec((None, D_MODEL, TN), lambda n, m: (layer, 0, off)),
                  pl.BlockSpec((None, 1, HEAD_DIM_M), lambda n, m: (layer, 0, 0)),
                  pl.BlockSpec((None, N_MEM, WIDTH_M), lambda n, m: (0, 0, 0)),
                  pl.BlockSpec((None, N_MEM, WIDTH_M), lambda n, m: (1, 0, 0))],
        out_specs=pl.BlockSpec((TM, WIDTH_M), lambda n, m: (m, 0)),
        scratch_shapes=[pltpu.VMEM((D_MODEL, TN), BF16)],
        compiler_params=_params("arbitrary", "arbitrary"),
        name="qm_memattn",
    )(h, w_in, gq_m, kvm, kvm)


def _gates_kernel(h_ref, w_ref, o_ref, wbf_ref):
    _cast_weight_once(w_ref, wbf_ref)
    o_ref[...] = jax.nn.sigmoid(
        jnp.dot(h_ref[...], wbf_ref[...], preferred_element_type=F32))


def _gates_proj(h, w_in, layer):
    s = h.shape[0]
    off = COL_GATES // TN
    n_tiles = 3 * D_MODEL // TN
    return pl.pallas_call(
        _gates_kernel,
        out_shape=jax.ShapeDtypeStruct((s, 3 * D_MODEL), F32),
        grid=(n_tiles, s // TM),
        in_specs=[pl.BlockSpec((TM, D_MODEL), lambda n, m: (m, 0)),
                  pl.BlockSpec((None, D_MODEL, TN), lambda n, m: (layer, 0, off + n))],
        out_specs=pl.BlockSpec((TM, TN), lambda n, m: (m, n)),
        scratch_shapes=[pltpu.VMEM((D_MODEL, TN), BF16)],
        compiler_params=_params("arbitrary", "arbitrary"),
        name="gates_proj",
    )(h, w_in)


def _moba_kernel(q_ref, k_ref, v_ref, kmean_ref, slope_ref, o_ref):
    i = pl.program_id(1)
    blk = MOBA_BLOCK
    q = q_ref[...]
    slope = slope_ref[...]

    k_own = k_ref[pl.ds(pl.multiple_of(i * blk, blk), blk), :]
    v_own = v_ref[pl.ds(pl.multiple_of(i * blk, blk), blk), :]
    row = lax.broadcasted_iota(jnp.int32, (blk, blk), 0)
    col = lax.broadcasted_iota(jnp.int32, (blk, blk), 1)
    s = lax.dot_general(q, k_own, (((1,), (1,)), ((), ())), preferred_element_type=F32)
    s = jnp.where(col <= row, s + slope * col.astype(F32), NEG_INF)
    m0 = jnp.max(s, axis=1, keepdims=True)
    p = jnp.exp(s - m0)
    l0 = jnp.sum(p, axis=1, keepdims=True)
    acc0 = jnp.dot(p.astype(BF16), v_own, preferred_element_type=F32)

    lane = lax.broadcasted_iota(jnp.int32, (blk, 128), 1)
    lane_f = lane.astype(F32)
    km = jnp.concatenate(
        [kmean_ref[...].astype(BF16), jnp.zeros((128 - N_BLOCKS, HEAD_DIM_A), BF16)], axis=0)
    gate = lax.dot_general(q, km, (((1,), (1,)), ((), ())), preferred_element_type=F32)
    gate = jnp.where(lane < i, gate, NEG_INF)
    pen = jnp.full((blk, 128), NEG_INF, F32)
    for _ in range(MOBA_TOPK):
        mx = jnp.max(gate, axis=1, keepdims=True)
        first = jnp.min(jnp.where(gate == mx, lane_f, 128.0), axis=1, keepdims=True)
        hit = lane_f == first
        pen = jnp.where(jnp.logical_and(hit, mx > 0.5 * NEG_INF), 0.0, pen)
        gate = jnp.where(hit, -3e38, gate)
    extra_q = jnp.where(lane < N_BLOCKS, pen,
                        jnp.where(lane < N_BLOCKS + 2, 1.0, 0.0))
    q_aug = jnp.concatenate([q, extra_q.astype(BF16)], axis=1)

    key_off = lax.broadcasted_iota(jnp.int32, (blk, 128), 0).astype(F32)
    slope_col = slope[:, :128]
    extra_k_const = jnp.where(lane == N_BLOCKS, slope_col * key_off, 0.0)

    def body(j, carry):
        m, l, acc = carry
        start = pl.multiple_of(j * blk, blk)
        kj = k_ref[pl.ds(start, blk), :]
        vj = v_ref[pl.ds(start, blk), :]
        dist = jnp.full((blk, 128), i - j, jnp.int32).astype(F32) * float(blk)
        extra_k = (extra_k_const + jnp.where(lane == j, 1.0, 0.0)
                   + jnp.where(lane == N_BLOCKS + 1, -slope_col * dist, 0.0))
        k_aug = jnp.concatenate([kj, extra_k.astype(BF16)], axis=1)
        s = lax.dot_general(q_aug, k_aug, (((1,), (1,)), ((), ())),
                            preferred_element_type=F32)
        m_new = jnp.maximum(m, jnp.max(s, axis=1, keepdims=True))
        alpha = jnp.exp(m - m_new)
        p = jnp.exp(s - m_new)
        l = alpha * l + jnp.sum(p, axis=1, keepdims=True)
        acc = alpha * acc + jnp.dot(p.astype(BF16), vj, preferred_element_type=F32)
        return m_new, l, acc

    _, l, acc = lax.fori_loop(0, i, body, (m0, l0, acc0))
    o_ref[...] = (acc / l).astype(o_ref.dtype)


def _moba(qkv, kmean, slopes):
    s = qkv.shape[1]
    return pl.pallas_call(
        _moba_kernel,
        out_shape=jax.ShapeDtypeStruct((s, WIDTH_A), BF16),
        grid=(N_HEADS_A, s // MOBA_BLOCK),
        in_specs=[pl.BlockSpec((None, MOBA_BLOCK, HEAD_DIM_A), lambda h, i: (0, i, h)),
                  pl.BlockSpec((None, s, HEAD_DIM_A), lambda h, i: (1, 0, h)),
                  pl.BlockSpec((None, s, HEAD_DIM_A), lambda h, i: (2, 0, h)),
                  pl.BlockSpec((N_BLOCKS, HEAD_DIM_A), lambda h, i: (0, h)),
                  pl.BlockSpec((None, 1, MOBA_BLOCK), lambda h, i: (h, 0, 0))],
        out_specs=pl.BlockSpec((MOBA_BLOCK, HEAD_DIM_A), lambda h, i: (i, h)),
        compiler_params=_params("arbitrary", "arbitrary"),
        name="moba_attention",
    )(qkv, qkv, qkv, kmean, slopes)


def _sgu_kernel(u_ref, v_ref, w_ref, bt_ref, o_ref):
    tm = u_ref.shape[0]
    row = lax.broadcasted_iota(jnp.int32, (SGU_CHUNK, SGU_CHUNK), 0)
    col = lax.broadcasted_iota(jnp.int32, (SGU_CHUNK, SGU_CHUNK), 1)
    for g in range(SGU_GROUPS):
        cols = slice(g * SGU_CHUNK, (g + 1) * SGU_CHUNK)
        w = jnp.where(col <= row, w_ref[g], 0.0).astype(BF16)
        bias = bt_ref[:, g:g + 1]
        for c in range(tm // SGU_CHUNK):
            rows = slice(c * SGU_CHUNK, (c + 1) * SGU_CHUNK)
            mixed = jnp.dot(w, v_ref[rows, cols].astype(BF16),
                            preferred_element_type=F32) + bias
            o_ref[rows, cols] = (u_ref[rows, cols] * mixed).astype(o_ref.dtype)


def _sgu(uv, w_sgu, b_sgu_t, layer):
    s = uv.shape[1]
    tm = 512
    return pl.pallas_call(
        _sgu_kernel,
        out_shape=jax.ShapeDtypeStruct((s, SGU_WIDTH), BF16),
        grid=(s // tm,),
        in_specs=[pl.BlockSpec((None, tm, SGU_WIDTH), lambda m: (0, m, 0)),
                  pl.BlockSpec((None, tm, SGU_WIDTH), lambda m: (1, m, 0)),
                  pl.BlockSpec((None, SGU_GROUPS, SGU_CHUNK, SGU_CHUNK),
                               lambda m: (layer, 0, 0, 0)),
                  pl.BlockSpec((None, SGU_CHUNK, SGU_GROUPS), lambda m: (layer, 0, 0))],
        out_specs=pl.BlockSpec((tm, SGU_WIDTH), lambda m: (m, 0)),
        compiler_params=_params("arbitrary"),
        name="sgu",
    )(uv, uv, w_sgu, b_sgu_t)


MERGE_TN = 512


def _merge_kernel(ya_ref, yb_ref, ym_ref, ga_ref, gb_ref, gm_ref, w_ref, o_ref, wbf_ref):
    _cast_weight_once(w_ref, wbf_ref)
    merged = ga_ref[...] * jnp.dot(ya_ref[...], wbf_ref[0], preferred_element_type=F32)
    merged += gb_ref[...] * jnp.dot(yb_ref[...], wbf_ref[1], preferred_element_type=F32)
    merged += gm_ref[...] * jnp.dot(ym_ref[...], wbf_ref[2], preferred_element_type=F32)
    o_ref[...] = merged.astype(o_ref.dtype)


def _merge(ya, yb, ym, gates, w_branch, layer):
    s = ya.shape[0]
    tn = MERGE_TN
    nt = D_MODEL // tn
    act = pl.BlockSpec((TM, WIDTH_A), lambda n, m: (m, 0))
    return pl.pallas_call(
        _merge_kernel,
        out_shape=jax.ShapeDtypeStruct((s, D_MODEL), BF16),
        grid=(nt, s // TM),
        in_specs=[act, act, act,
                  pl.BlockSpec((TM, tn), lambda n, m: (m, n)),
                  pl.BlockSpec((TM, tn), lambda n, m: (m, nt + n)),
                  pl.BlockSpec((TM, tn), lambda n, m: (m, 2 * nt + n)),
                  pl.BlockSpec((None, 3, WIDTH_A, tn), lambda n, m: (layer, 0, 0, n))],
        out_specs=pl.BlockSpec((TM, tn), lambda n, m: (m, n)),
        scratch_shapes=[pltpu.VMEM((3, WIDTH_A, tn), BF16)],
        compiler_params=_params("arbitrary", "arbitrary"),
        name="branch_merge",
    )(ya, yb, ym, gates, gates, gates, w_branch)


def _resid_proj_kernel(a_ref, w_ref, x_ref, o_ref, wbf_ref):
    _cast_weight_once(w_ref, wbf_ref)
    o_ref[...] = x_ref[...] + jnp.dot(a_ref[...], wbf_ref[...], preferred_element_type=F32)


def _resid_proj(a, w, x, layer, tm, tn, name):
    s, k = a.shape
    n_out = w.shape[2]
    return pl.pallas_call(
        _resid_proj_kernel,
        out_shape=jax.ShapeDtypeStruct((s, n_out), F32),
        grid=(n_out // tn, s // tm),
        in_specs=[pl.BlockSpec((tm, k), lambda n, m: (m, 0)),
                  pl.BlockSpec((None, k, tn), lambda n, m: (layer, 0, n)),
                  pl.BlockSpec((tm, tn), lambda n, m: (m, n))],
        out_specs=pl.BlockSpec((tm, tn), lambda n, m: (m, n)),
        scratch_shapes=[pltpu.VMEM((k, tn), BF16)],
        compiler_params=_params("arbitrary", "arbitrary"),
        name=name,
    )(a, w, x)


FFN_TN = 512


def _ffn_up_kernel(h_ref, wg_ref, wu_ref, o_ref, wgbf_ref, wubf_ref):
    @pl.when(pl.program_id(1) == 0)
    def _():
        wgbf_ref[...] = wg_ref[...].astype(BF16)
        wubf_ref[...] = wu_ref[...].astype(BF16)
    h = h_ref[...]
    gt = jnp.dot(h, wgbf_ref[...], preferred_element_type=F32)
    up = jnp.dot(h, wubf_ref[...], preferred_element_type=F32)
    o_ref[...] = (jax.nn.silu(gt) * up).astype(o_ref.dtype)


def _ffn_up(h, w_gate_up, layer):
    s = h.shape[0]
    tn = FFN_TN
    nt = D_FF // tn
    return pl.pallas_call(
        _ffn_up_kernel,
        out_shape=jax.ShapeDtypeStruct((s, D_FF), BF16),
        grid=(nt, s // TM),
        in_specs=[pl.BlockSpec((TM, D_MODEL), lambda n, m: (m, 0)),
                  pl.BlockSpec((None, D_MODEL, tn), lambda n, m: (layer, 0, n)),
                  pl.BlockSpec((None, D_MODEL, tn), lambda n, m: (layer, 0, nt + n))],
        out_specs=pl.BlockSpec((TM, tn), lambda n, m: (m, n)),
        scratch_shapes=[pltpu.VMEM((D_MODEL, tn), BF16), pltpu.VMEM((D_MODEL, tn), BF16)],
        compiler_params=_params("arbitrary", "arbitrary"),
        name="ffn_up",
    )(h, w_gate_up, w_gate_up)


def kernel(x, mem, g_mix, w_in, gq_a, gk_a, g_sgu, w_sgu, b_sgu, gq_m, gk_m, g_mem,
           w_mem_kv, w_branch, w_out, g_ffn, w_gate_up, w_down):
    b, s, d = x.shape
    assert (b, s, d) == (1, SEQ, D_MODEL) and mem.shape == (1, N_MEM, D_MODEL)
    x2 = x.reshape(s, d)
    mem2 = mem.reshape(N_MEM, d)

    def row(p):
        return p.reshape(DEPTH, 1, p.shape[-1])

    g_mix3, g_ffn3, g_mem3, g_sgu3 = row(g_mix), row(g_ffn), row(g_mem), row(g_sgu)
    gq_a3, gk_a3, gq_m3, gk_m3 = row(gq_a), row(gk_a), row(gq_m), row(gk_m)
    b_sgu_t = jnp.swapaxes(b_sgu, 1, 2)
    slopes = 2.0 ** (-8.0 * jnp.arange(1, N_HEADS_A + 1, dtype=F32) / N_HEADS_A)
    slopes = jnp.broadcast_to(slopes[:, None, None], (N_HEADS_A, 1, MOBA_BLOCK))

    for layer in range(DEPTH):
        h = _rmsnorm_bf16(x2, g_mix3, layer)
        qkv, kmeans = _qkv_proj(h, w_in, gq_a3, gk_a3, layer)
        kmean = kmeans[1].reshape(N_BLOCKS, WIDTH_A)
        uv = _uv_proj(h, w_in, g_sgu3, layer)
        kvm = _memkv_proj(mem2, g_mem3, w_mem_kv, gk_m3, layer)
        ym = _qm_memattn(h, w_in, gq_m3, kvm, layer)
        gates = _gates_proj(h, w_in, layer)
        ya = _moba(qkv, kmean, slopes)
        yb = _sgu(uv, w_sgu, b_sgu_t, layer)
        merged = _merge(ya, yb, ym, gates, w_branch, layer)
        x2 = _resid_proj(merged, w_out, x2, layer, TM, TN, "out_proj")
        hf = _rmsnorm_bf16(x2, g_ffn3, layer)
        act = _ffn_up(hf, w_gate_up, layer)
        x2 = _resid_proj(act, w_down, x2, layer, 512, 512, "ffn_down")
    return x2.reshape(b, s, d)
```

```python
import functools

import jax
import jax.numpy as jnp
from jax import lax
from jax.experimental import pallas as pl
from jax.experimental.pallas import tpu as pltpu

F32 = jnp.float32
BF16 = jnp.bfloat16

D_MODEL = 2048
SEQ = 8192
DEPTH = 2
N_HEADS_A = 8
HEAD_DIM_A = 128
WIDTH_A = N_HEADS_A * HEAD_DIM_A
MOBA_BLOCK = 256
MOBA_TOPK = 3
N_BLOCKS = SEQ // MOBA_BLOCK
SGU_WIDTH = 1024
SGU_GROUPS = 8
SGU_CHUNK = 128
N_MEM = 256
N_HEADS_M = 4
HEAD_DIM_M = 256
WIDTH_M = N_HEADS_M * HEAD_DIM_M
D_FF = 5632
NEG_INF = -1e30
EPS = 1e-6

COL_QKV = 0
COL_UV = 3 * WIDTH_A
COL_QM = COL_UV + 2 * SGU_WIDTH
COL_GATES = COL_QM + WIDTH_M

VMEM_LIMIT_BYTES = 56 * 1024 * 1024

TM = 1024
TN = 1024


def _params(*semantics):
    return pltpu.CompilerParams(dimension_semantics=semantics,
                                vmem_limit_bytes=VMEM_LIMIT_BYTES)


def _cast_weight_once(w_ref, wbf_ref):
    @pl.when(pl.program_id(1) == 0)
    def _():
        wbf_ref[...] = w_ref[...].astype(BF16)


def _group_rmsnorm(a, g, width):
    outs = []
    for s in range(0, a.shape[1], width):
        blk = a[:, s:s + width]
        ms = jnp.mean(blk * blk, axis=-1, keepdims=True)
        outs.append(blk * lax.rsqrt(ms + EPS) * g)
    return jnp.concatenate(outs, axis=1)


def _norm_kernel(x_ref, g_ref, o_ref):
    x = x_ref[...]
    ms = jnp.mean(x * x, axis=-1, keepdims=True)
    o_ref[...] = (x * lax.rsqrt(ms + EPS) * g_ref[...]).astype(o_ref.dtype)


def _rmsnorm_bf16(x, g_all, layer):
    tm = 512
    return pl.pallas_call(
        _norm_kernel,
        out_shape=jax.ShapeDtypeStruct(x.shape, BF16),
        grid=(x.shape[0] // tm,),
        in_specs=[pl.BlockSpec((tm, D_MODEL), lambda m: (m, 0)),
                  pl.BlockSpec((None, 1, D_MODEL), lambda m: (layer, 0, 0))],
        out_specs=pl.BlockSpec((tm, D_MODEL), lambda m: (m, 0)),
        compiler_params=_params("arbitrary"),
        name="rmsnorm",
    )(x, g_all)


def _qkv_kernel(h_ref, w_ref, gq_ref, gk_ref, o_ref, kmean_ref, wbf_ref, acc_ref):
    n = pl.program_id(0)
    _cast_weight_once(w_ref, wbf_ref)
    acc_ref[...] = jnp.dot(h_ref[...], wbf_ref[...], preferred_element_type=F32)

    def emit(per_head):
        for c in range(0, TN, HEAD_DIM_A):
            a = per_head(acc_ref[:, c:c + HEAD_DIM_A])
            o_ref[:, c:c + HEAD_DIM_A] = a.astype(o_ref.dtype)
            kmean_ref[:, c:c + HEAD_DIM_A] = jnp.mean(
                a.reshape(TM // MOBA_BLOCK, MOBA_BLOCK, HEAD_DIM_A), axis=1)

    def head_norm(a, g):
        ms = jnp.mean(a * a, axis=-1, keepdims=True)
        return a * lax.rsqrt(ms + EPS) * g

    @pl.when(n == 0)
    def _():
        emit(lambda a: head_norm(a, gq_ref[...]) * HEAD_DIM_A ** -0.5)

    @pl.when(n == 1)
    def _():
        emit(lambda a: head_norm(a, gk_ref[...]))

    @pl.when(n == 2)
    def _():
        emit(lambda a: a)


def _qkv_proj(h, w_in, gq_a, gk_a, layer):
    s = h.shape[0]
    return pl.pallas_call(
        _qkv_kernel,
        out_shape=(jax.ShapeDtypeStruct((3, s, WIDTH_A), BF16),
                   jax.ShapeDtypeStruct((3, s // TM, TM // MOBA_BLOCK, WIDTH_A), F32)),
        grid=(3, s // TM),
        in_specs=[pl.BlockSpec((TM, D_MODEL), lambda n, m: (m, 0)),
                  pl.BlockSpec((None, D_MODEL, TN), lambda n, m: (layer, 0, n)),
                  pl.BlockSpec((None, 1, HEAD_DIM_A), lambda n, m: (layer, 0, 0)),
                  pl.BlockSpec((None, 1, HEAD_DIM_A), lambda n, m: (layer, 0, 0))],
        out_specs=(pl.BlockSpec((None, TM, WIDTH_A), lambda n, m: (n, m, 0)),
                   pl.BlockSpec((None, None, TM // MOBA_BLOCK, WIDTH_A),
                                lambda n, m: (n, m, 0, 0))),
        scratch_shapes=[pltpu.VMEM((D_MODEL, TN), BF16), pltpu.VMEM((TM, TN), F32)],
        compiler_params=_params("arbitrary", "arbitrary"),
        name="qkv_proj",
    )(h, w_in, gq_a, gk_a)


def _uv_kernel(h_ref, w_ref, g_ref, o_ref, wbf_ref, acc_ref):
    n = pl.program_id(0)
    _cast_weight_once(w_ref, wbf_ref)
    acc_ref[...] = jax.nn.gelu(
        jnp.dot(h_ref[...], wbf_ref[...], preferred_element_type=F32))

    @pl.when(n == 0)
    def _():
        o_ref[...] = acc_ref[...]

    @pl.when(n == 1)
    def _():
        a = acc_ref[...]
        ms = jnp.mean(a * a, axis=-1, keepdims=True)
        o_ref[...] = a * lax.rsqrt(ms + EPS) * g_ref[...]


def _uv_proj(h, w_in, g_sgu, layer):
    s = h.shape[0]
    off = COL_UV // TN
    return pl.pallas_call(
        _uv_kernel,
        out_shape=jax.ShapeDtypeStruct((2, s, SGU_WIDTH), F32),
        grid=(2, s // TM),
        in_specs=[pl.BlockSpec((TM, D_MODEL), lambda n, m: (m, 0)),
                  pl.BlockSpec((None, D_MODEL, TN), lambda n, m: (layer, 0, off + n)),
                  pl.BlockSpec((None, 1, SGU_WIDTH), lambda n, m: (layer, 0, 0))],
        out_specs=pl.BlockSpec((None, TM, SGU_WIDTH), lambda n, m: (n, m, 0)),
        scratch_shapes=[pltpu.VMEM((D_MODEL, TN), BF16), pltpu.VMEM((TM, TN), F32)],
        compiler_params=_params("arbitrary", "arbitrary"),
        name="uv_proj",
    )(h, w_in, g_sgu)


def _memkv_kernel(mem_ref, gmem_ref, w_ref, gk_ref, o_ref):
    n = pl.program_id(0)
    x = mem_ref[...]
    ms = jnp.mean(x * x, axis=-1, keepdims=True)
    hm = (x * lax.rsqrt(ms + EPS) * gmem_ref[...]).astype(BF16)
    acc = jnp.dot(hm, w_ref[...].astype(BF16), preferred_element_type=F32)

    @pl.when(n == 0)
    def _():
        o_ref[...] = _group_rmsnorm(acc, gk_ref[...], HEAD_DIM_M).astype(o_ref.dtype)

    @pl.when(n == 1)
    def _():
        o_ref[...] = acc.astype(o_ref.dtype)


def _memkv_proj(mem, g_mem, w_mem_kv, gk_m, layer):
    return pl.pallas_call(
        _memkv_kernel,
        out_shape=jax.ShapeDtypeStruct((2, N_MEM, WIDTH_M), BF16),
        grid=(2,),
        in_specs=[pl.BlockSpec((N_MEM, D_MODEL), lambda n: (0, 0)),
                  pl.BlockSpec((None, 1, D_MODEL), lambda n: (layer, 0, 0)),
                  pl.BlockSpec((None, D_MODEL, WIDTH_M), lambda n: (layer, 0, n)),
                  pl.BlockSpec((None, 1, HEAD_DIM_M), lambda n: (layer, 0, 0))],
        out_specs=pl.BlockSpec((None, N_MEM, WIDTH_M), lambda n: (n, 0, 0)),
        compiler_params=_params("arbitrary"),
        name="memkv_proj",
    )(mem, g_mem, w_mem_kv, gk_m)


def _qm_kernel(h_ref, w_ref, gq_ref, km_ref, vm_ref, o_ref, wbf_ref):
    _cast_weight_once(w_ref, wbf_ref)
    acc = jnp.dot(h_ref[...], wbf_ref[...], preferred_element_type=F32)
    outs = []
    for hh in range(N_HEADS_M):
        cols = slice(hh * HEAD_DIM_M, (hh + 1) * HEAD_DIM_M)
        qh = acc[:, cols]
        ms = jnp.mean(qh * qh, axis=-1, keepdims=True)
        qn = (qh * lax.rsqrt(ms + EPS) * gq_ref[...]).astype(BF16)
        sc = lax.dot_general(qn, km_ref[:, cols], (((1,), (1,)), ((), ())),
                             preferred_element_type=F32) * HEAD_DIM_M ** -0.5
        mx = jnp.max(sc, axis=-1, keepdims=True)
        e = jnp.exp(sc - mx)
        p = (e / jnp.sum(e, axis=-1, keepdims=True)).astype(BF16)
        outs.append(jnp.dot(p, vm_ref[:, cols], preferred_element_type=F32))
    o_ref[...] = jnp.concatenate(outs, axis=1).astype(o_ref.dtype)


def _qm_memattn(h, w_in, gq_m, kvm, layer):
    s = h.shape[0]
    off = COL_QM // TN
    return pl.pallas_call(
        _qm_kernel,
        out_shape=jax.ShapeDtypeStruct((s, WIDTH_M), BF16),
        grid=(1, s // TM),
        in_specs=[pl.BlockSpec((TM, D_MODEL), lambda n, m: (m, 0)),
                  pl.BlockSpec((None, D_MODEL, TN), lambda n, m: (layer, 0, off)),
                  pl.BlockSpec((None, 1, HEAD_DIM_M), lambda n, m: (layer, 0, 0)),
                  pl.BlockSpec((None, N_MEM, WIDTH_M), lambda n, m: (0, 0, 0)),
                  pl.BlockSpec((None, N_MEM, WIDTH_M), lambda n, m: (1, 0, 0))],
        out_specs=pl.BlockSpec((TM, WIDTH_M), lambda n, m: (m, 0)),
        scratch_shapes=[pltpu.VMEM((D_MODEL, TN), BF16)],
        compiler_params=_params("arbitrary", "arbitrary"),
        name="qm_memattn",
    )(h, w_in, gq_m, kvm, kvm)


def _gates_kernel(h_ref, w_ref, o_ref, wbf_ref):
    _cast_weight_once(w_ref, wbf_ref)
    o_ref[...] = jax.nn.sigmoid(
        jnp.dot(h_ref[...], wbf_ref[...], preferred_element_type=F32))


def _gates_proj(h, w_in, layer):
    s = h.shape[0]
    off = COL_GATES // TN
    n_tiles = 3 * D_MODEL // TN
    return pl.pallas_call(
        _gates_kernel,
        out_shape=jax.ShapeDtypeStruct((s, 3 * D_MODEL), F32),
        grid=(n_tiles, s // TM),
        in_specs=[pl.BlockSpec((TM, D_MODEL), lambda n, m: (m, 0)),
                  pl.BlockSpec((None, D_MODEL, TN), lambda n, m: (layer, 0, off + n))],
        out_specs=pl.BlockSpec((TM, TN), lambda n, m: (m, n)),
        scratch_shapes=[pltpu.VMEM((D_MODEL, TN), BF16)],
        compiler_params=_params("arbitrary", "arbitrary"),
        name="gates_proj",
    )(h, w_in)


COL_KEY_OFF = N_BLOCKS
COL_BLK_OFF = N_BLOCKS + 1
COL_ONES = N_BLOCKS + 2
MOBA_HALF = 2
MOBA_HEADS = 2
SUM_ROWS = 16


def _moba_kernel(q_ref, k_ref, v_ref, kmean_ref, slope_ref, o_ref, kaug_ref, vt_ref, s_ref):
    i = pl.program_id(1)
    blk, dh = MOBA_BLOCK, HEAD_DIM_A
    heads = range(MOBA_HEADS)
    slopes = [slope_ref[hh] for hh in heads]

    @pl.when(i == 0)
    def _():
        lane = lax.broadcasted_iota(jnp.int32, (blk, dh), 1)
        lane_row = lax.broadcasted_iota(jnp.int32, (1, dh), 1)
        key_off = lax.broadcasted_iota(jnp.int32, (blk, dh), 0).astype(F32)

        def build(b, carry):
            rows = pl.ds(pl.multiple_of(b * blk, blk), blk)
            blk_start = jnp.full((1, dh), b * blk, jnp.int32).astype(F32)
            onehot = jnp.where(lane == b, 1.0, jnp.where(lane == COL_ONES, 1.0, 0.0))
            for hh in heads:
                sl = slopes[hh][:, :dh]
                extra = (onehot + jnp.where(lane == COL_KEY_OFF, sl * key_off, 0.0)
                         + jnp.where(lane_row == COL_BLK_OFF, sl * blk_start, 0.0))
                kaug_ref[hh, rows, 0:dh] = k_ref[rows, hh * dh:(hh + 1) * dh]
                kaug_ref[hh, rows, dh:2 * dh] = extra.astype(BF16)
                vt_ref[hh, b, 0:dh] = (
                    v_ref[rows, hh * dh:(hh + 1) * dh].astype(F32).T.astype(BF16))
                vt_ref[hh, b, dh:dh + SUM_ROWS] = jnp.ones((SUM_ROWS, blk), BF16)
            return carry

        lax.fori_loop(0, N_BLOCKS, build, 0)

    own = pl.ds(pl.multiple_of(i * blk, blk), blk)
    key = lax.broadcasted_iota(jnp.int32, (blk, blk), 0)
    qry = lax.broadcasted_iota(jnp.int32, (blk, blk), 1)
    causal = key <= qry
    key_f = key.astype(F32)
    blk_id = lax.broadcasted_iota(jnp.int32, (N_BLOCKS, blk), 0)
    blk_id_f = blk_id.astype(F32)
    past = blk_id < i
    rest = lax.broadcasted_iota(jnp.int32, (dh - N_BLOCKS, blk), 0) + N_BLOCKS
    q_start = jnp.full((1, blk), i * blk, jnp.int32).astype(F32)

    state, q_aug_t = [], []
    for hh in heads:
        slope = slopes[hh]
        q_t = q_ref[:, hh * dh:(hh + 1) * dh].astype(F32).T.astype(BF16)

        s = jnp.dot(kaug_ref[hh, own, 0:dh], q_t, preferred_element_type=F32)
        s = jnp.where(causal, s + slope * key_f, NEG_INF)
        m0 = jnp.max(s, axis=0, keepdims=True)
        p = jnp.exp(s - m0)
        acc0 = jnp.dot(vt_ref[hh, i], p.astype(BF16), preferred_element_type=F32)
        state.append((m0, acc0))

        gate = jnp.dot(kmean_ref[:, hh * dh:(hh + 1) * dh].astype(BF16), q_t,
                       preferred_element_type=F32)
        gate = jnp.where(past, gate, NEG_INF)
        pen = jnp.full((N_BLOCKS, blk), NEG_INF, F32)
        for _ in range(MOBA_TOPK):
            mx = jnp.max(gate, axis=0, keepdims=True)
            first = jnp.min(jnp.where(gate == mx, blk_id_f, float(N_BLOCKS)),
                            axis=0, keepdims=True)
            hit = blk_id_f == first
            pen = jnp.where(jnp.logical_and(hit, mx > 0.5 * NEG_INF), 0.0, pen)
            gate = jnp.where(hit, -3e38, gate)
        extra_q = jnp.where(rest == COL_ONES, -slope * q_start,
                            jnp.where(rest < COL_ONES, 1.0, 0.0))
        q_aug_t.append(jnp.concatenate(
            [q_t, pen.astype(BF16), extra_q.astype(BF16)], axis=0))

    half = MOBA_HALF
    half_rows = half * blk

    def score(hh, first_blk):
        rows = pl.ds(pl.multiple_of(first_blk * blk, half_rows), half_rows)
        return jnp.dot(kaug_ref[hh, rows, :], q_aug_t[hh], preferred_element_type=F32)

    def attend(hh, st, s, first_blk):
        m, acc = st
        m_new = jnp.maximum(m, jnp.max(s, axis=0, keepdims=True))
        alpha = jnp.exp(m - m_new)
        p = jnp.exp(s - m_new)
        v_t = jnp.concatenate([vt_ref[hh, first_blk + g] for g in range(half)], axis=1)
        acc = alpha * acc + jnp.dot(v_t, p.astype(BF16), preferred_element_type=F32)
        return m_new, acc

    for hh in heads:
        s_ref[hh] = score(hh, 0)

    def body(t, carry):
        first = t * (2 * half)
        nxt = jnp.minimum(first + 2 * half, N_BLOCKS - half)
        s_second = [score(hh, first + half) for hh in heads]
        carry = [attend(hh, carry[hh], s_ref[hh], first) for hh in heads]
        for hh in heads:
            s_ref[hh] = score(hh, nxt)
        carry = [attend(hh, carry[hh], s_second[hh], first + half) for hh in heads]
        return tuple(carry)

    n_trips = (i + 2 * half - 1) // (2 * half)
    final = lax.fori_loop(0, n_trips, body, tuple(state))
    for hh in heads:
        _, acc = final[hh]
        out = acc[0:dh] / acc[dh:dh + 1]
        o_ref[:, hh * dh:(hh + 1) * dh] = out.T.astype(o_ref.dtype)


def _moba(qkv, kmean, slopes):
    s = qkv.shape[1]
    width = MOBA_HEADS * HEAD_DIM_A
    return pl.pallas_call(
        _moba_kernel,
        out_shape=jax.ShapeDtypeStruct((s, WIDTH_A), BF16),
        grid=(N_HEADS_A // MOBA_HEADS, s // MOBA_BLOCK),
        in_specs=[pl.BlockSpec((None, MOBA_BLOCK, width), lambda h, i: (0, i, h)),
                  pl.BlockSpec((None, s, width), lambda h, i: (1, 0, h)),
                  pl.BlockSpec((None, s, width), lambda h, i: (2, 0, h)),
                  pl.BlockSpec((N_BLOCKS, width), lambda h, i: (0, h)),
                  pl.BlockSpec((MOBA_HEADS, 1, MOBA_BLOCK), lambda h, i: (h, 0, 0))],
        out_specs=pl.BlockSpec((MOBA_BLOCK, width), lambda h, i: (i, h)),
        scratch_shapes=[pltpu.VMEM((MOBA_HEADS, s, 2 * HEAD_DIM_A), BF16),
                        pltpu.VMEM((MOBA_HEADS, N_BLOCKS, HEAD_DIM_A + SUM_ROWS, MOBA_BLOCK),
                                   BF16),
                        pltpu.VMEM((MOBA_HEADS, MOBA_HALF * MOBA_BLOCK, MOBA_BLOCK), F32)],
        compiler_params=_params("arbitrary", "arbitrary"),
        name="moba_attention",
    )(qkv, qkv, qkv, kmean, slopes)


def _sgu_kernel(u_ref, v_ref, w_ref, bt_ref, o_ref):
    tm = u_ref.shape[0]
    row = lax.broadcasted_iota(jnp.int32, (SGU_CHUNK, SGU_CHUNK), 0)
    col = lax.broadcasted_iota(jnp.int32, (SGU_CHUNK, SGU_CHUNK), 1)
    for g in range(SGU_GROUPS):
        cols = slice(g * SGU_CHUNK, (g + 1) * SGU_CHUNK)
        w = jnp.where(col <= row, w_ref[g], 0.0).astype(BF16)
        bias = bt_ref[:, g:g + 1]
        for c in range(tm // SGU_CHUNK):
            rows = slice(c * SGU_CHUNK, (c + 1) * SGU_CHUNK)
            mixed = jnp.dot(w, v_ref[rows, cols].astype(BF16),
                            preferred_element_type=F32) + bias
            o_ref[rows, cols] = (u_ref[rows, cols] * mixed).astype(o_ref.dtype)


def _sgu(uv, w_sgu, b_sgu_t, layer):
    s = uv.shape[1]
    tm = 512
    return pl.pallas_call(
        _sgu_kernel,
        out_shape=jax.ShapeDtypeStruct((s, SGU_WIDTH), BF16),
        grid=(s // tm,),
        in_specs=[pl.BlockSpec((None, tm, SGU_WIDTH), lambda m: (0, m, 0)),
                  pl.BlockSpec((None, tm, SGU_WIDTH), lambda m: (1, m, 0)),
                  pl.BlockSpec((None, SGU_GROUPS, SGU_CHUNK, SGU_CHUNK),
                               lambda m: (layer, 0, 0, 0)),
                  pl.BlockSpec((None, SGU_CHUNK, SGU_GROUPS), lambda m: (layer, 0, 0))],
        out_specs=pl.BlockSpec((tm, SGU_WIDTH), lambda m: (m, 0)),
        compiler_params=_params("arbitrary"),
        name="sgu",
    )(uv, uv, w_sgu, b_sgu_t)


MERGE_TN = 512


def _merge_kernel(ya_ref, yb_ref, ym_ref, ga_ref, gb_ref, gm_ref, w_ref, o_ref, wbf_ref):
    _cast_weight_once(w_ref, wbf_ref)
    merged = ga_ref[...] * jnp.dot(ya_ref[...], wbf_ref[0], preferred_element_type=F32)
    merged += gb_ref[...] * jnp.dot(yb_ref[...], wbf_ref[1], preferred_element_type=F32)
    merged += gm_ref[...] * jnp.dot(ym_ref[...], wbf_ref[2], preferred_element_type=F32)
    o_ref[...] = merged.astype(o_ref.dtype)


def _merge(ya, yb, ym, gates, w_branch, layer):
    s = ya.shape[0]
    tn = MERGE_TN
    nt = D_MODEL // tn
    act = pl.BlockSpec((TM, WIDTH_A), lambda n, m: (m, 0))
    return pl.pallas_call(
        _merge_kernel,
        out_shape=jax.ShapeDtypeStruct((s, D_MODEL), BF16),
        grid=(nt, s // TM),
        in_specs=[act, act, act,
                  pl.BlockSpec((TM, tn), lambda n, m: (m, n)),
                  pl.BlockSpec((TM, tn), lambda n, m: (m, nt + n)),
                  pl.BlockSpec((TM, tn), lambda n, m: (m, 2 * nt + n)),
                  pl.BlockSpec((None, 3, WIDTH_A, tn), lambda n, m: (layer, 0, 0, n))],
        out_specs=pl.BlockSpec((TM, tn), lambda n, m: (m, n)),
        scratch_shapes=[pltpu.VMEM((3, WIDTH_A, tn), BF16)],
        compiler_params=_params("arbitrary", "arbitrary"),
        name="branch_merge",
    )(ya, yb, ym, gates, gates, gates, w_branch)


def _resid_proj_kernel(a_ref, w_ref, x_ref, o_ref, wbf_ref):
    _cast_weight_once(w_ref, wbf_ref)
    o_ref[...] = x_ref[...] + jnp.dot(a_ref[...], wbf_ref[...], preferred_element_type=F32)


def _resid_proj(a, w, x, layer, tm, tn, name):
    s, k = a.shape
    n_out = w.shape[2]
    return pl.pallas_call(
        _resid_proj_kernel,
        out_shape=jax.ShapeDtypeStruct((s, n_out), F32),
        grid=(n_out // tn, s // tm),
        in_specs=[pl.BlockSpec((tm, k), lambda n, m: (m, 0)),
                  pl.BlockSpec((None, k, tn), lambda n, m: (layer, 0, n)),
                  pl.BlockSpec((tm, tn), lambda n, m: (m, n))],
        out_specs=pl.BlockSpec((tm, tn), lambda n, m: (m, n)),
        scratch_shapes=[pltpu.VMEM((k, tn), BF16)],
        compiler_params=_params("arbitrary", "arbitrary"),
        name=name,
    )(a, w, x)


FFN_TN = 512


def _ffn_up_kernel(h_ref, wg_ref, wu_ref, o_ref, wgbf_ref, wubf_ref):
    @pl.when(pl.program_id(1) == 0)
    def _():
        wgbf_ref[...] = wg_ref[...].astype(BF16)
        wubf_ref[...] = wu_ref[...].astype(BF16)
    h = h_ref[...]
    gt = jnp.dot(h, wgbf_ref[...], preferred_element_type=F32)
    up = jnp.dot(h, wubf_ref[...], preferred_element_type=F32)
    o_ref[...] = (jax.nn.silu(gt) * up).astype(o_ref.dtype)


def _ffn_up(h, w_gate_up, layer):
    s = h.shape[0]
    tn = FFN_TN
    nt = D_FF // tn
    return pl.pallas_call(
        _ffn_up_kernel,
        out_shape=jax.ShapeDtypeStruct((s, D_FF), BF16),
        grid=(nt, s // TM),
        in_specs=[pl.BlockSpec((TM, D_MODEL), lambda n, m: (m, 0)),
                  pl.BlockSpec((None, D_MODEL, tn), lambda n, m: (layer, 0, n)),
                  pl.BlockSpec((None, D_MODEL, tn), lambda n, m: (layer, 0, nt + n))],
        out_specs=pl.BlockSpec((TM, tn), lambda n, m: (m, n)),
        scratch_shapes=[pltpu.VMEM((D_MODEL, tn), BF16), pltpu.VMEM((D_MODEL, tn), BF16)],
        compiler_params=_params("arbitrary", "arbitrary"),
        name="ffn_up",
    )(h, w_gate_up, w_gate_up)


def kernel(x, mem, g_mix, w_in, gq_a, gk_a, g_sgu, w_sgu, b_sgu, gq_m, gk_m, g_mem,
           w_mem_kv, w_branch, w_out, g_ffn, w_gate_up, w_down):
    b, s, d = x.shape
    assert (b, s, d) == (1, SEQ, D_MODEL) and mem.shape == (1, N_MEM, D_MODEL)
    x2 = x.reshape(s, d)
    mem2 = mem.reshape(N_MEM, d)

    def row(p):
        return p.reshape(DEPTH, 1, p.shape[-1])

    g_mix3, g_ffn3, g_mem3, g_sgu3 = row(g_mix), row(g_ffn), row(g_mem), row(g_sgu)
    gq_a3, gk_a3, gq_m3, gk_m3 = row(gq_a), row(gk_a), row(gq_m), row(gk_m)
    b_sgu_t = jnp.swapaxes(b_sgu, 1, 2)
    slopes = 2.0 ** (-8.0 * jnp.arange(1, N_HEADS_A + 1, dtype=F32) / N_HEADS_A)
    slopes = jnp.broadcast_to(slopes[:, None, None], (N_HEADS_A, 1, MOBA_BLOCK))

    for layer in range(DEPTH):
        h = _rmsnorm_bf16(x2, g_mix3, layer)
        qkv, kmeans = _qkv_proj(h, w_in, gq_a3, gk_a3, layer)
        kmean = kmeans[1].reshape(N_BLOCKS, WIDTH_A)
        uv = _uv_proj(h, w_in, g_sgu3, layer)
        kvm = _memkv_proj(mem2, g_mem3, w_mem_kv, gk_m3, layer)
        ym = _qm_memattn(h, w_in, gq_m3, kvm, layer)
        gates = _gates_proj(h, w_in, layer)
        ya = _moba(qkv, kmean, slopes)
        yb = _sgu(uv, w_sgu, b_sgu_t, layer)
        merged = _merge(ya, yb, ym, gates, w_branch, layer)
        x2 = _resid_proj(merged, w_out, x2, layer, TM, TN, "out_proj")
        hf = _rmsnorm_bf16(x2, g_ffn3, layer)
        act = _ffn_up(hf, w_gate_up, layer)
        x2 = _resid_proj(act, w_down, x2, layer, 512, 512, "ffn_down")
    return x2.reshape(b, s, d)
```

```python
import functools

import jax
import jax.numpy as jnp
from jax import lax
from jax.experimental import pallas as pl
from jax.experimental.pallas import tpu as pltpu

F32 = jnp.float32
BF16 = jnp.bfloat16

D_MODEL = 2048
SEQ = 8192
DEPTH = 2
N_HEADS_A = 8
HEAD_DIM_A = 128
WIDTH_A = N_HEADS_A * HEAD_DIM_A
MOBA_BLOCK = 256
MOBA_TOPK = 3
N_BLOCKS = SEQ // MOBA_BLOCK
SGU_WIDTH = 1024
SGU_GROUPS = 8
SGU_CHUNK = 128
N_MEM = 256
N_HEADS_M = 4
HEAD_DIM_M = 256
WIDTH_M = N_HEADS_M * HEAD_DIM_M
D_FF = 5632
NEG_INF = -1e30
EPS = 1e-6

COL_QKV = 0
COL_UV = 3 * WIDTH_A
COL_QM = COL_UV + 2 * SGU_WIDTH
COL_GATES = COL_QM + WIDTH_M

VMEM_LIMIT_BYTES = 56 * 1024 * 1024

TM = 1024
TN = 1024
SUB_ROWS = 256


def _params(*semantics):
    return pltpu.CompilerParams(dimension_semantics=semantics,
                                vmem_limit_bytes=VMEM_LIMIT_BYTES)


def _cast_weight_once(w_ref, wbf_ref):
    @pl.when(pl.program_id(1) == 0)
    def _():
        wbf_ref[...] = w_ref[...].astype(BF16)


def _group_rmsnorm(a, g, width):
    outs = []
    for s in range(0, a.shape[1], width):
        blk = a[:, s:s + width]
        ms = jnp.mean(blk * blk, axis=-1, keepdims=True)
        outs.append(blk * lax.rsqrt(ms + EPS) * g)
    return jnp.concatenate(outs, axis=1)


def _norm_kernel(x_ref, g_ref, o_ref):
    x = x_ref[...]
    ms = jnp.mean(x * x, axis=-1, keepdims=True)
    o_ref[...] = (x * lax.rsqrt(ms + EPS) * g_ref[...]).astype(o_ref.dtype)


def _rmsnorm_bf16(x, g_all, layer):
    tm = 512
    return pl.pallas_call(
        _norm_kernel,
        out_shape=jax.ShapeDtypeStruct(x.shape, BF16),
        grid=(x.shape[0] // tm,),
        in_specs=[pl.BlockSpec((tm, D_MODEL), lambda m: (m, 0)),
                  pl.BlockSpec((None, 1, D_MODEL), lambda m: (layer, 0, 0))],
        out_specs=pl.BlockSpec((tm, D_MODEL), lambda m: (m, 0)),
        compiler_params=_params("arbitrary"),
        name="rmsnorm",
    )(x, g_all)


def _sub_dots(lhs_ref, rhs, emit):
    for r in range(0, lhs_ref.shape[0], SUB_ROWS):
        rows = slice(r, r + SUB_ROWS)
        emit(rows, jnp.dot(lhs_ref[rows, :], rhs, preferred_element_type=F32))


def _qkv_kernel(h_ref, w_ref, gq_ref, gk_ref, o_ref, kmean_ref, wbf_ref):
    n = pl.program_id(0)
    _cast_weight_once(w_ref, wbf_ref)
    assert SUB_ROWS == MOBA_BLOCK

    def emit(per_head):
        def emit_rows(rows, acc):
            b = rows.start // MOBA_BLOCK
            for c in range(0, TN, HEAD_DIM_A):
                a = per_head(acc[:, c:c + HEAD_DIM_A])
                o_ref[rows, c:c + HEAD_DIM_A] = a.astype(o_ref.dtype)
                kmean_ref[b:b + 1, c:c + HEAD_DIM_A] = jnp.mean(a, axis=0, keepdims=True)
        _sub_dots(h_ref, wbf_ref[...], emit_rows)

    def head_norm(a, g):
        ms = jnp.mean(a * a, axis=-1, keepdims=True)
        return a * lax.rsqrt(ms + EPS) * g

    @pl.when(n == 0)
    def _():
        emit(lambda a: head_norm(a, gq_ref[...]) * HEAD_DIM_A ** -0.5)

    @pl.when(n == 1)
    def _():
        emit(lambda a: head_norm(a, gk_ref[...]))

    @pl.when(n == 2)
    def _():
        emit(lambda a: a)


def _qkv_proj(h, w_in, gq_a, gk_a, layer):
    s = h.shape[0]
    return pl.pallas_call(
        _qkv_kernel,
        out_shape=(jax.ShapeDtypeStruct((3, s, WIDTH_A), BF16),
                   jax.ShapeDtypeStruct((3, s // TM, TM // MOBA_BLOCK, WIDTH_A), F32)),
        grid=(3, s // TM),
        in_specs=[pl.BlockSpec((TM, D_MODEL), lambda n, m: (m, 0)),
                  pl.BlockSpec((None, D_MODEL, TN), lambda n, m: (layer, 0, n)),
                  pl.BlockSpec((None, 1, HEAD_DIM_A), lambda n, m: (layer, 0, 0)),
                  pl.BlockSpec((None, 1, HEAD_DIM_A), lambda n, m: (layer, 0, 0))],
        out_specs=(pl.BlockSpec((None, TM, WIDTH_A), lambda n, m: (n, m, 0)),
                   pl.BlockSpec((None, None, TM // MOBA_BLOCK, WIDTH_A),
                                lambda n, m: (n, m, 0, 0))),
        scratch_shapes=[pltpu.VMEM((D_MODEL, TN), BF16)],
        compiler_params=_params("arbitrary", "arbitrary"),
        name="qkv_proj",
    )(h, w_in, gq_a, gk_a)


def _uv_kernel(h_ref, w_ref, g_ref, o_ref, wbf_ref):
    n = pl.program_id(0)
    _cast_weight_once(w_ref, wbf_ref)

    @pl.when(n == 0)
    def _():
        def emit(rows, acc):
            o_ref[rows, :] = jax.nn.gelu(acc)
        _sub_dots(h_ref, wbf_ref[...], emit)

    @pl.when(n == 1)
    def _():
        def emit(rows, acc):
            a = jax.nn.gelu(acc)
            ms = jnp.mean(a * a, axis=-1, keepdims=True)
            o_ref[rows, :] = a * lax.rsqrt(ms + EPS) * g_ref[...]
        _sub_dots(h_ref, wbf_ref[...], emit)


def _uv_proj(h, w_in, g_sgu, layer):
    s = h.shape[0]
    off = COL_UV // TN
    return pl.pallas_call(
        _uv_kernel,
        out_shape=jax.ShapeDtypeStruct((2, s, SGU_WIDTH), F32),
        grid=(2, s // TM),
        in_specs=[pl.BlockSpec((TM, D_MODEL), lambda n, m: (m, 0)),
                  pl.BlockSpec((None, D_MODEL, TN), lambda n, m: (layer, 0, off + n)),
                  pl.BlockSpec((None, 1, SGU_WIDTH), lambda n, m: (layer, 0, 0))],
        out_specs=pl.BlockSpec((None, TM, SGU_WIDTH), lambda n, m: (n, m, 0)),
        scratch_shapes=[pltpu.VMEM((D_MODEL, TN), BF16)],
        compiler_params=_params("arbitrary", "arbitrary"),
        name="uv_proj",
    )(h, w_in, g_sgu)


def _memkv_kernel(mem_ref, gmem_ref, w_ref, gk_ref, o_ref):
    n = pl.program_id(0)
    x = mem_ref[...]
    ms = jnp.mean(x * x, axis=-1, keepdims=True)
    hm = (x * lax.rsqrt(ms + EPS) * gmem_ref[...]).astype(BF16)
    acc = jnp.dot(hm, w_ref[...].astype(BF16), preferred_element_type=F32)

    @pl.when(n == 0)
    def _():
        o_ref[...] = _group_rmsnorm(acc, gk_ref[...], HEAD_DIM_M).astype(o_ref.dtype)

    @pl.when(n == 1)
    def _():
        o_ref[...] = acc.astype(o_ref.dtype)


def _memkv_proj(mem, g_mem, w_mem_kv, gk_m, layer):
    return pl.pallas_call(
        _memkv_kernel,
        out_shape=jax.ShapeDtypeStruct((2, N_MEM, WIDTH_M), BF16),
        grid=(2,),
        in_specs=[pl.BlockSpec((N_MEM, D_MODEL), lambda n: (0, 0)),
                  pl.BlockSpec((None, 1, D_MODEL), lambda n: (layer, 0, 0)),
                  pl.BlockSpec((None, D_MODEL, WIDTH_M), lambda n: (layer, 0, n)),
                  pl.BlockSpec((None, 1, HEAD_DIM_M), lambda n: (layer, 0, 0))],
        out_specs=pl.BlockSpec((None, N_MEM, WIDTH_M), lambda n: (n, 0, 0)),
        compiler_params=_params("arbitrary"),
        name="memkv_proj",
    )(mem, g_mem, w_mem_kv, gk_m)


def _qm_kernel(h_ref, w_ref, gq_ref, km_ref, vm_ref, o_ref, wbf_ref):
    _cast_weight_once(w_ref, wbf_ref)

    def emit(rows, acc):
        for hh in range(N_HEADS_M):
            cols = slice(hh * HEAD_DIM_M, (hh + 1) * HEAD_DIM_M)
            qh = acc[:, cols]
            ms = jnp.mean(qh * qh, axis=-1, keepdims=True)
            qn = (qh * lax.rsqrt(ms + EPS) * gq_ref[...]).astype(BF16)
            sc = lax.dot_general(qn, km_ref[:, cols], (((1,), (1,)), ((), ())),
                                 preferred_element_type=F32) * HEAD_DIM_M ** -0.5
            mx = jnp.max(sc, axis=-1, keepdims=True)
            e = jnp.exp(sc - mx)
            p = (e / jnp.sum(e, axis=-1, keepdims=True)).astype(BF16)
            o_ref[rows, cols] = jnp.dot(
                p, vm_ref[:, cols], preferred_element_type=F32).astype(o_ref.dtype)

    _sub_dots(h_ref, wbf_ref[...], emit)


def _qm_memattn(h, w_in, gq_m, kvm, layer):
    s = h.shape[0]
    off = COL_QM // TN
    return pl.pallas_call(
        _qm_kernel,
        out_shape=jax.ShapeDtypeStruct((s, WIDTH_M), BF16),
        grid=(1, s // TM),
        in_specs=[pl.BlockSpec((TM, D_MODEL), lambda n, m: (m, 0)),
                  pl.BlockSpec((None, D_MODEL, TN), lambda n, m: (layer, 0, off)),
                  pl.BlockSpec((None, 1, HEAD_DIM_M), lambda n, m: (layer, 0, 0)),
                  pl.BlockSpec((None, N_MEM, WIDTH_M), lambda n, m: (0, 0, 0)),
                  pl.BlockSpec((None, N_MEM, WIDTH_M), lambda n, m: (1, 0, 0))],
        out_specs=pl.BlockSpec((TM, WIDTH_M), lambda n, m: (m, 0)),
        scratch_shapes=[pltpu.VMEM((D_MODEL, TN), BF16)],
        compiler_params=_params("arbitrary", "arbitrary"),
        name="qm_memattn",
    )(h, w_in, gq_m, kvm, kvm)


def _gates_kernel(h_ref, w_ref, o_ref, wbf_ref):
    _cast_weight_once(w_ref, wbf_ref)
    def emit(rows, acc):
        o_ref[rows, :] = jax.nn.sigmoid(acc)

    _sub_dots(h_ref, wbf_ref[...], emit)


def _gates_proj(h, w_in, layer):
    s = h.shape[0]
    off = COL_GATES // TN
    n_tiles = 3 * D_MODEL // TN
    return pl.pallas_call(
        _gates_kernel,
        out_shape=jax.ShapeDtypeStruct((s, 3 * D_MODEL), F32),
        grid=(n_tiles, s // TM),
        in_specs=[pl.BlockSpec((TM, D_MODEL), lambda n, m: (m, 0)),
                  pl.BlockSpec((None, D_MODEL, TN), lambda n, m: (layer, 0, off + n))],
        out_specs=pl.BlockSpec((TM, TN), lambda n, m: (m, n)),
        scratch_shapes=[pltpu.VMEM((D_MODEL, TN), BF16)],
        compiler_params=_params("arbitrary", "arbitrary"),
        name="gates_proj",
    )(h, w_in)


COL_KEY_OFF = N_BLOCKS
COL_BLK_OFF = N_BLOCKS + 1
COL_ONES = N_BLOCKS + 2
MOBA_HALF = 2
MOBA_HEADS = 2
SUM_ROWS = 16


def _moba_kernel(q_ref, k_ref, v_ref, kmean_ref, slope_ref, o_ref, kaug_ref, vt_ref, s_ref):
    i = pl.program_id(1)
    blk, dh = MOBA_BLOCK, HEAD_DIM_A
    heads = range(MOBA_HEADS)
    slopes = [slope_ref[hh] for hh in heads]

    @pl.when(i == 0)
    def _():
        lane = lax.broadcasted_iota(jnp.int32, (blk, dh), 1)
        lane_row = lax.broadcasted_iota(jnp.int32, (1, dh), 1)
        key_off = lax.broadcasted_iota(jnp.int32, (blk, dh), 0).astype(F32)

        def build(b, carry):
            rows = pl.ds(pl.multiple_of(b * blk, blk), blk)
            blk_start = jnp.full((1, dh), b * blk, jnp.int32).astype(F32)
            onehot = jnp.where(lane == b, 1.0, jnp.where(lane == COL_ONES, 1.0, 0.0))
            for hh in heads:
                sl = slopes[hh][:, :dh]
                extra = (onehot + jnp.where(lane == COL_KEY_OFF, sl * key_off, 0.0)
                         + jnp.where(lane_row == COL_BLK_OFF, sl * blk_start, 0.0))
                kaug_ref[hh, rows, 0:dh] = k_ref[rows, hh * dh:(hh + 1) * dh]
                kaug_ref[hh, rows, dh:2 * dh] = extra.astype(BF16)
                vt_ref[hh, b, 0:dh] = (
                    v_ref[rows, hh * dh:(hh + 1) * dh].astype(F32).T.astype(BF16))
                vt_ref[hh, b, dh:dh + SUM_ROWS] = jnp.ones((SUM_ROWS, blk), BF16)
            return carry

        lax.fori_loop(0, N_BLOCKS, build, 0)

    own = pl.ds(pl.multiple_of(i * blk, blk), blk)
    key = lax.broadcasted_iota(jnp.int32, (blk, blk), 0)
    qry = lax.broadcasted_iota(jnp.int32, (blk, blk), 1)
    causal = key <= qry
    key_f = key.astype(F32)
    blk_id = lax.broadcasted_iota(jnp.int32, (N_BLOCKS, blk), 0)
    blk_id_f = blk_id.astype(F32)
    past = blk_id < i
    rest = lax.broadcasted_iota(jnp.int32, (dh - N_BLOCKS, blk), 0) + N_BLOCKS
    q_start = jnp.full((1, blk), i * blk, jnp.int32).astype(F32)

    state, q_aug_t = [], []
    for hh in heads:
        slope = slopes[hh]
        q_t = q_ref[:, hh * dh:(hh + 1) * dh].astype(F32).T.astype(BF16)

        s = jnp.dot(kaug_ref[hh, own, 0:dh], q_t, preferred_element_type=F32)
        s = jnp.where(causal, s + slope * key_f, NEG_INF)
        m0 = jnp.max(s, axis=0, keepdims=True)
        p = jnp.exp(s - m0)
        acc0 = jnp.dot(vt_ref[hh, i], p.astype(BF16), preferred_element_type=F32)
        state.append((m0, acc0))

        gate = jnp.dot(kmean_ref[:, hh * dh:(hh + 1) * dh].astype(BF16), q_t,
                       preferred_element_type=F32)
        gate = jnp.where(past, gate, NEG_INF)
        pen = jnp.full((N_BLOCKS, blk), NEG_INF, F32)
        for _ in range(MOBA_TOPK):
            mx = jnp.max(gate, axis=0, keepdims=True)
            first = jnp.min(jnp.where(gate == mx, blk_id_f, float(N_BLOCKS)),
                            axis=0, keepdims=True)
            hit = blk_id_f == first
            pen = jnp.where(jnp.logical_and(hit, mx > 0.5 * NEG_INF), 0.0, pen)
            gate = jnp.where(hit, -3e38, gate)
        extra_q = jnp.where(rest == COL_ONES, -slope * q_start,
                            jnp.where(rest < COL_ONES, 1.0, 0.0))
        q_aug_t.append(jnp.concatenate(
            [q_t, pen.astype(BF16), extra_q.astype(BF16)], axis=0))

    half = MOBA_HALF
    half_rows = half * blk

    def score(hh, first_blk):
        rows = pl.ds(pl.multiple_of(first_blk * blk, half_rows), half_rows)
        return jnp.dot(kaug_ref[hh, rows, :], q_aug_t[hh], preferred_element_type=F32)

    def attend(hh, st, s, first_blk):
        m, acc = st
        m_new = jnp.maximum(m, jnp.max(s, axis=0, keepdims=True))
        alpha = jnp.exp(m - m_new)
        p = jnp.exp(s - m_new)
        v_t = jnp.concatenate([vt_ref[hh, first_blk + g] for g in range(half)], axis=1)
        acc = alpha * acc + jnp.dot(v_t, p.astype(BF16), preferred_element_type=F32)
        return m_new, acc

    for hh in heads:
        s_ref[hh] = score(hh, 0)

    def body(t, carry):
        first = t * (2 * half)
        nxt = jnp.minimum(first + 2 * half, N_BLOCKS - half)
        s_second = [score(hh, first + half) for hh in heads]
        carry = [attend(hh, carry[hh], s_ref[hh], first) for hh in heads]
        for hh in heads:
            s_ref[hh] = score(hh, nxt)
        carry = [attend(hh, carry[hh], s_second[hh], first + half) for hh in heads]
        return tuple(carry)

    n_trips = (i + 2 * half - 1) // (2 * half)
    final = lax.fori_loop(0, n_trips, body, tuple(state))
    for hh in heads:
        _, acc = final[hh]
        out = acc[0:dh] / acc[dh:dh + 1]
        o_ref[:, hh * dh:(hh + 1) * dh] = out.T.astype(o_ref.dtype)


def _moba(qkv, kmean, slopes):
    s = qkv.shape[1]
    width = MOBA_HEADS * HEAD_DIM_A
    return pl.pallas_call(
        _moba_kernel,
        out_shape=jax.ShapeDtypeStruct((s, WIDTH_A), BF16),
        grid=(N_HEADS_A // MOBA_HEADS, s // MOBA_BLOCK),
        in_specs=[pl.BlockSpec((None, MOBA_BLOCK, width), lambda h, i: (0, i, h)),
                  pl.BlockSpec((None, s, width), lambda h, i: (1, 0, h)),
                  pl.BlockSpec((None, s, width), lambda h, i: (2, 0, h)),
                  pl.BlockSpec((N_BLOCKS, width), lambda h, i: (0, h)),
                  pl.BlockSpec((MOBA_HEADS, 1, MOBA_BLOCK), lambda h, i: (h, 0, 0))],
        out_specs=pl.BlockSpec((MOBA_BLOCK, width), lambda h, i: (i, h)),
        scratch_shapes=[pltpu.VMEM((MOBA_HEADS, s, 2 * HEAD_DIM_A), BF16),
                        pltpu.VMEM((MOBA_HEADS, N_BLOCKS, HEAD_DIM_A + SUM_ROWS, MOBA_BLOCK),
                                   BF16),
                        pltpu.VMEM((MOBA_HEADS, MOBA_HALF * MOBA_BLOCK, MOBA_BLOCK), F32)],
        compiler_params=_params("arbitrary", "arbitrary"),
        name="moba_attention",
    )(qkv, qkv, qkv, kmean, slopes)


def _sgu_kernel(u_ref, v_ref, w_ref, bt_ref, o_ref):
    tm = u_ref.shape[0]
    row = lax.broadcasted_iota(jnp.int32, (SGU_CHUNK, SGU_CHUNK), 0)
    col = lax.broadcasted_iota(jnp.int32, (SGU_CHUNK, SGU_CHUNK), 1)
    for g in range(SGU_GROUPS):
        cols = slice(g * SGU_CHUNK, (g + 1) * SGU_CHUNK)
        w = jnp.where(col <= row, w_ref[g], 0.0).astype(BF16)
        bias = bt_ref[:, g:g + 1]
        for c in range(tm // SGU_CHUNK):
            rows = slice(c * SGU_CHUNK, (c + 1) * SGU_CHUNK)
            mixed = jnp.dot(w, v_ref[rows, cols].astype(BF16),
                            preferred_element_type=F32) + bias
            o_ref[rows, cols] = (u_ref[rows, cols] * mixed).astype(o_ref.dtype)


def _sgu(uv, w_sgu, b_sgu_t, layer):
    s = uv.shape[1]
    tm = 512
    return pl.pallas_call(
        _sgu_kernel,
        out_shape=jax.ShapeDtypeStruct((s, SGU_WIDTH), BF16),
        grid=(s // tm,),
        in_specs=[pl.BlockSpec((None, tm, SGU_WIDTH), lambda m: (0, m, 0)),
                  pl.BlockSpec((None, tm, SGU_WIDTH), lambda m: (1, m, 0)),
                  pl.BlockSpec((None, SGU_GROUPS, SGU_CHUNK, SGU_CHUNK),
                               lambda m: (layer, 0, 0, 0)),
                  pl.BlockSpec((None, SGU_CHUNK, SGU_GROUPS), lambda m: (layer, 0, 0))],
        out_specs=pl.BlockSpec((tm, SGU_WIDTH), lambda m: (m, 0)),
        compiler_params=_params("arbitrary"),
        name="sgu",
    )(uv, uv, w_sgu, b_sgu_t)


MERGE_TN = 512


def _merge_kernel(ya_ref, yb_ref, ym_ref, ga_ref, gb_ref, gm_ref, w_ref, o_ref, wbf_ref):
    _cast_weight_once(w_ref, wbf_ref)
    for r in range(0, ya_ref.shape[0], SUB_ROWS):
        rows = slice(r, r + SUB_ROWS)
        merged = ga_ref[rows, :] * jnp.dot(ya_ref[rows, :], wbf_ref[0],
                                           preferred_element_type=F32)
        merged += gb_ref[rows, :] * jnp.dot(yb_ref[rows, :], wbf_ref[1],
                                            preferred_element_type=F32)
        merged += gm_ref[rows, :] * jnp.dot(ym_ref[rows, :], wbf_ref[2],
                                            preferred_element_type=F32)
        o_ref[rows, :] = merged.astype(o_ref.dtype)


def _merge(ya, yb, ym, gates, w_branch, layer):
    s = ya.shape[0]
    tn = MERGE_TN
    nt = D_MODEL // tn
    act = pl.BlockSpec((TM, WIDTH_A), lambda n, m: (m, 0))
    return pl.pallas_call(
        _merge_kernel,
        out_shape=jax.ShapeDtypeStruct((s, D_MODEL), BF16),
        grid=(nt, s // TM),
        in_specs=[act, act, act,
                  pl.BlockSpec((TM, tn), lambda n, m: (m, n)),
                  pl.BlockSpec((TM, tn), lambda n, m: (m, nt + n)),
                  pl.BlockSpec((TM, tn), lambda n, m: (m, 2 * nt + n)),
                  pl.BlockSpec((None, 3, WIDTH_A, tn), lambda n, m: (layer, 0, 0, n))],
        out_specs=pl.BlockSpec((TM, tn), lambda n, m: (m, n)),
        scratch_shapes=[pltpu.VMEM((3, WIDTH_A, tn), BF16)],
        compiler_params=_params("arbitrary", "arbitrary"),
        name="branch_merge",
    )(ya, yb, ym, gates, gates, gates, w_branch)


def _resid_proj_kernel(a_ref, w_ref, x_ref, o_ref, wbf_ref):
    _cast_weight_once(w_ref, wbf_ref)
    def emit(rows, acc):
        o_ref[rows, :] = x_ref[rows, :] + acc

    _sub_dots(a_ref, wbf_ref[...], emit)


def _resid_proj(a, w, x, layer, tm, tn, name):
    s, k = a.shape
    n_out = w.shape[2]
    return pl.pallas_call(
        _resid_proj_kernel,
        out_shape=jax.ShapeDtypeStruct((s, n_out), F32),
        grid=(n_out // tn, s // tm),
        in_specs=[pl.BlockSpec((tm, k), lambda n, m: (m, 0)),
                  pl.BlockSpec((None, k, tn), lambda n, m: (layer, 0, n)),
                  pl.BlockSpec((tm, tn), lambda n, m: (m, n))],
        out_specs=pl.BlockSpec((tm, tn), lambda n, m: (m, n)),
        scratch_shapes=[pltpu.VMEM((k, tn), BF16)],
        compiler_params=_params("arbitrary", "arbitrary"),
        name=name,
    )(a, w, x)


FFN_TN = 512


def _ffn_up_kernel(h_ref, wg_ref, wu_ref, o_ref, wgbf_ref, wubf_ref):
    @pl.when(pl.program_id(1) == 0)
    def _():
        wgbf_ref[...] = wg_ref[...].astype(BF16)
        wubf_ref[...] = wu_ref[...].astype(BF16)
    for r in range(0, h_ref.shape[0], SUB_ROWS):
        rows = slice(r, r + SUB_ROWS)
        gt = jnp.dot(h_ref[rows, :], wgbf_ref[...], preferred_element_type=F32)
        up = jnp.dot(h_ref[rows, :], wubf_ref[...], preferred_element_type=F32)
        o_ref[rows, :] = (jax.nn.silu(gt) * up).astype(o_ref.dtype)


def _ffn_up(h, w_gate_up, layer):
    s = h.shape[0]
    tn = FFN_TN
    nt = D_FF // tn
    return pl.pallas_call(
        _ffn_up_kernel,
        out_shape=jax.ShapeDtypeStruct((s, D_FF), BF16),
        grid=(nt, s // TM),
        in_specs=[pl.BlockSpec((TM, D_MODEL), lambda n, m: (m, 0)),
                  pl.BlockSpec((None, D_MODEL, tn), lambda n, m: (layer, 0, n)),
                  pl.BlockSpec((None, D_MODEL, tn), lambda n, m: (layer, 0, nt + n))],
        out_specs=pl.BlockSpec((TM, tn), lambda n, m: (m, n)),
        scratch_shapes=[pltpu.VMEM((D_MODEL, tn), BF16), pltpu.VMEM((D_MODEL, tn), BF16)],
        compiler_params=_params("arbitrary", "arbitrary"),
        name="ffn_up",
    )(h, w_gate_up, w_gate_up)


def kernel(x, mem, g_mix, w_in, gq_a, gk_a, g_sgu, w_sgu, b_sgu, gq_m, gk_m, g_mem,
           w_mem_kv, w_branch, w_out, g_ffn, w_gate_up, w_down):
    b, s, d = x.shape
    assert (b, s, d) == (1, SEQ, D_MODEL) and mem.shape == (1, N_MEM, D_MODEL)
    x2 = x.reshape(s, d)
    mem2 = mem.reshape(N_MEM, d)

    def row(p):
        return p.reshape(DEPTH, 1, p.shape[-1])

    g_mix3, g_ffn3, g_mem3, g_sgu3 = row(g_mix), row(g_ffn), row(g_mem), row(g_sgu)
    gq_a3, gk_a3, gq_m3, gk_m3 = row(gq_a), row(gk_a), row(gq_m), row(gk_m)
    b_sgu_t = jnp.swapaxes(b_sgu, 1, 2)
    slopes = 2.0 ** (-8.0 * jnp.arange(1, N_HEADS_A + 1, dtype=F32) / N_HEADS_A)
    slopes = jnp.broadcast_to(slopes[:, None, None], (N_HEADS_A, 1, MOBA_BLOCK))

    for layer in range(DEPTH):
        h = _rmsnorm_bf16(x2, g_mix3, layer)
        qkv, kmeans = _qkv_proj(h, w_in, gq_a3, gk_a3, layer)
        kmean = kmeans[1].reshape(N_BLOCKS, WIDTH_A)
        uv = _uv_proj(h, w_in, g_sgu3, layer)
        kvm = _memkv_proj(mem2, g_mem3, w_mem_kv, gk_m3, layer)
        ym = _qm_memattn(h, w_in, gq_m3, kvm, layer)
        gates = _gates_proj(h, w_in, layer)
        ya = _moba(qkv, kmean, slopes)
        yb = _sgu(uv, w_sgu, b_sgu_t, layer)
        merged = _merge(ya, yb, ym, gates, w_branch, layer)
        x2 = _resid_proj(merged, w_out, x2, layer, TM, TN, "out_proj")
        hf = _rmsnorm_bf16(x2, g_ffn3, layer)
        act = _ffn_up(hf, w_gate_up, layer)
        x2 = _resid_proj(act, w_down, x2, layer, 512, 512, "ffn_down")
    return x2.reshape(b, s, d)
```

```python
import functools

import jax
import jax.numpy as jnp
from jax import lax
from jax.experimental import pallas as pl
from jax.experimental.pallas import tpu as pltpu

F32 = jnp.float32
BF16 = jnp.bfloat16

D_MODEL = 2048
SEQ = 8192
DEPTH = 2
N_HEADS_A = 8
HEAD_DIM_A = 128
WIDTH_A = N_HEADS_A * HEAD_DIM_A
MOBA_BLOCK = 256
MOBA_TOPK = 3
N_BLOCKS = SEQ // MOBA_BLOCK
SGU_WIDTH = 1024
SGU_GROUPS = 8
SGU_CHUNK = 128
N_MEM = 256
N_HEADS_M = 4
HEAD_DIM_M = 256
WIDTH_M = N_HEADS_M * HEAD_DIM_M
D_FF = 5632
NEG_INF = -1e30
EPS = 1e-6

COL_QKV = 0
COL_UV = 3 * WIDTH_A
COL_QM = COL_UV + 2 * SGU_WIDTH
COL_GATES = COL_QM + WIDTH_M

VMEM_LIMIT_BYTES = 60 * 1024 * 1024

TM = 1024
TM_BIG = 2048
TN = 1024
SUB_ROWS = 256


def _params(*semantics):
    return pltpu.CompilerParams(dimension_semantics=semantics,
                                vmem_limit_bytes=VMEM_LIMIT_BYTES)


def _cast_weight_once(w_ref, wbf_ref):
    @pl.when(pl.program_id(1) == 0)
    def _():
        wbf_ref[...] = w_ref[...].astype(BF16)


def _group_rmsnorm(a, g, width):
    outs = []
    for s in range(0, a.shape[1], width):
        blk = a[:, s:s + width]
        ms = jnp.mean(blk * blk, axis=-1, keepdims=True)
        outs.append(blk * lax.rsqrt(ms + EPS) * g)
    return jnp.concatenate(outs, axis=1)


def _norm_kernel(x_ref, g_ref, o_ref):
    x = x_ref[...]
    ms = jnp.mean(x * x, axis=-1, keepdims=True)
    o_ref[...] = (x * lax.rsqrt(ms + EPS) * g_ref[...]).astype(o_ref.dtype)


def _rmsnorm_bf16(x, g_all, layer):
    tm = 512
    return pl.pallas_call(
        _norm_kernel,
        out_shape=jax.ShapeDtypeStruct(x.shape, BF16),
        grid=(x.shape[0] // tm,),
        in_specs=[pl.BlockSpec((tm, D_MODEL), lambda m: (m, 0)),
                  pl.BlockSpec((None, 1, D_MODEL), lambda m: (layer, 0, 0))],
        out_specs=pl.BlockSpec((tm, D_MODEL), lambda m: (m, 0)),
        compiler_params=_params("arbitrary"),
        name="rmsnorm",
    )(x, g_all)


def _sub_dots(lhs_ref, rhs, emit):
    for r in range(0, lhs_ref.shape[0], SUB_ROWS):
        rows = slice(r, r + SUB_ROWS)
        emit(rows, jnp.dot(lhs_ref[rows, :], rhs, preferred_element_type=F32))


def _qkv_kernel(h_ref, w_ref, gq_ref, gk_ref, o_ref, kmean_ref, wbf_ref):
    n = pl.program_id(0)
    _cast_weight_once(w_ref, wbf_ref)
    assert SUB_ROWS == MOBA_BLOCK

    def emit(per_head):
        def emit_rows(rows, acc):
            b = rows.start // MOBA_BLOCK
            for c in range(0, TN, HEAD_DIM_A):
                a = per_head(acc[:, c:c + HEAD_DIM_A])
                o_ref[rows, c:c + HEAD_DIM_A] = a.astype(o_ref.dtype)
                kmean_ref[b:b + 1, c:c + HEAD_DIM_A] = jnp.mean(a, axis=0, keepdims=True)
        _sub_dots(h_ref, wbf_ref[...], emit_rows)

    def head_norm(a, g):
        ms = jnp.mean(a * a, axis=-1, keepdims=True)
        return a * lax.rsqrt(ms + EPS) * g

    @pl.when(n == 0)
    def _():
        emit(lambda a: head_norm(a, gq_ref[...]) * HEAD_DIM_A ** -0.5)

    @pl.when(n == 1)
    def _():
        emit(lambda a: head_norm(a, gk_ref[...]))

    @pl.when(n == 2)
    def _():
        emit(lambda a: a)


def _qkv_proj(h, w_in, gq_a, gk_a, layer):
    s = h.shape[0]
    tm = TM_BIG
    return pl.pallas_call(
        _qkv_kernel,
        out_shape=(jax.ShapeDtypeStruct((3, s, WIDTH_A), BF16),
                   jax.ShapeDtypeStruct((3, s // tm, tm // MOBA_BLOCK, WIDTH_A), F32)),
        grid=(3, s // tm),
        in_specs=[pl.BlockSpec((tm, D_MODEL), lambda n, m: (m, 0)),
                  pl.BlockSpec((None, D_MODEL, TN), lambda n, m: (layer, 0, n)),
                  pl.BlockSpec((None, 1, HEAD_DIM_A), lambda n, m: (layer, 0, 0)),
                  pl.BlockSpec((None, 1, HEAD_DIM_A), lambda n, m: (layer, 0, 0))],
        out_specs=(pl.BlockSpec((None, tm, WIDTH_A), lambda n, m: (n, m, 0)),
                   pl.BlockSpec((None, None, tm // MOBA_BLOCK, WIDTH_A),
                                lambda n, m: (n, m, 0, 0))),
        scratch_shapes=[pltpu.VMEM((D_MODEL, TN), BF16)],
        compiler_params=_params("arbitrary", "arbitrary"),
        name="qkv_proj",
    )(h, w_in, gq_a, gk_a)


def _uv_kernel(h_ref, w_ref, g_ref, o_ref, wbf_ref):
    n = pl.program_id(0)
    _cast_weight_once(w_ref, wbf_ref)

    @pl.when(n == 0)
    def _():
        def emit(rows, acc):
            o_ref[rows, :] = jax.nn.gelu(acc)
        _sub_dots(h_ref, wbf_ref[...], emit)

    @pl.when(n == 1)
    def _():
        def emit(rows, acc):
            a = jax.nn.gelu(acc)
            ms = jnp.mean(a * a, axis=-1, keepdims=True)
            o_ref[rows, :] = a * lax.rsqrt(ms + EPS) * g_ref[...]
        _sub_dots(h_ref, wbf_ref[...], emit)


def _uv_proj(h, w_in, g_sgu, layer):
    s = h.shape[0]
    off = COL_UV // TN
    return pl.pallas_call(
        _uv_kernel,
        out_shape=jax.ShapeDtypeStruct((2, s, SGU_WIDTH), F32),
        grid=(2, s // TM),
        in_specs=[pl.BlockSpec((TM, D_MODEL), lambda n, m: (m, 0)),
                  pl.BlockSpec((None, D_MODEL, TN), lambda n, m: (layer, 0, off + n)),
                  pl.BlockSpec((None, 1, SGU_WIDTH), lambda n, m: (layer, 0, 0))],
        out_specs=pl.BlockSpec((None, TM, SGU_WIDTH), lambda n, m: (n, m, 0)),
        scratch_shapes=[pltpu.VMEM((D_MODEL, TN), BF16)],
        compiler_params=_params("arbitrary", "arbitrary"),
        name="uv_proj",
    )(h, w_in, g_sgu)


def _memkv_kernel(mem_ref, gmem_ref, w_ref, gk_ref, o_ref):
    n = pl.program_id(0)
    x = mem_ref[...]
    ms = jnp.mean(x * x, axis=-1, keepdims=True)
    hm = (x * lax.rsqrt(ms + EPS) * gmem_ref[...]).astype(BF16)
    acc = jnp.dot(hm, w_ref[...].astype(BF16), preferred_element_type=F32)

    @pl.when(n == 0)
    def _():
        o_ref[...] = _group_rmsnorm(acc, gk_ref[...], HEAD_DIM_M).astype(o_ref.dtype)

    @pl.when(n == 1)
    def _():
        o_ref[...] = acc.astype(o_ref.dtype)


def _memkv_proj(mem, g_mem, w_mem_kv, gk_m, layer):
    return pl.pallas_call(
        _memkv_kernel,
        out_shape=jax.ShapeDtypeStruct((2, N_MEM, WIDTH_M), BF16),
        grid=(2,),
        in_specs=[pl.BlockSpec((N_MEM, D_MODEL), lambda n: (0, 0)),
                  pl.BlockSpec((None, 1, D_MODEL), lambda n: (layer, 0, 0)),
                  pl.BlockSpec((None, D_MODEL, WIDTH_M), lambda n: (layer, 0, n)),
                  pl.BlockSpec((None, 1, HEAD_DIM_M), lambda n: (layer, 0, 0))],
        out_specs=pl.BlockSpec((None, N_MEM, WIDTH_M), lambda n: (n, 0, 0)),
        compiler_params=_params("arbitrary"),
        name="memkv_proj",
    )(mem, g_mem, w_mem_kv, gk_m)


def _qm_kernel(h_ref, w_ref, gq_ref, km_ref, vm_ref, o_ref, wbf_ref):
    _cast_weight_once(w_ref, wbf_ref)

    def emit(rows, acc):
        for hh in range(N_HEADS_M):
            cols = slice(hh * HEAD_DIM_M, (hh + 1) * HEAD_DIM_M)
            qh = acc[:, cols]
            ms = jnp.mean(qh * qh, axis=-1, keepdims=True)
            qn = (qh * lax.rsqrt(ms + EPS) * gq_ref[...]).astype(BF16)
            sc = lax.dot_general(qn, km_ref[:, cols], (((1,), (1,)), ((), ())),
                                 preferred_element_type=F32) * HEAD_DIM_M ** -0.5
            mx = jnp.max(sc, axis=-1, keepdims=True)
            e = jnp.exp(sc - mx)
            p = (e / jnp.sum(e, axis=-1, keepdims=True)).astype(BF16)
            o_ref[rows, cols] = jnp.dot(
                p, vm_ref[:, cols], preferred_element_type=F32).astype(o_ref.dtype)

    _sub_dots(h_ref, wbf_ref[...], emit)


def _qm_memattn(h, w_in, gq_m, kvm, layer):
    s = h.shape[0]
    off = COL_QM // TN
    tm = TM_BIG
    return pl.pallas_call(
        _qm_kernel,
        out_shape=jax.ShapeDtypeStruct((s, WIDTH_M), BF16),
        grid=(1, s // tm),
        in_specs=[pl.BlockSpec((tm, D_MODEL), lambda n, m: (m, 0)),
                  pl.BlockSpec((None, D_MODEL, TN), lambda n, m: (layer, 0, off)),
                  pl.BlockSpec((None, 1, HEAD_DIM_M), lambda n, m: (layer, 0, 0)),
                  pl.BlockSpec((None, N_MEM, WIDTH_M), lambda n, m: (0, 0, 0)),
                  pl.BlockSpec((None, N_MEM, WIDTH_M), lambda n, m: (1, 0, 0))],
        out_specs=pl.BlockSpec((tm, WIDTH_M), lambda n, m: (m, 0)),
        scratch_shapes=[pltpu.VMEM((D_MODEL, TN), BF16)],
        compiler_params=_params("arbitrary", "arbitrary"),
        name="qm_memattn",
    )(h, w_in, gq_m, kvm, kvm)


def _gates_kernel(h_ref, w_ref, o_ref, wbf_ref):
    _cast_weight_once(w_ref, wbf_ref)
    def emit(rows, acc):
        o_ref[rows, :] = jax.nn.sigmoid(acc)

    _sub_dots(h_ref, wbf_ref[...], emit)


def _gates_proj(h, w_in, layer):
    s = h.shape[0]
    off = COL_GATES // TN
    n_tiles = 3 * D_MODEL // TN
    return pl.pallas_call(
        _gates_kernel,
        out_shape=jax.ShapeDtypeStruct((s, 3 * D_MODEL), F32),
        grid=(n_tiles, s // TM),
        in_specs=[pl.BlockSpec((TM, D_MODEL), lambda n, m: (m, 0)),
                  pl.BlockSpec((None, D_MODEL, TN), lambda n, m: (layer, 0, off + n))],
        out_specs=pl.BlockSpec((TM, TN), lambda n, m: (m, n)),
        scratch_shapes=[pltpu.VMEM((D_MODEL, TN), BF16)],
        compiler_params=_params("arbitrary", "arbitrary"),
        name="gates_proj",
    )(h, w_in)


COL_KEY_OFF = N_BLOCKS
COL_BLK_OFF = N_BLOCKS + 1
COL_ONES = N_BLOCKS + 2
MOBA_HALF = 2
MOBA_HEADS = 2
SUM_ROWS = 16
M_INIT = -3e38


def _moba_kernel(q_ref, k_ref, v_ref, kmean_ref, slope_ref, o_ref, kaug_ref, vt_ref, s_ref):
    i = pl.program_id(1)
    blk, dh = MOBA_BLOCK, HEAD_DIM_A
    heads = range(MOBA_HEADS)
    slopes = [slope_ref[hh] for hh in heads]

    @pl.when(i == 0)
    def _():
        lane = lax.broadcasted_iota(jnp.int32, (blk, dh), 1)
        lane_row = lax.broadcasted_iota(jnp.int32, (1, dh), 1)
        key_off = lax.broadcasted_iota(jnp.int32, (blk, dh), 0).astype(F32)

        def build(b, carry):
            rows = pl.ds(pl.multiple_of(b * blk, blk), blk)
            blk_start = jnp.full((1, dh), b * blk, jnp.int32).astype(F32)
            onehot = jnp.where(lane == b, 1.0, jnp.where(lane == COL_ONES, 1.0, 0.0))
            for hh in heads:
                sl = slopes[hh][:, :dh]
                extra = (onehot + jnp.where(lane == COL_KEY_OFF, sl * key_off, 0.0)
                         + jnp.where(lane_row == COL_BLK_OFF, sl * blk_start, 0.0))
                kaug_ref[hh, rows, 0:dh] = k_ref[rows, hh * dh:(hh + 1) * dh]
                kaug_ref[hh, rows, dh:2 * dh] = extra.astype(BF16)
                vt_ref[hh, b, 0:dh] = (
                    v_ref[rows, hh * dh:(hh + 1) * dh].astype(F32).T.astype(BF16))
                vt_ref[hh, b, dh:dh + SUM_ROWS] = jnp.ones((SUM_ROWS, blk), BF16)
            return carry

        lax.fori_loop(0, N_BLOCKS, build, 0)

    own = pl.ds(pl.multiple_of(i * blk, blk), blk)
    key = lax.broadcasted_iota(jnp.int32, (blk, blk), 0)
    qry = lax.broadcasted_iota(jnp.int32, (blk, blk), 1)
    causal = key <= qry
    key_f = key.astype(F32)
    blk_id = lax.broadcasted_iota(jnp.int32, (N_BLOCKS, blk), 0)
    blk_id_f = blk_id.astype(F32)
    past = blk_id < i
    rest = lax.broadcasted_iota(jnp.int32, (dh - N_BLOCKS, blk), 0) + N_BLOCKS
    q_start = jnp.full((1, blk), i * blk, jnp.int32).astype(F32)

    q_t32 = [q_ref[:, hh * dh:(hh + 1) * dh].astype(F32).T for hh in heads]
    q_t = [a.astype(BF16) for a in q_t32]
    gates = [jnp.dot(kmean_ref[:, hh * dh:(hh + 1) * dh].astype(BF16), q_t[hh],
                     preferred_element_type=F32) for hh in heads]
    own_s = [jnp.dot(kaug_ref[hh, own, 0:dh], q_t[hh], preferred_element_type=F32)
             for hh in heads]
    own_state = []
    for hh in heads:
        s = jnp.where(causal, own_s[hh] + slopes[hh] * key_f, NEG_INF)
        m_own = jnp.max(s, axis=0, keepdims=True)
        p = jnp.exp(s - m_own)
        own_state.append((m_own, jnp.dot(vt_ref[hh, i], p.astype(BF16),
                                         preferred_element_type=F32)))

    q_aug_t = []
    for hh in heads:
        gate = jnp.where(past, gates[hh], NEG_INF)
        pen = jnp.full((N_BLOCKS, blk), NEG_INF, F32)
        for _ in range(MOBA_TOPK):
            mx = jnp.max(gate, axis=0, keepdims=True)
            first = jnp.min(jnp.where(gate == mx, blk_id_f, float(N_BLOCKS)),
                            axis=0, keepdims=True)
            hit = blk_id_f == first
            pen = jnp.where(jnp.logical_and(hit, mx > 0.5 * NEG_INF), 0.0, pen)
            gate = jnp.where(hit, -3e38, gate)
        extra_q = jnp.where(rest == COL_ONES, -slopes[hh] * q_start,
                            jnp.where(rest < COL_ONES, 1.0, 0.0))
        q_aug_t.append(jnp.concatenate([q_t32[hh], pen, extra_q], axis=0).astype(BF16))

    half = MOBA_HALF
    half_rows = half * blk

    def score(hh, first_blk):
        rows = pl.ds(pl.multiple_of(first_blk * blk, half_rows), half_rows)
        return jnp.dot(kaug_ref[hh, rows, :], q_aug_t[hh], preferred_element_type=F32)

    def attend(hh, st, s, first_blk):
        m, acc = st
        m_new = jnp.maximum(m, jnp.max(s, axis=0, keepdims=True))
        alpha = jnp.exp(m - m_new)
        p = jnp.exp(s - m_new)
        v_t = jnp.concatenate([vt_ref[hh, first_blk + g] for g in range(half)], axis=1)
        acc = alpha * acc + jnp.dot(v_t, p.astype(BF16), preferred_element_type=F32)
        return m_new, acc

    for hh in heads:
        s_ref[hh] = score(hh, 0)

    state = [(jnp.full((1, blk), M_INIT, F32), jnp.zeros((dh + SUM_ROWS, blk), F32))
             for _ in heads]

    def body(t, carry):
        first = t * (2 * half)
        nxt = jnp.minimum(first + 2 * half, N_BLOCKS - half)
        s_second = [score(hh, first + half) for hh in heads]
        carry = [attend(hh, carry[hh], s_ref[hh], first) for hh in heads]
        for hh in heads:
            s_ref[hh] = score(hh, nxt)
        carry = [attend(hh, carry[hh], s_second[hh], first + half) for hh in heads]
        return tuple(carry)

    n_trips = (i + 2 * half - 1) // (2 * half)
    final = lax.fori_loop(0, n_trips, body, tuple(state))
    for hh in heads:
        m_past, acc_past = final[hh]
        m_own, acc_own = own_state[hh]
        m = jnp.maximum(m_past, m_own)
        acc = jnp.exp(m_past - m) * acc_past + jnp.exp(m_own - m) * acc_own
        out = acc[0:dh] / acc[dh:dh + 1]
        o_ref[:, hh * dh:(hh + 1) * dh] = out.T.astype(o_ref.dtype)


def _moba(qkv, kmean, slopes):
    s = qkv.shape[1]
    width = MOBA_HEADS * HEAD_DIM_A
    return pl.pallas_call(
        _moba_kernel,
        out_shape=jax.ShapeDtypeStruct((s, WIDTH_A), BF16),
        grid=(N_HEADS_A // MOBA_HEADS, s // MOBA_BLOCK),
        in_specs=[pl.BlockSpec((None, MOBA_BLOCK, width), lambda h, i: (0, i, h)),
                  pl.BlockSpec((None, s, width), lambda h, i: (1, 0, h),
                               pipeline_mode=pl.Buffered(1)),
                  pl.BlockSpec((None, s, width), lambda h, i: (2, 0, h),
                               pipeline_mode=pl.Buffered(1)),
                  pl.BlockSpec((N_BLOCKS, width), lambda h, i: (0, h)),
                  pl.BlockSpec((MOBA_HEADS, 1, MOBA_BLOCK), lambda h, i: (h, 0, 0))],
        out_specs=pl.BlockSpec((MOBA_BLOCK, width), lambda h, i: (i, h)),
        scratch_shapes=[pltpu.VMEM((MOBA_HEADS, s, 2 * HEAD_DIM_A), BF16),
                        pltpu.VMEM((MOBA_HEADS, N_BLOCKS, HEAD_DIM_A + SUM_ROWS, MOBA_BLOCK),
                                   BF16),
                        pltpu.VMEM((MOBA_HEADS, MOBA_HALF * MOBA_BLOCK, MOBA_BLOCK), F32)],
        compiler_params=_params("arbitrary", "arbitrary"),
        name="moba_attention",
    )(qkv, qkv, qkv, kmean, slopes)


def _sgu_kernel(u_ref, v_ref, w_ref, bt_ref, o_ref):
    tm = u_ref.shape[0]
    row = lax.broadcasted_iota(jnp.int32, (SGU_CHUNK, SGU_CHUNK), 0)
    col = lax.broadcasted_iota(jnp.int32, (SGU_CHUNK, SGU_CHUNK), 1)
    for g in range(SGU_GROUPS):
        cols = slice(g * SGU_CHUNK, (g + 1) * SGU_CHUNK)
        w = jnp.where(col <= row, w_ref[g], 0.0).astype(BF16)
        bias = bt_ref[:, g:g + 1]
        for c in range(tm // SGU_CHUNK):
            rows = slice(c * SGU_CHUNK, (c + 1) * SGU_CHUNK)
            mixed = jnp.dot(w, v_ref[rows, cols].astype(BF16),
                            preferred_element_type=F32) + bias
            o_ref[rows, cols] = (u_ref[rows, cols] * mixed).astype(o_ref.dtype)


def _sgu(uv, w_sgu, b_sgu_t, layer):
    s = uv.shape[1]
    tm = 512
    return pl.pallas_call(
        _sgu_kernel,
        out_shape=jax.ShapeDtypeStruct((s, SGU_WIDTH), BF16),
        grid=(s // tm,),
        in_specs=[pl.BlockSpec((None, tm, SGU_WIDTH), lambda m: (0, m, 0)),
                  pl.BlockSpec((None, tm, SGU_WIDTH), lambda m: (1, m, 0)),
                  pl.BlockSpec((None, SGU_GROUPS, SGU_CHUNK, SGU_CHUNK),
                               lambda m: (layer, 0, 0, 0)),
                  pl.BlockSpec((None, SGU_CHUNK, SGU_GROUPS), lambda m: (layer, 0, 0))],
        out_specs=pl.BlockSpec((tm, SGU_WIDTH), lambda m: (m, 0)),
        compiler_params=_params("arbitrary"),
        name="sgu",
    )(uv, uv, w_sgu, b_sgu_t)


MERGE_TN = 512


def _merge_kernel(ya_ref, yb_ref, ym_ref, ga_ref, gb_ref, gm_ref, w_ref, o_ref, wbf_ref):
    _cast_weight_once(w_ref, wbf_ref)
    for r in range(0, ya_ref.shape[0], SUB_ROWS):
        rows = slice(r, r + SUB_ROWS)
        merged = ga_ref[rows, :] * jnp.dot(ya_ref[rows, :], wbf_ref[0],
                                           preferred_element_type=F32)
        merged += gb_ref[rows, :] * jnp.dot(yb_ref[rows, :], wbf_ref[1],
                                            preferred_element_type=F32)
        merged += gm_ref[rows, :] * jnp.dot(ym_ref[rows, :], wbf_ref[2],
                                            preferred_element_type=F32)
        o_ref[rows, :] = merged.astype(o_ref.dtype)


def _merge(ya, yb, ym, gates, w_branch, layer):
    s = ya.shape[0]
    tn = MERGE_TN
    nt = D_MODEL // tn
    act = pl.BlockSpec((TM, WIDTH_A), lambda n, m: (m, 0))
    return pl.pallas_call(
        _merge_kernel,
        out_shape=jax.ShapeDtypeStruct((s, D_MODEL), BF16),
        grid=(nt, s // TM),
        in_specs=[act, act, act,
                  pl.BlockSpec((TM, tn), lambda n, m: (m, n)),
                  pl.BlockSpec((TM, tn), lambda n, m: (m, nt + n)),
                  pl.BlockSpec((TM, tn), lambda n, m: (m, 2 * nt + n)),
                  pl.BlockSpec((None, 3, WIDTH_A, tn), lambda n, m: (layer, 0, 0, n))],
        out_specs=pl.BlockSpec((TM, tn), lambda n, m: (m, n)),
        scratch_shapes=[pltpu.VMEM((3, WIDTH_A, tn), BF16)],
        compiler_params=_params("arbitrary", "arbitrary"),
        name="branch_merge",
    )(ya, yb, ym, gates, gates, gates, w_branch)


def _resid_proj_kernel(a_ref, w_ref, x_ref, o_ref, wbf_ref):
    _cast_weight_once(w_ref, wbf_ref)
    def emit(rows, acc):
        o_ref[rows, :] = x_ref[rows, :] + acc

    _sub_dots(a_ref, wbf_ref[...], emit)


def _resid_proj(a, w, x, layer, tm, tn, name):
    s, k = a.shape
    n_out = w.shape[2]
    return pl.pallas_call(
        _resid_proj_kernel,
        out_shape=jax.ShapeDtypeStruct((s, n_out), F32),
        grid=(n_out // tn, s // tm),
        in_specs=[pl.BlockSpec((tm, k), lambda n, m: (m, 0)),
                  pl.BlockSpec((None, k, tn), lambda n, m: (layer, 0, n)),
                  pl.BlockSpec((tm, tn), lambda n, m: (m, n))],
        out_specs=pl.BlockSpec((tm, tn), lambda n, m: (m, n)),
        scratch_shapes=[pltpu.VMEM((k, tn), BF16)],
        compiler_params=_params("arbitrary", "arbitrary"),
        name=name,
    )(a, w, x)


def _out_proj_norm_kernel(a_ref, w_ref, x_ref, g_ref, o_ref, h_ref, wbf_ref):
    @pl.when(pl.program_id(0) == 0)
    def _():
        wbf_ref[...] = w_ref[...].astype(BF16)

    def emit(rows, acc):
        x = x_ref[rows, :] + acc
        o_ref[rows, :] = x
        ms = jnp.mean(x * x, axis=-1, keepdims=True)
        h_ref[rows, :] = (x * lax.rsqrt(ms + EPS) * g_ref[...]).astype(h_ref.dtype)

    _sub_dots(a_ref, wbf_ref[...], emit)


def _out_proj_norm(a, w, x, g_all, layer):
    s, k = a.shape
    tm = 512
    return pl.pallas_call(
        _out_proj_norm_kernel,
        out_shape=(jax.ShapeDtypeStruct((s, D_MODEL), F32),
                   jax.ShapeDtypeStruct((s, D_MODEL), BF16)),
        grid=(s // tm,),
        in_specs=[pl.BlockSpec((tm, k), lambda m: (m, 0)),
                  pl.BlockSpec((None, k, D_MODEL), lambda m: (layer, 0, 0),
                               pipeline_mode=pl.Buffered(1)),
                  pl.BlockSpec((tm, D_MODEL), lambda m: (m, 0)),
                  pl.BlockSpec((None, 1, D_MODEL), lambda m: (layer, 0, 0))],
        out_specs=(pl.BlockSpec((tm, D_MODEL), lambda m: (m, 0)),
                   pl.BlockSpec((tm, D_MODEL), lambda m: (m, 0))),
        scratch_shapes=[pltpu.VMEM((k, D_MODEL), BF16)],
        compiler_params=_params("arbitrary"),
        name="out_proj_norm",
    )(a, w, x, g_all)


FFN_TN = 512


def _ffn_up_kernel(h_ref, wg_ref, wu_ref, o_ref, wgbf_ref, wubf_ref):
    @pl.when(pl.program_id(1) == 0)
    def _():
        wgbf_ref[...] = wg_ref[...].astype(BF16)
        wubf_ref[...] = wu_ref[...].astype(BF16)
    for r in range(0, h_ref.shape[0], SUB_ROWS):
        rows = slice(r, r + SUB_ROWS)
        gt = jnp.dot(h_ref[rows, :], wgbf_ref[...], preferred_element_type=F32)
        up = jnp.dot(h_ref[rows, :], wubf_ref[...], preferred_element_type=F32)
        o_ref[rows, :] = (jax.nn.silu(gt) * up).astype(o_ref.dtype)


def _ffn_up(h, w_gate_up, layer):
    s = h.shape[0]
    tn = FFN_TN
    nt = D_FF // tn
    tm = TM_BIG
    return pl.pallas_call(
        _ffn_up_kernel,
        out_shape=jax.ShapeDtypeStruct((s, D_FF), BF16),
        grid=(nt, s // tm),
        in_specs=[pl.BlockSpec((tm, D_MODEL), lambda n, m: (m, 0)),
                  pl.BlockSpec((None, D_MODEL, tn), lambda n, m: (layer, 0, n)),
                  pl.BlockSpec((None, D_MODEL, tn), lambda n, m: (layer, 0, nt + n))],
        out_specs=pl.BlockSpec((tm, tn), lambda n, m: (m, n)),
        scratch_shapes=[pltpu.VMEM((D_MODEL, tn), BF16), pltpu.VMEM((D_MODEL, tn), BF16)],
        compiler_params=_params("arbitrary", "arbitrary"),
        name="ffn_up",
    )(h, w_gate_up, w_gate_up)


def kernel(x, mem, g_mix, w_in, gq_a, gk_a, g_sgu, w_sgu, b_sgu, gq_m, gk_m, g_mem,
           w_mem_kv, w_branch, w_out, g_ffn, w_gate_up, w_down):
    b, s, d = x.shape
    assert (b, s, d) == (1, SEQ, D_MODEL) and mem.shape == (1, N_MEM, D_MODEL)
    x2 = x.reshape(s, d)
    mem2 = mem.reshape(N_MEM, d)

    def row(p):
        return p.reshape(DEPTH, 1, p.shape[-1])

    g_mix3, g_ffn3, g_mem3, g_sgu3 = row(g_mix), row(g_ffn), row(g_mem), row(g_sgu)
    gq_a3, gk_a3, gq_m3, gk_m3 = row(gq_a), row(gk_a), row(gq_m), row(gk_m)
    b_sgu_t = jnp.swapaxes(b_sgu, 1, 2)
    slopes = 2.0 ** (-8.0 * jnp.arange(1, N_HEADS_A + 1, dtype=F32) / N_HEADS_A)
    slopes = jnp.broadcast_to(slopes[:, None, None], (N_HEADS_A, 1, MOBA_BLOCK))

    for layer in range(DEPTH):
        h = _rmsnorm_bf16(x2, g_mix3, layer)
        qkv, kmeans = _qkv_proj(h, w_in, gq_a3, gk_a3, layer)
        kmean = kmeans[1].reshape(N_BLOCKS, WIDTH_A)
        uv = _uv_proj(h, w_in, g_sgu3, layer)
        kvm = _memkv_proj(mem2, g_mem3, w_mem_kv, gk_m3, layer)
        ym = _qm_memattn(h, w_in, gq_m3, kvm, layer)
        gates = _gates_proj(h, w_in, layer)
        ya = _moba(qkv, kmean, slopes)
        yb = _sgu(uv, w_sgu, b_sgu_t, layer)
        merged = _merge(ya, yb, ym, gates, w_branch, layer)
        x2, hf = _out_proj_norm(merged, w_out, x2, g_ffn3, layer)
        act = _ffn_up(hf, w_gate_up, layer)
        x2 = _resid_proj(act, w_down, x2, layer, 512, 512, "ffn_down")
    return x2.reshape(b, s, d)
```

```python
import functools

import jax
import jax.numpy as jnp
from jax import lax
from jax.experimental import pallas as pl
from jax.experimental.pallas import tpu as pltpu

F32 = jnp.float32
BF16 = jnp.bfloat16

D_MODEL = 2048
SEQ = 8192
DEPTH = 2
N_HEADS_A = 8
HEAD_DIM_A = 128
WIDTH_A = N_HEADS_A * HEAD_DIM_A
MOBA_BLOCK = 256
MOBA_TOPK = 3
N_BLOCKS = SEQ // MOBA_BLOCK
SGU_WIDTH = 1024
SGU_GROUPS = 8
SGU_CHUNK = 128
N_MEM = 256
N_HEADS_M = 4
HEAD_DIM_M = 256
WIDTH_M = N_HEADS_M * HEAD_DIM_M
D_FF = 5632
NEG_INF = -1e30
EPS = 1e-6

COL_QKV = 0
COL_UV = 3 * WIDTH_A
COL_QM = COL_UV + 2 * SGU_WIDTH
COL_GATES = COL_QM + WIDTH_M

VMEM_LIMIT_BYTES = 60 * 1024 * 1024

TM = 1024
TM_BIG = 2048
TN = 1024
SUB_ROWS = 256


def _params(*semantics):
    return pltpu.CompilerParams(dimension_semantics=semantics,
                                vmem_limit_bytes=VMEM_LIMIT_BYTES)


def _cast_weight_once(w_ref, wbf_ref):
    @pl.when(pl.program_id(1) == 0)
    def _():
        wbf_ref[...] = w_ref[...].astype(BF16)


def _group_rmsnorm(a, g, width):
    outs = []
    for s in range(0, a.shape[1], width):
        blk = a[:, s:s + width]
        ms = jnp.mean(blk * blk, axis=-1, keepdims=True)
        outs.append(blk * lax.rsqrt(ms + EPS) * g)
    return jnp.concatenate(outs, axis=1)


def _norm_kernel(x_ref, g_ref, o_ref):
    x = x_ref[...]
    ms = jnp.mean(x * x, axis=-1, keepdims=True)
    o_ref[...] = (x * lax.rsqrt(ms + EPS) * g_ref[...]).astype(o_ref.dtype)


def _rmsnorm_bf16(x, g_all, layer):
    tm = 512
    return pl.pallas_call(
        _norm_kernel,
        out_shape=jax.ShapeDtypeStruct(x.shape, BF16),
        grid=(x.shape[0] // tm,),
        in_specs=[pl.BlockSpec((tm, D_MODEL), lambda m: (m, 0)),
                  pl.BlockSpec((None, 1, D_MODEL), lambda m: (layer, 0, 0))],
        out_specs=pl.BlockSpec((tm, D_MODEL), lambda m: (m, 0)),
        compiler_params=_params("arbitrary"),
        name="rmsnorm",
    )(x, g_all)


def _sub_dots(lhs_ref, rhs, emit):
    for r in range(0, lhs_ref.shape[0], SUB_ROWS):
        rows = slice(r, r + SUB_ROWS)
        emit(rows, jnp.dot(lhs_ref[rows, :], rhs, preferred_element_type=F32))


COL_KEY_OFF = N_BLOCKS
COL_BLK_OFF = N_BLOCKS + 1
COL_ONES = N_BLOCKS + 2
SUM_ROWS = 16


def _head_norm(a, g):
    ms = jnp.mean(a * a, axis=-1, keepdims=True)
    return a * lax.rsqrt(ms + EPS) * g


def _k_kernel(h_ref, w_ref, gk_ref, slope_ref, kaug_ref, kmean_ref, wbf_ref):
    m = pl.program_id(1)
    _cast_weight_once(w_ref, wbf_ref)
    assert SUB_ROWS == MOBA_BLOCK
    blk, dh = MOBA_BLOCK, HEAD_DIM_A
    lane = lax.broadcasted_iota(jnp.int32, (blk, dh), 1)
    lane_row = lax.broadcasted_iota(jnp.int32, (1, dh), 1)
    key_off = lax.broadcasted_iota(jnp.int32, (blk, dh), 0).astype(F32)

    def emit(rows, acc):
        b_local = rows.start // blk
        b = m * (h_ref.shape[0] // blk) + b_local
        blk_start = jnp.full((1, dh), b * blk, jnp.int32).astype(F32)
        onehot = jnp.where(lane == b, 1.0, jnp.where(lane == COL_ONES, 1.0, 0.0))
        for hh in range(N_HEADS_A):
            cols = slice(hh * dh, (hh + 1) * dh)
            kn = _head_norm(acc[:, cols], gk_ref[...])
            kmean_ref[b_local:b_local + 1, cols] = jnp.mean(kn, axis=0, keepdims=True)
            sl = slope_ref[hh][:, :dh]
            extra = (onehot + jnp.where(lane == COL_KEY_OFF, sl * key_off, 0.0)
                     + jnp.where(lane_row == COL_BLK_OFF, sl * blk_start, 0.0))
            kaug_ref[hh, rows, 0:dh] = kn.astype(BF16)
            kaug_ref[hh, rows, dh:2 * dh] = extra.astype(BF16)

    _sub_dots(h_ref, wbf_ref[...], emit)


def _q_kernel(h_ref, w_ref, gq_ref, kmean_ref, slope_ref, qaug_ref, wbf_ref):
    m = pl.program_id(1)
    _cast_weight_once(w_ref, wbf_ref)
    assert SUB_ROWS == MOBA_BLOCK
    blk, dh = MOBA_BLOCK, HEAD_DIM_A
    blk_id = lax.broadcasted_iota(jnp.int32, (N_BLOCKS, blk), 0)
    blk_id_f = blk_id.astype(F32)
    rest = lax.broadcasted_iota(jnp.int32, (dh - N_BLOCKS, blk), 0) + N_BLOCKS

    def emit(rows, acc):
        i = m * (h_ref.shape[0] // blk) + rows.start // blk
        past = blk_id < i
        q_start = jnp.full((1, blk), i * blk, jnp.int32).astype(F32)
        for hh in range(N_HEADS_A):
            cols = slice(hh * dh, (hh + 1) * dh)
            q = _head_norm(acc[:, cols], gq_ref[...]) * dh ** -0.5
            q_t = q.T
            gate = jnp.dot(kmean_ref[:, cols].astype(BF16), q_t.astype(BF16),
                           preferred_element_type=F32)
            gate = jnp.where(past, gate, NEG_INF)
            pen = jnp.full((N_BLOCKS, blk), NEG_INF, F32)
            for _ in range(MOBA_TOPK):
                mx = jnp.max(gate, axis=0, keepdims=True)
                first = jnp.min(jnp.where(gate == mx, blk_id_f, float(N_BLOCKS)),
                                axis=0, keepdims=True)
                hit = blk_id_f == first
                pen = jnp.where(jnp.logical_and(hit, mx > 0.5 * NEG_INF), 0.0, pen)
                gate = jnp.where(hit, -3e38, gate)
            pen = jnp.where(blk_id == i, 0.0, pen)
            extra_q = jnp.where(rest == COL_ONES, -slope_ref[hh] * q_start,
                                jnp.where(rest < COL_ONES, 1.0, 0.0))
            qaug_ref[hh, :, rows] = jnp.concatenate([q_t, pen, extra_q], axis=0).astype(BF16)

    _sub_dots(h_ref, wbf_ref[...], emit)


def _v_kernel(h_ref, w_ref, vt_ref, wbf_ref):
    _cast_weight_once(w_ref, wbf_ref)
    assert SUB_ROWS == MOBA_BLOCK
    blk, dh = MOBA_BLOCK, HEAD_DIM_A

    def emit(rows, acc):
        b_local = rows.start // blk
        for hh in range(N_HEADS_A):
            vt_ref[hh, b_local, 0:dh, :] = acc[:, hh * dh:(hh + 1) * dh].T.astype(BF16)
            vt_ref[hh, b_local, dh:dh + SUM_ROWS, :] = jnp.ones((SUM_ROWS, blk), BF16)

    _sub_dots(h_ref, wbf_ref[...], emit)


def _qkv_proj(h, w_in, gq_a, gk_a, slopes, layer):
    s = h.shape[0]
    tm, dh, nh = TM, HEAD_DIM_A, N_HEADS_A
    grid = (1, s // tm)
    h_spec = pl.BlockSpec((tm, D_MODEL), lambda n, m: (m, 0))
    gain = pl.BlockSpec((None, 1, dh), lambda n, m: (layer, 0, 0))
    slope = pl.BlockSpec((nh, 1, MOBA_BLOCK), lambda n, m: (0, 0, 0))
    scratch = [pltpu.VMEM((D_MODEL, TN), BF16)]
    params = _params("arbitrary", "arbitrary")

    def w_spec(col):
        return pl.BlockSpec((None, D_MODEL, TN), lambda n, m: (layer, 0, col))

    kaug, kmeans = pl.pallas_call(
        _k_kernel,
        out_shape=(jax.ShapeDtypeStruct((nh, s, 2 * dh), BF16),
                   jax.ShapeDtypeStruct((s // tm, tm // MOBA_BLOCK, WIDTH_A), F32)),
        grid=grid,
        in_specs=[h_spec, w_spec(1), gain, slope],
        out_specs=(pl.BlockSpec((nh, tm, 2 * dh), lambda n, m: (0, m, 0)),
                   pl.BlockSpec((None, tm // MOBA_BLOCK, WIDTH_A), lambda n, m: (m, 0, 0))),
        scratch_shapes=scratch, compiler_params=params, name="k_proj",
    )(h, w_in, gk_a, slopes)
    kmean = kmeans.reshape(N_BLOCKS, WIDTH_A)
    qaug = pl.pallas_call(
        _q_kernel,
        out_shape=jax.ShapeDtypeStruct((nh, 2 * dh, s), BF16),
        grid=grid,
        in_specs=[h_spec, w_spec(0), gain,
                  pl.BlockSpec((N_BLOCKS, WIDTH_A), lambda n, m: (0, 0)), slope],
        out_specs=pl.BlockSpec((nh, 2 * dh, tm), lambda n, m: (0, 0, m)),
        scratch_shapes=scratch, compiler_params=params, name="q_proj",
    )(h, w_in, gq_a, kmean, slopes)
    vt = pl.pallas_call(
        _v_kernel,
        out_shape=jax.ShapeDtypeStruct((nh, N_BLOCKS, dh + SUM_ROWS, MOBA_BLOCK), BF16),
        grid=grid,
        in_specs=[h_spec, w_spec(2)],
        out_specs=pl.BlockSpec((nh, tm // MOBA_BLOCK, dh + SUM_ROWS, MOBA_BLOCK),
                               lambda n, m: (0, m, 0, 0)),
        scratch_shapes=scratch, compiler_params=params, name="v_proj",
    )(h, w_in)
    return qaug, kaug, vt


def _uv_kernel(h_ref, w_ref, g_ref, o_ref, wbf_ref):
    n = pl.program_id(0)
    _cast_weight_once(w_ref, wbf_ref)

    @pl.when(n == 0)
    def _():
        def emit(rows, acc):
            o_ref[rows, :] = jax.nn.gelu(acc)
        _sub_dots(h_ref, wbf_ref[...], emit)

    @pl.when(n == 1)
    def _():
        def emit(rows, acc):
            a = jax.nn.gelu(acc)
            ms = jnp.mean(a * a, axis=-1, keepdims=True)
            o_ref[rows, :] = a * lax.rsqrt(ms + EPS) * g_ref[...]
        _sub_dots(h_ref, wbf_ref[...], emit)


def _uv_proj(h, w_in, g_sgu, layer):
    s = h.shape[0]
    off = COL_UV // TN
    return pl.pallas_call(
        _uv_kernel,
        out_shape=jax.ShapeDtypeStruct((2, s, SGU_WIDTH), F32),
        grid=(2, s // TM),
        in_specs=[pl.BlockSpec((TM, D_MODEL), lambda n, m: (m, 0)),
                  pl.BlockSpec((None, D_MODEL, TN), lambda n, m: (layer, 0, off + n)),
                  pl.BlockSpec((None, 1, SGU_WIDTH), lambda n, m: (layer, 0, 0))],
        out_specs=pl.BlockSpec((None, TM, SGU_WIDTH), lambda n, m: (n, m, 0)),
        scratch_shapes=[pltpu.VMEM((D_MODEL, TN), BF16)],
        compiler_params=_params("arbitrary", "arbitrary"),
        name="uv_proj",
    )(h, w_in, g_sgu)


def _memkv_kernel(mem_ref, gmem_ref, w_ref, gk_ref, o_ref):
    n = pl.program_id(0)
    x = mem_ref[...]
    ms = jnp.mean(x * x, axis=-1, keepdims=True)
    hm = (x * lax.rsqrt(ms + EPS) * gmem_ref[...]).astype(BF16)
    acc = jnp.dot(hm, w_ref[...].astype(BF16), preferred_element_type=F32)

    @pl.when(n == 0)
    def _():
        o_ref[...] = _group_rmsnorm(acc, gk_ref[...], HEAD_DIM_M).astype(o_ref.dtype)

    @pl.when(n == 1)
    def _():
        o_ref[...] = acc.astype(o_ref.dtype)


def _memkv_proj(mem, g_mem, w_mem_kv, gk_m, layer):
    return pl.pallas_call(
        _memkv_kernel,
        out_shape=jax.ShapeDtypeStruct((2, N_MEM, WIDTH_M), BF16),
        grid=(2,),
        in_specs=[pl.BlockSpec((N_MEM, D_MODEL), lambda n: (0, 0)),
                  pl.BlockSpec((None, 1, D_MODEL), lambda n: (layer, 0, 0)),
                  pl.BlockSpec((None, D_MODEL, WIDTH_M), lambda n: (layer, 0, n)),
                  pl.BlockSpec((None, 1, HEAD_DIM_M), lambda n: (layer, 0, 0))],
        out_specs=pl.BlockSpec((None, N_MEM, WIDTH_M), lambda n: (n, 0, 0)),
        compiler_params=_params("arbitrary"),
        name="memkv_proj",
    )(mem, g_mem, w_mem_kv, gk_m)


def _qm_kernel(h_ref, w_ref, gq_ref, km_ref, vm_ref, o_ref, wbf_ref):
    _cast_weight_once(w_ref, wbf_ref)

    def emit(rows, acc):
        for hh in range(N_HEADS_M):
            cols = slice(hh * HEAD_DIM_M, (hh + 1) * HEAD_DIM_M)
            qh = acc[:, cols]
            ms = jnp.mean(qh * qh, axis=-1, keepdims=True)
            qn = (qh * lax.rsqrt(ms + EPS) * gq_ref[...]).astype(BF16)
            sc = lax.dot_general(qn, km_ref[:, cols], (((1,), (1,)), ((), ())),
                                 preferred_element_type=F32) * HEAD_DIM_M ** -0.5
            mx = jnp.max(sc, axis=-1, keepdims=True)
            e = jnp.exp(sc - mx)
            p = (e / jnp.sum(e, axis=-1, keepdims=True)).astype(BF16)
            o_ref[rows, cols] = jnp.dot(
                p, vm_ref[:, cols], preferred_element_type=F32).astype(o_ref.dtype)

    _sub_dots(h_ref, wbf_ref[...], emit)


def _qm_memattn(h, w_in, gq_m, kvm, layer):
    s = h.shape[0]
    off = COL_QM // TN
    tm = TM_BIG
    return pl.pallas_call(
        _qm_kernel,
        out_shape=jax.ShapeDtypeStruct((s, WIDTH_M), BF16),
        grid=(1, s // tm),
        in_specs=[pl.BlockSpec((tm, D_MODEL), lambda n, m: (m, 0)),
                  pl.BlockSpec((None, D_MODEL, TN), lambda n, m: (layer, 0, off)),
                  pl.BlockSpec((None, 1, HEAD_DIM_M), lambda n, m: (layer, 0, 0)),
                  pl.BlockSpec((None, N_MEM, WIDTH_M), lambda n, m: (0, 0, 0)),
                  pl.BlockSpec((None, N_MEM, WIDTH_M), lambda n, m: (1, 0, 0))],
        out_specs=pl.BlockSpec((tm, WIDTH_M), lambda n, m: (m, 0)),
        scratch_shapes=[pltpu.VMEM((D_MODEL, TN), BF16)],
        compiler_params=_params("arbitrary", "arbitrary"),
        name="qm_memattn",
    )(h, w_in, gq_m, kvm, kvm)


def _gates_kernel(h_ref, w_ref, o_ref, wbf_ref):
    _cast_weight_once(w_ref, wbf_ref)
    def emit(rows, acc):
        o_ref[rows, :] = jax.nn.sigmoid(acc)

    _sub_dots(h_ref, wbf_ref[...], emit)


def _gates_proj(h, w_in, layer):
    s = h.shape[0]
    off = COL_GATES // TN
    n_tiles = 3 * D_MODEL // TN
    return pl.pallas_call(
        _gates_kernel,
        out_shape=jax.ShapeDtypeStruct((s, 3 * D_MODEL), F32),
        grid=(n_tiles, s // TM),
        in_specs=[pl.BlockSpec((TM, D_MODEL), lambda n, m: (m, 0)),
                  pl.BlockSpec((None, D_MODEL, TN), lambda n, m: (layer, 0, off + n))],
        out_specs=pl.BlockSpec((TM, TN), lambda n, m: (m, n)),
        scratch_shapes=[pltpu.VMEM((D_MODEL, TN), BF16)],
        compiler_params=_params("arbitrary", "arbitrary"),
        name="gates_proj",
    )(h, w_in)


MOBA_HALF = 2
MOBA_HEADS = 2
M_INIT = -3e38


def _moba_kernel(qaug_ref, kaug_ref, vt_ref, o_ref, s_ref):
    i = pl.program_id(1)
    blk, dh = MOBA_BLOCK, HEAD_DIM_A
    heads = range(MOBA_HEADS)
    half = MOBA_HALF
    half_rows = half * blk
    key_minus_qry = (lax.broadcasted_iota(jnp.int32, (half_rows, blk), 0)
                     - lax.broadcasted_iota(jnp.int32, (half_rows, blk), 1))

    def score(hh, first_blk):
        rows = pl.ds(pl.multiple_of(first_blk * blk, half_rows), half_rows)
        return jnp.dot(kaug_ref[hh, rows, :], qaug_ref[hh], preferred_element_type=F32)

    def attend(hh, st, s, first_blk, causal):
        m, acc = st
        if causal:
            s = jnp.where(key_minus_qry <= (i - first_blk) * blk, s, NEG_INF)
        m_new = jnp.maximum(m, jnp.max(s, axis=0, keepdims=True))
        alpha = jnp.exp(m - m_new)
        p = jnp.exp(s - m_new)
        v_t = jnp.concatenate([vt_ref[hh, first_blk + g] for g in range(half)], axis=1)
        acc = alpha * acc + jnp.dot(v_t, p.astype(BF16), preferred_element_type=F32)
        return m_new, acc

    def trip(t, carry, last):
        first = t * (2 * half)
        s_second = [score(hh, first + half) for hh in heads]
        carry = [attend(hh, carry[hh], s_ref[hh], first, last) for hh in heads]
        if not last:
            for hh in heads:
                s_ref[hh] = score(hh, first + 2 * half)
        carry = [attend(hh, carry[hh], s_second[hh], first + half, last) for hh in heads]
        return tuple(carry)

    for hh in heads:
        s_ref[hh] = score(hh, 0)

    state = tuple((jnp.full((1, blk), M_INIT, F32), jnp.zeros((dh + SUM_ROWS, blk), F32))
                  for _ in heads)
    last_trip = i // (2 * half)
    state = lax.fori_loop(0, last_trip, lambda t, c: trip(t, c, False), state)
    state = trip(last_trip, state, True)
    for hh in heads:
        _, acc = state[hh]
        out = acc[0:dh] / acc[dh:dh + 1]
        o_ref[:, hh * dh:(hh + 1) * dh] = out.T.astype(o_ref.dtype)


def _moba(qaug, kaug, vt):
    s = kaug.shape[1]
    nh = MOBA_HEADS
    return pl.pallas_call(
        _moba_kernel,
        out_shape=jax.ShapeDtypeStruct((s, WIDTH_A), BF16),
        grid=(N_HEADS_A // nh, s // MOBA_BLOCK),
        in_specs=[pl.BlockSpec((nh, 2 * HEAD_DIM_A, MOBA_BLOCK), lambda h, i: (h, 0, i)),
                  pl.BlockSpec((nh, s, 2 * HEAD_DIM_A), lambda h, i: (h, 0, 0)),
                  pl.BlockSpec((nh, N_BLOCKS, HEAD_DIM_A + SUM_ROWS, MOBA_BLOCK),
                               lambda h, i: (h, 0, 0, 0))],
        out_specs=pl.BlockSpec((MOBA_BLOCK, nh * HEAD_DIM_A), lambda h, i: (i, h)),
        scratch_shapes=[pltpu.VMEM((nh, MOBA_HALF * MOBA_BLOCK, MOBA_BLOCK), F32)],
        compiler_params=_params("arbitrary", "arbitrary"),
        name="moba_attention",
    )(qaug, kaug, vt)


def _sgu_kernel(u_ref, v_ref, w_ref, bt_ref, o_ref):
    tm = u_ref.shape[0]
    row = lax.broadcasted_iota(jnp.int32, (SGU_CHUNK, SGU_CHUNK), 0)
    col = lax.broadcasted_iota(jnp.int32, (SGU_CHUNK, SGU_CHUNK), 1)
    for g in range(SGU_GROUPS):
        cols = slice(g * SGU_CHUNK, (g + 1) * SGU_CHUNK)
        w = jnp.where(col <= row, w_ref[g], 0.0).astype(BF16)
        bias = bt_ref[:, g:g + 1]
        for c in range(tm // SGU_CHUNK):
            rows = slice(c * SGU_CHUNK, (c + 1) * SGU_CHUNK)
            mixed = jnp.dot(w, v_ref[rows, cols].astype(BF16),
                            preferred_element_type=F32) + bias
            o_ref[rows, cols] = (u_ref[rows, cols] * mixed).astype(o_ref.dtype)


def _sgu(uv, w_sgu, b_sgu_t, layer):
    s = uv.shape[1]
    tm = 512
    return pl.pallas_call(
        _sgu_kernel,
        out_shape=jax.ShapeDtypeStruct((s, SGU_WIDTH), BF16),
        grid=(s // tm,),
        in_specs=[pl.BlockSpec((None, tm, SGU_WIDTH), lambda m: (0, m, 0)),
                  pl.BlockSpec((None, tm, SGU_WIDTH), lambda m: (1, m, 0)),
                  pl.BlockSpec((None, SGU_GROUPS, SGU_CHUNK, SGU_CHUNK),
                               lambda m: (layer, 0, 0, 0)),
                  pl.BlockSpec((None, SGU_CHUNK, SGU_GROUPS), lambda m: (layer, 0, 0))],
        out_specs=pl.BlockSpec((tm, SGU_WIDTH), lambda m: (m, 0)),
        compiler_params=_params("arbitrary"),
        name="sgu",
    )(uv, uv, w_sgu, b_sgu_t)


MERGE_TN = 1024
MERGE_TM = 512


def _merge_kernel(ya_ref, yb_ref, ym_ref, ga_ref, gb_ref, gm_ref, w_ref, o_ref, wbf_ref):
    _cast_weight_once(w_ref, wbf_ref)
    for r in range(0, ya_ref.shape[0], SUB_ROWS):
        rows = slice(r, r + SUB_ROWS)
        merged = ga_ref[rows, :] * jnp.dot(ya_ref[rows, :], wbf_ref[0],
                                           preferred_element_type=F32)
        merged += gb_ref[rows, :] * jnp.dot(yb_ref[rows, :], wbf_ref[1],
                                            preferred_element_type=F32)
        merged += gm_ref[rows, :] * jnp.dot(ym_ref[rows, :], wbf_ref[2],
                                            preferred_element_type=F32)
        o_ref[rows, :] = merged.astype(o_ref.dtype)


def _merge(ya, yb, ym, gates, w_branch, layer):
    s = ya.shape[0]
    tn = MERGE_TN
    nt = D_MODEL // tn
    tm = MERGE_TM
    act = pl.BlockSpec((tm, WIDTH_A), lambda n, m: (m, 0))
    return pl.pallas_call(
        _merge_kernel,
        out_shape=jax.ShapeDtypeStruct((s, D_MODEL), BF16),
        grid=(nt, s // tm),
        in_specs=[act, act, act,
                  pl.BlockSpec((tm, tn), lambda n, m: (m, n)),
                  pl.BlockSpec((tm, tn), lambda n, m: (m, nt + n)),
                  pl.BlockSpec((tm, tn), lambda n, m: (m, 2 * nt + n)),
                  pl.BlockSpec((None, 3, WIDTH_A, tn), lambda n, m: (layer, 0, 0, n))],
        out_specs=pl.BlockSpec((tm, tn), lambda n, m: (m, n)),
        scratch_shapes=[pltpu.VMEM((3, WIDTH_A, tn), BF16)],
        compiler_params=_params("arbitrary", "arbitrary"),
        name="branch_merge",
    )(ya, yb, ym, gates, gates, gates, w_branch)


def _resid_proj_kernel(a_ref, w_ref, x_ref, o_ref, wbf_ref):
    _cast_weight_once(w_ref, wbf_ref)
    def emit(rows, acc):
        o_ref[rows, :] = x_ref[rows, :] + acc

    _sub_dots(a_ref, wbf_ref[...], emit)


def _resid_proj(a, w, x, layer, tm, tn, name):
    s, k = a.shape
    n_out = w.shape[2]
    return pl.pallas_call(
        _resid_proj_kernel,
        out_shape=jax.ShapeDtypeStruct((s, n_out), F32),
        grid=(n_out // tn, s // tm),
        in_specs=[pl.BlockSpec((tm, k), lambda n, m: (m, 0)),
                  pl.BlockSpec((None, k, tn), lambda n, m: (layer, 0, n)),
                  pl.BlockSpec((tm, tn), lambda n, m: (m, n))],
        out_specs=pl.BlockSpec((tm, tn), lambda n, m: (m, n)),
        scratch_shapes=[pltpu.VMEM((k, tn), BF16)],
        compiler_params=_params("arbitrary", "arbitrary"),
        name=name,
    )(a, w, x)


def _out_proj_norm_kernel(a_ref, w_ref, x_ref, g_ref, o_ref, h_ref, wbf_ref):
    @pl.when(pl.program_id(0) == 0)
    def _():
        wbf_ref[...] = w_ref[...].astype(BF16)

    def emit(rows, acc):
        x = x_ref[rows, :] + acc
        o_ref[rows, :] = x
        ms = jnp.mean(x * x, axis=-1, keepdims=True)
        h_ref[rows, :] = (x * lax.rsqrt(ms + EPS) * g_ref[...]).astype(h_ref.dtype)

    _sub_dots(a_ref, wbf_ref[...], emit)


def _out_proj_norm(a, w, x, g_all, layer):
    s, k = a.shape
    tm = 512
    return pl.pallas_call(
        _out_proj_norm_kernel,
        out_shape=(jax.ShapeDtypeStruct((s, D_MODEL), F32),
                   jax.ShapeDtypeStruct((s, D_MODEL), BF16)),
        grid=(s // tm,),
        in_specs=[pl.BlockSpec((tm, k), lambda m: (m, 0)),
                  pl.BlockSpec((None, k, D_MODEL), lambda m: (layer, 0, 0),
                               pipeline_mode=pl.Buffered(1)),
                  pl.BlockSpec((tm, D_MODEL), lambda m: (m, 0)),
                  pl.BlockSpec((None, 1, D_MODEL), lambda m: (layer, 0, 0))],
        out_specs=(pl.BlockSpec((tm, D_MODEL), lambda m: (m, 0)),
                   pl.BlockSpec((tm, D_MODEL), lambda m: (m, 0))),
        scratch_shapes=[pltpu.VMEM((k, D_MODEL), BF16)],
        compiler_params=_params("arbitrary"),
        name="out_proj_norm",
    )(a, w, x, g_all)


FFN_TN = 512


def _ffn_up_kernel(h_ref, wg_ref, wu_ref, o_ref, wgbf_ref, wubf_ref):
    @pl.when(pl.program_id(1) == 0)
    def _():
        wgbf_ref[...] = wg_ref[...].astype(BF16)
        wubf_ref[...] = wu_ref[...].astype(BF16)
    for r in range(0, h_ref.shape[0], SUB_ROWS):
        rows = slice(r, r + SUB_ROWS)
        gt = jnp.dot(h_ref[rows, :], wgbf_ref[...], preferred_element_type=F32)
        up = jnp.dot(h_ref[rows, :], wubf_ref[...], preferred_element_type=F32)
        o_ref[rows, :] = (jax.nn.silu(gt) * up).astype(o_ref.dtype)


def _ffn_up(h, w_gate_up, layer):
    s = h.shape[0]
    tn = FFN_TN
    nt = D_FF // tn
    tm = TM_BIG
    return pl.pallas_call(
        _ffn_up_kernel,
        out_shape=jax.ShapeDtypeStruct((s, D_FF), BF16),
        grid=(nt, s // tm),
        in_specs=[pl.BlockSpec((tm, D_MODEL), lambda n, m: (m, 0)),
                  pl.BlockSpec((None, D_MODEL, tn), lambda n, m: (layer, 0, n)),
                  pl.BlockSpec((None, D_MODEL, tn), lambda n, m: (layer, 0, nt + n))],
        out_specs=pl.BlockSpec((tm, tn), lambda n, m: (m, n)),
        scratch_shapes=[pltpu.VMEM((D_MODEL, tn), BF16), pltpu.VMEM((D_MODEL, tn), BF16)],
        compiler_params=_params("arbitrary", "arbitrary"),
        name="ffn_up",
    )(h, w_gate_up, w_gate_up)


def kernel(x, mem, g_mix, w_in, gq_a, gk_a, g_sgu, w_sgu, b_sgu, gq_m, gk_m, g_mem,
           w_mem_kv, w_branch, w_out, g_ffn, w_gate_up, w_down):
    b, s, d = x.shape
    assert (b, s, d) == (1, SEQ, D_MODEL) and mem.shape == (1, N_MEM, D_MODEL)
    x2 = x.reshape(s, d)
    mem2 = mem.reshape(N_MEM, d)

    def row(p):
        return p.reshape(DEPTH, 1, p.shape[-1])

    g_mix3, g_ffn3, g_mem3, g_sgu3 = row(g_mix), row(g_ffn), row(g_mem), row(g_sgu)
    gq_a3, gk_a3, gq_m3, gk_m3 = row(gq_a), row(gk_a), row(gq_m), row(gk_m)
    b_sgu_t = jnp.swapaxes(b_sgu, 1, 2)
    slopes = 2.0 ** (-8.0 * jnp.arange(1, N_HEADS_A + 1, dtype=F32) / N_HEADS_A)
    slopes = jnp.broadcast_to(slopes[:, None, None], (N_HEADS_A, 1, MOBA_BLOCK))

    for layer in range(DEPTH):
        h = _rmsnorm_bf16(x2, g_mix3, layer)
        qaug, kaug, vt = _qkv_proj(h, w_in, gq_a3, gk_a3, slopes, layer)
        uv = _uv_proj(h, w_in, g_sgu3, layer)
        kvm = _memkv_proj(mem2, g_mem3, w_mem_kv, gk_m3, layer)
        ym = _qm_memattn(h, w_in, gq_m3, kvm, layer)
        gates = _gates_proj(h, w_in, layer)
        ya = _moba(qaug, kaug, vt)
        yb = _sgu(uv, w_sgu, b_sgu_t, layer)
        merged = _merge(ya, yb, ym, gates, w_branch, layer)
        x2, hf = _out_proj_norm(merged, w_out, x2, g_ffn3, layer)
        act = _ffn_up(hf, w_gate_up, layer)
        x2 = _resid_proj(act, w_down, x2, layer, 512, 512, "ffn_down")
    return x2.reshape(b, s, d)
```

```python
import functools

import jax
import jax.numpy as jnp
from jax import lax
from jax.experimental import pallas as pl
from jax.experimental.pallas import tpu as pltpu

F32 = jnp.float32
BF16 = jnp.bfloat16

D_MODEL = 2048
SEQ = 8192
DEPTH = 2
N_HEADS_A = 8
HEAD_DIM_A = 128
WIDTH_A = N_HEADS_A * HEAD_DIM_A
MOBA_BLOCK = 256
MOBA_TOPK = 3
N_BLOCKS = SEQ // MOBA_BLOCK
SGU_WIDTH = 1024
SGU_GROUPS = 8
SGU_CHUNK = 128
N_MEM = 256
N_HEADS_M = 4
HEAD_DIM_M = 256
WIDTH_M = N_HEADS_M * HEAD_DIM_M
D_FF = 5632
NEG_INF = -1e30
EPS = 1e-6

COL_QKV = 0
COL_UV = 3 * WIDTH_A
COL_QM = COL_UV + 2 * SGU_WIDTH
COL_GATES = COL_QM + WIDTH_M

VMEM_LIMIT_BYTES = 60 * 1024 * 1024

TM = 1024
TM_BIG = 2048
TN = 1024
SUB_ROWS = 256


def _params(*semantics):
    return pltpu.CompilerParams(dimension_semantics=semantics,
                                vmem_limit_bytes=VMEM_LIMIT_BYTES)


def _cast_weight_once(w_ref, wbf_ref):
    @pl.when(pl.program_id(1) == 0)
    def _():
        wbf_ref[...] = w_ref[...].astype(BF16)


def _group_rmsnorm(a, g, width):
    outs = []
    for s in range(0, a.shape[1], width):
        blk = a[:, s:s + width]
        ms = jnp.mean(blk * blk, axis=-1, keepdims=True)
        outs.append(blk * lax.rsqrt(ms + EPS) * g)
    return jnp.concatenate(outs, axis=1)


def _norm_kernel(x_ref, g_ref, o_ref):
    x = x_ref[...]
    ms = jnp.mean(x * x, axis=-1, keepdims=True)
    o_ref[...] = (x * lax.rsqrt(ms + EPS) * g_ref[...]).astype(o_ref.dtype)


def _rmsnorm_bf16(x, g_all, layer):
    tm = 512
    return pl.pallas_call(
        _norm_kernel,
        out_shape=jax.ShapeDtypeStruct(x.shape, BF16),
        grid=(x.shape[0] // tm,),
        in_specs=[pl.BlockSpec((tm, D_MODEL), lambda m: (m, 0)),
                  pl.BlockSpec((None, 1, D_MODEL), lambda m: (layer, 0, 0))],
        out_specs=pl.BlockSpec((tm, D_MODEL), lambda m: (m, 0)),
        compiler_params=_params("arbitrary"),
        name="rmsnorm",
    )(x, g_all)


def _sub_dots(lhs_ref, rhs, emit):
    def sub_dot(r):
        rows = slice(r, r + SUB_ROWS)
        return rows, jnp.dot(lhs_ref[rows, :], rhs, preferred_element_type=F32)

    n_rows = lhs_ref.shape[0]
    cur = sub_dot(0)
    for r in range(0, n_rows, SUB_ROWS):
        nxt = sub_dot(r + SUB_ROWS) if r + SUB_ROWS < n_rows else None
        emit(*cur)
        cur = nxt


COL_KEY_OFF = N_BLOCKS
COL_BLK_OFF = N_BLOCKS + 1
COL_ONES = N_BLOCKS + 2
SUM_ROWS = 16


def _head_norm(a, g):
    ms = jnp.mean(a * a, axis=-1, keepdims=True)
    return a * lax.rsqrt(ms + EPS) * g


def _k_kernel(h_ref, w_ref, gk_ref, slope_ref, kaug_ref, kmean_ref, wbf_ref):
    m = pl.program_id(1)
    _cast_weight_once(w_ref, wbf_ref)
    assert SUB_ROWS == MOBA_BLOCK
    blk, dh = MOBA_BLOCK, HEAD_DIM_A
    lane = lax.broadcasted_iota(jnp.int32, (blk, dh), 1)
    lane_row = lax.broadcasted_iota(jnp.int32, (1, dh), 1)
    key_off = lax.broadcasted_iota(jnp.int32, (blk, dh), 0).astype(F32)

    def emit(rows, acc):
        b_local = rows.start // blk
        b = m * (h_ref.shape[0] // blk) + b_local
        blk_start = jnp.full((1, dh), b * blk, jnp.int32).astype(F32)
        onehot = jnp.where(lane == b, 1.0, jnp.where(lane == COL_ONES, 1.0, 0.0))
        for hh in range(N_HEADS_A):
            cols = slice(hh * dh, (hh + 1) * dh)
            kn = _head_norm(acc[:, cols], gk_ref[...])
            kmean_ref[b_local:b_local + 1, cols] = jnp.mean(kn, axis=0, keepdims=True)
            sl = slope_ref[hh][:, :dh]
            extra = (onehot + jnp.where(lane == COL_KEY_OFF, sl * key_off, 0.0)
                     + jnp.where(lane_row == COL_BLK_OFF, sl * blk_start, 0.0))
            kaug_ref[hh, rows, 0:dh] = kn.astype(BF16)
            kaug_ref[hh, rows, dh:2 * dh] = extra.astype(BF16)

    _sub_dots(h_ref, wbf_ref[...], emit)


def _q_kernel(h_ref, w_ref, gq_ref, kmean_ref, slope_ref, qaug_ref, wbf_ref):
    m = pl.program_id(1)
    _cast_weight_once(w_ref, wbf_ref)
    assert SUB_ROWS == MOBA_BLOCK
    blk, dh = MOBA_BLOCK, HEAD_DIM_A
    blk_id = lax.broadcasted_iota(jnp.int32, (N_BLOCKS, blk), 0)
    blk_id_f = blk_id.astype(F32)
    rest = lax.broadcasted_iota(jnp.int32, (dh - N_BLOCKS, blk), 0) + N_BLOCKS

    def emit(rows, acc):
        i = m * (h_ref.shape[0] // blk) + rows.start // blk
        past = blk_id < i
        q_start = jnp.full((1, blk), i * blk, jnp.int32).astype(F32)
        for hh in range(N_HEADS_A):
            cols = slice(hh * dh, (hh + 1) * dh)
            q = _head_norm(acc[:, cols], gq_ref[...]) * dh ** -0.5
            q_t = q.T
            gate = jnp.dot(kmean_ref[:, cols].astype(BF16), q_t.astype(BF16),
                           preferred_element_type=F32)
            gate = jnp.where(past, gate, NEG_INF)
            pen = jnp.full((N_BLOCKS, blk), NEG_INF, F32)
            for _ in range(MOBA_TOPK):
                mx = jnp.max(gate, axis=0, keepdims=True)
                first = jnp.min(jnp.where(gate == mx, blk_id_f, float(N_BLOCKS)),
                                axis=0, keepdims=True)
                hit = blk_id_f == first
                pen = jnp.where(jnp.logical_and(hit, mx > 0.5 * NEG_INF), 0.0, pen)
                gate = jnp.where(hit, -3e38, gate)
            pen = jnp.where(blk_id == i, 0.0, pen)
            extra_q = jnp.where(rest == COL_ONES, -slope_ref[hh] * q_start,
                                jnp.where(rest < COL_ONES, 1.0, 0.0))
            qaug_ref[hh, :, rows] = jnp.concatenate([q_t, pen, extra_q], axis=0).astype(BF16)

    _sub_dots(h_ref, wbf_ref[...], emit)


def _v_kernel(h_ref, w_ref, vt_ref, wbf_ref):
    _cast_weight_once(w_ref, wbf_ref)
    assert SUB_ROWS == MOBA_BLOCK
    blk, dh = MOBA_BLOCK, HEAD_DIM_A

    def emit(rows, acc):
        b_local = rows.start // blk
        for hh in range(N_HEADS_A):
            vt_ref[hh, b_local, 0:dh, :] = acc[:, hh * dh:(hh + 1) * dh].T.astype(BF16)
            vt_ref[hh, b_local, dh:dh + SUM_ROWS, :] = jnp.ones((SUM_ROWS, blk), BF16)

    _sub_dots(h_ref, wbf_ref[...], emit)


def _qkv_proj(h, w_in, gq_a, gk_a, slopes, layer):
    s = h.shape[0]
    tm, dh, nh = TM, HEAD_DIM_A, N_HEADS_A
    grid = (1, s // tm)
    h_spec = pl.BlockSpec((tm, D_MODEL), lambda n, m: (m, 0))
    gain = pl.BlockSpec((None, 1, dh), lambda n, m: (layer, 0, 0))
    slope = pl.BlockSpec((nh, 1, MOBA_BLOCK), lambda n, m: (0, 0, 0))
    scratch = [pltpu.VMEM((D_MODEL, TN), BF16)]
    params = _params("arbitrary", "arbitrary")

    def w_spec(col):
        return pl.BlockSpec((None, D_MODEL, TN), lambda n, m: (layer, 0, col))

    kaug, kmeans = pl.pallas_call(
        _k_kernel,
        out_shape=(jax.ShapeDtypeStruct((nh, s, 2 * dh), BF16),
                   jax.ShapeDtypeStruct((s // tm, tm // MOBA_BLOCK, WIDTH_A), F32)),
        grid=grid,
        in_specs=[h_spec, w_spec(1), gain, slope],
        out_specs=(pl.BlockSpec((nh, tm, 2 * dh), lambda n, m: (0, m, 0)),
                   pl.BlockSpec((None, tm // MOBA_BLOCK, WIDTH_A), lambda n, m: (m, 0, 0))),
        scratch_shapes=scratch, compiler_params=params, name="k_proj",
    )(h, w_in, gk_a, slopes)
    kmean = kmeans.reshape(N_BLOCKS, WIDTH_A)
    qaug = pl.pallas_call(
        _q_kernel,
        out_shape=jax.ShapeDtypeStruct((nh, 2 * dh, s), BF16),
        grid=grid,
        in_specs=[h_spec, w_spec(0), gain,
                  pl.BlockSpec((N_BLOCKS, WIDTH_A), lambda n, m: (0, 0)), slope],
        out_specs=pl.BlockSpec((nh, 2 * dh, tm), lambda n, m: (0, 0, m)),
        scratch_shapes=scratch, compiler_params=params, name="q_proj",
    )(h, w_in, gq_a, kmean, slopes)
    vt = pl.pallas_call(
        _v_kernel,
        out_shape=jax.ShapeDtypeStruct((nh, N_BLOCKS, dh + SUM_ROWS, MOBA_BLOCK), BF16),
        grid=grid,
        in_specs=[h_spec, w_spec(2)],
        out_specs=pl.BlockSpec((nh, tm // MOBA_BLOCK, dh + SUM_ROWS, MOBA_BLOCK),
                               lambda n, m: (0, m, 0, 0)),
        scratch_shapes=scratch, compiler_params=params, name="v_proj",
    )(h, w_in)
    return qaug, kaug, vt


def _sgu_kernel(h_ref, wu_ref, wv_ref, g_ref, ws_ref, bt_ref, o_ref,
                wubf_ref, wvbf_ref, wsbf_ref):
    @pl.when(pl.program_id(0) == 0)
    def _():
        wubf_ref[...] = wu_ref[...].astype(BF16)
        wvbf_ref[...] = wv_ref[...].astype(BF16)
        row = lax.broadcasted_iota(jnp.int32, (SGU_CHUNK, SGU_CHUNK), 0)
        col = lax.broadcasted_iota(jnp.int32, (SGU_CHUNK, SGU_CHUNK), 1)
        for g in range(SGU_GROUPS):
            wsbf_ref[g] = jnp.where(col <= row, ws_ref[g], 0.0).astype(BF16)

    def proj(r):
        rows = slice(r, r + SUB_ROWS)
        return (rows,
                jnp.dot(h_ref[rows, :], wubf_ref[...], preferred_element_type=F32),
                jnp.dot(h_ref[rows, :], wvbf_ref[...], preferred_element_type=F32))

    def mix(rows, u_acc, v_acc):
        u = jax.nn.gelu(u_acc)
        v = jax.nn.gelu(v_acc)
        ms = jnp.mean(v * v, axis=-1, keepdims=True)
        vn = (v * lax.rsqrt(ms + EPS) * g_ref[...]).astype(BF16)
        chunks = [slice(c, c + SGU_CHUNK) for c in range(0, SUB_ROWS, SGU_CHUNK)]
        for g in range(SGU_GROUPS):
            cols = slice(g * SGU_CHUNK, (g + 1) * SGU_CHUNK)
            v_all = jnp.concatenate([vn[c, cols] for c in chunks], axis=1)
            mixed = jnp.dot(wsbf_ref[g], v_all, preferred_element_type=F32) + bt_ref[:, g:g + 1]
            for c in chunks:
                o_ref[rows.start + c.start:rows.start + c.stop, cols] = (
                    u[c, cols] * mixed[:, c]).astype(o_ref.dtype)

    n_rows = h_ref.shape[0]
    cur = proj(0)
    for r in range(0, n_rows, SUB_ROWS):
        nxt = proj(r + SUB_ROWS) if r + SUB_ROWS < n_rows else None
        mix(*cur)
        cur = nxt


def _sgu_branch(h, w_in, g_sgu, w_sgu, b_sgu_t, layer):
    s = h.shape[0]
    off = COL_UV // TN
    once = pl.Buffered(1)
    return pl.pallas_call(
        _sgu_kernel,
        out_shape=jax.ShapeDtypeStruct((s, SGU_WIDTH), BF16),
        grid=(s // TM,),
        in_specs=[pl.BlockSpec((TM, D_MODEL), lambda m: (m, 0)),
                  pl.BlockSpec((None, D_MODEL, TN), lambda m: (layer, 0, off),
                               pipeline_mode=once),
                  pl.BlockSpec((None, D_MODEL, TN), lambda m: (layer, 0, off + 1),
                               pipeline_mode=once),
                  pl.BlockSpec((None, 1, SGU_WIDTH), lambda m: (layer, 0, 0)),
                  pl.BlockSpec((None, SGU_GROUPS, SGU_CHUNK, SGU_CHUNK),
                               lambda m: (layer, 0, 0, 0)),
                  pl.BlockSpec((None, SGU_CHUNK, SGU_GROUPS), lambda m: (layer, 0, 0))],
        out_specs=pl.BlockSpec((TM, SGU_WIDTH), lambda m: (m, 0)),
        scratch_shapes=[pltpu.VMEM((D_MODEL, TN), BF16), pltpu.VMEM((D_MODEL, TN), BF16),
                        pltpu.VMEM((SGU_GROUPS, SGU_CHUNK, SGU_CHUNK), BF16)],
        compiler_params=_params("arbitrary"),
        name="sgu_branch",
    )(h, w_in, w_in, g_sgu, w_sgu, b_sgu_t)


def _memkv_kernel(mem_ref, gmem_ref, w_ref, gk_ref, o_ref):
    n = pl.program_id(0)
    x = mem_ref[...]
    ms = jnp.mean(x * x, axis=-1, keepdims=True)
    hm = (x * lax.rsqrt(ms + EPS) * gmem_ref[...]).astype(BF16)
    acc = jnp.dot(hm, w_ref[...].astype(BF16), preferred_element_type=F32)

    @pl.when(n == 0)
    def _():
        o_ref[...] = _group_rmsnorm(acc, gk_ref[...], HEAD_DIM_M).astype(o_ref.dtype)

    @pl.when(n == 1)
    def _():
        o_ref[...] = acc.astype(o_ref.dtype)


def _memkv_proj(mem, g_mem, w_mem_kv, gk_m, layer):
    return pl.pallas_call(
        _memkv_kernel,
        out_shape=jax.ShapeDtypeStruct((2, N_MEM, WIDTH_M), BF16),
        grid=(2,),
        in_specs=[pl.BlockSpec((N_MEM, D_MODEL), lambda n: (0, 0)),
                  pl.BlockSpec((None, 1, D_MODEL), lambda n: (layer, 0, 0)),
                  pl.BlockSpec((None, D_MODEL, WIDTH_M), lambda n: (layer, 0, n)),
                  pl.BlockSpec((None, 1, HEAD_DIM_M), lambda n: (layer, 0, 0))],
        out_specs=pl.BlockSpec((None, N_MEM, WIDTH_M), lambda n: (n, 0, 0)),
        compiler_params=_params("arbitrary"),
        name="memkv_proj",
    )(mem, g_mem, w_mem_kv, gk_m)


def _qm_kernel(h_ref, w_ref, gq_ref, km_ref, vm_ref, o_ref, wbf_ref):
    _cast_weight_once(w_ref, wbf_ref)

    def emit(rows, acc):
        heads = [slice(hh * HEAD_DIM_M, (hh + 1) * HEAD_DIM_M) for hh in range(N_HEADS_M)]
        scores = []
        for cols in heads:
            qh = acc[:, cols]
            ms = jnp.mean(qh * qh, axis=-1, keepdims=True)
            qn = (qh * lax.rsqrt(ms + EPS) * gq_ref[...]).astype(BF16)
            scores.append(lax.dot_general(qn, km_ref[:, cols], (((1,), (1,)), ((), ())),
                                          preferred_element_type=F32) * HEAD_DIM_M ** -0.5)
        weights = []
        for sc in scores:
            mx = jnp.max(sc, axis=-1, keepdims=True)
            e = jnp.exp(sc - mx)
            weights.append((e / jnp.sum(e, axis=-1, keepdims=True)).astype(BF16))
        for cols, p in zip(heads, weights):
            o_ref[rows, cols] = jnp.dot(
                p, vm_ref[:, cols], preferred_element_type=F32).astype(o_ref.dtype)

    _sub_dots(h_ref, wbf_ref[...], emit)


def _qm_memattn(h, w_in, gq_m, kvm, layer):
    s = h.shape[0]
    off = COL_QM // TN
    tm = TM_BIG
    return pl.pallas_call(
        _qm_kernel,
        out_shape=jax.ShapeDtypeStruct((s, WIDTH_M), BF16),
        grid=(1, s // tm),
        in_specs=[pl.BlockSpec((tm, D_MODEL), lambda n, m: (m, 0)),
                  pl.BlockSpec((None, D_MODEL, TN), lambda n, m: (layer, 0, off)),
                  pl.BlockSpec((None, 1, HEAD_DIM_M), lambda n, m: (layer, 0, 0)),
                  pl.BlockSpec((None, N_MEM, WIDTH_M), lambda n, m: (0, 0, 0)),
                  pl.BlockSpec((None, N_MEM, WIDTH_M), lambda n, m: (1, 0, 0))],
        out_specs=pl.BlockSpec((tm, WIDTH_M), lambda n, m: (m, 0)),
        scratch_shapes=[pltpu.VMEM((D_MODEL, TN), BF16)],
        compiler_params=_params("arbitrary", "arbitrary"),
        name="qm_memattn",
    )(h, w_in, gq_m, kvm, kvm)


def _gates_kernel(h_ref, w_ref, o_ref, wbf_ref):
    _cast_weight_once(w_ref, wbf_ref)
    def emit(rows, acc):
        o_ref[rows, :] = jax.nn.sigmoid(acc)

    _sub_dots(h_ref, wbf_ref[...], emit)


def _gates_proj(h, w_in, layer):
    s = h.shape[0]
    off = COL_GATES // TN
    n_tiles = 3 * D_MODEL // TN
    return pl.pallas_call(
        _gates_kernel,
        out_shape=jax.ShapeDtypeStruct((s, 3 * D_MODEL), F32),
        grid=(n_tiles, s // TM),
        in_specs=[pl.BlockSpec((TM, D_MODEL), lambda n, m: (m, 0)),
                  pl.BlockSpec((None, D_MODEL, TN), lambda n, m: (layer, 0, off + n))],
        out_specs=pl.BlockSpec((TM, TN), lambda n, m: (m, n)),
        scratch_shapes=[pltpu.VMEM((D_MODEL, TN), BF16)],
        compiler_params=_params("arbitrary", "arbitrary"),
        name="gates_proj",
    )(h, w_in)


MOBA_HALF = 2
MOBA_HEADS = 2
M_INIT = -3e38


def _moba_kernel(qaug_ref, kaug_ref, vt_ref, o_ref, s_ref):
    i = pl.program_id(1)
    blk, dh = MOBA_BLOCK, HEAD_DIM_A
    heads = range(MOBA_HEADS)
    half = MOBA_HALF
    half_rows = half * blk
    key_minus_qry = (lax.broadcasted_iota(jnp.int32, (half_rows, blk), 0)
                     - lax.broadcasted_iota(jnp.int32, (half_rows, blk), 1))

    def score(hh, first_blk):
        rows = pl.ds(pl.multiple_of(first_blk * blk, half_rows), half_rows)
        return jnp.dot(kaug_ref[hh, rows, :], qaug_ref[hh], preferred_element_type=F32)

    def attend(hh, st, s, first_blk, causal):
        m, acc = st
        if causal:
            s = jnp.where(key_minus_qry <= (i - first_blk) * blk, s, NEG_INF)
        m_new = jnp.maximum(m, jnp.max(s, axis=0, keepdims=True))
        alpha = jnp.exp(m - m_new)
        p = jnp.exp(s - m_new)
        v_t = jnp.concatenate([vt_ref[hh, first_blk + g] for g in range(half)], axis=1)
        acc = alpha * acc + jnp.dot(v_t, p.astype(BF16), preferred_element_type=F32)
        return m_new, acc

    def trip(t, carry, last):
        first = t * (2 * half)
        s_second = [score(hh, first + half) for hh in heads]
        carry = [attend(hh, carry[hh], s_ref[hh], first, last) for hh in heads]
        if not last:
            for hh in heads:
                s_ref[hh] = score(hh, first + 2 * half)
        carry = [attend(hh, carry[hh], s_second[hh], first + half, last) for hh in heads]
        return tuple(carry)

    for hh in heads:
        s_ref[hh] = score(hh, 0)

    state = tuple((jnp.full((1, blk), M_INIT, F32), jnp.zeros((dh + SUM_ROWS, blk), F32))
                  for _ in heads)
    last_trip = i // (2 * half)
    state = lax.fori_loop(0, last_trip, lambda t, c: trip(t, c, False), state)
    state = trip(last_trip, state, True)
    for hh in heads:
        _, acc = state[hh]
        out = acc[0:dh] / acc[dh:dh + 1]
        o_ref[:, hh * dh:(hh + 1) * dh] = out.T.astype(o_ref.dtype)


def _moba(qaug, kaug, vt):
    s = kaug.shape[1]
    nh = MOBA_HEADS
    return pl.pallas_call(
        _moba_kernel,
        out_shape=jax.ShapeDtypeStruct((s, WIDTH_A), BF16),
        grid=(N_HEADS_A // nh, s // MOBA_BLOCK),
        in_specs=[pl.BlockSpec((nh, 2 * HEAD_DIM_A, MOBA_BLOCK), lambda h, i: (h, 0, i)),
                  pl.BlockSpec((nh, s, 2 * HEAD_DIM_A), lambda h, i: (h, 0, 0)),
                  pl.BlockSpec((nh, N_BLOCKS, HEAD_DIM_A + SUM_ROWS, MOBA_BLOCK),
                               lambda h, i: (h, 0, 0, 0))],
        out_specs=pl.BlockSpec((MOBA_BLOCK, nh * HEAD_DIM_A), lambda h, i: (i, h)),
        scratch_shapes=[pltpu.VMEM((nh, MOBA_HALF * MOBA_BLOCK, MOBA_BLOCK), F32)],
        compiler_params=_params("arbitrary", "arbitrary"),
        name="moba_attention",
    )(qaug, kaug, vt)


MERGE_TN = 1024
MERGE_TM = 512


def _merge_kernel(ya_ref, yb_ref, ym_ref, ga_ref, gb_ref, gm_ref, w_ref, o_ref, wbf_ref):
    _cast_weight_once(w_ref, wbf_ref)
    for r in range(0, ya_ref.shape[0], SUB_ROWS):
        rows = slice(r, r + SUB_ROWS)
        merged = ga_ref[rows, :] * jnp.dot(ya_ref[rows, :], wbf_ref[0],
                                           preferred_element_type=F32)
        merged += gb_ref[rows, :] * jnp.dot(yb_ref[rows, :], wbf_ref[1],
                                            preferred_element_type=F32)
        merged += gm_ref[rows, :] * jnp.dot(ym_ref[rows, :], wbf_ref[2],
                                            preferred_element_type=F32)
        o_ref[rows, :] = merged.astype(o_ref.dtype)


def _merge(ya, yb, ym, gates, w_branch, layer):
    s = ya.shape[0]
    tn = MERGE_TN
    nt = D_MODEL // tn
    tm = MERGE_TM
    act = pl.BlockSpec((tm, WIDTH_A), lambda n, m: (m, 0))
    return pl.pallas_call(
        _merge_kernel,
        out_shape=jax.ShapeDtypeStruct((s, D_MODEL), BF16),
        grid=(nt, s // tm),
        in_specs=[act, act, act,
                  pl.BlockSpec((tm, tn), lambda n, m: (m, n)),
                  pl.BlockSpec((tm, tn), lambda n, m: (m, nt + n)),
                  pl.BlockSpec((tm, tn), lambda n, m: (m, 2 * nt + n)),
                  pl.BlockSpec((None, 3, WIDTH_A, tn), lambda n, m: (layer, 0, 0, n))],
        out_specs=pl.BlockSpec((tm, tn), lambda n, m: (m, n)),
        scratch_shapes=[pltpu.VMEM((3, WIDTH_A, tn), BF16)],
        compiler_params=_params("arbitrary", "arbitrary"),
        name="branch_merge",
    )(ya, yb, ym, gates, gates, gates, w_branch)


def _resid_proj_kernel(a_ref, w_ref, x_ref, o_ref, wbf_ref):
    _cast_weight_once(w_ref, wbf_ref)
    def emit(rows, acc):
        o_ref[rows, :] = x_ref[rows, :] + acc

    _sub_dots(a_ref, wbf_ref[...], emit)


def _resid_proj(a, w, x, layer, tm, tn, name):
    s, k = a.shape
    n_out = w.shape[2]
    return pl.pallas_call(
        _resid_proj_kernel,
        out_shape=jax.ShapeDtypeStruct((s, n_out), F32),
        grid=(n_out // tn, s // tm),
        in_specs=[pl.BlockSpec((tm, k), lambda n, m: (m, 0)),
                  pl.BlockSpec((None, k, tn), lambda n, m: (layer, 0, n)),
                  pl.BlockSpec((tm, tn), lambda n, m: (m, n))],
        out_specs=pl.BlockSpec((tm, tn), lambda n, m: (m, n)),
        scratch_shapes=[pltpu.VMEM((k, tn), BF16)],
        compiler_params=_params("arbitrary", "arbitrary"),
        name=name,
    )(a, w, x)


def _out_proj_norm_kernel(a_ref, w_ref, x_ref, g_ref, o_ref, h_ref, wbf_ref):
    @pl.when(pl.program_id(0) == 0)
    def _():
        wbf_ref[...] = w_ref[...].astype(BF16)

    def emit(rows, acc):
        x = x_ref[rows, :] + acc
        o_ref[rows, :] = x
        ms = jnp.mean(x * x, axis=-1, keepdims=True)
        h_ref[rows, :] = (x * lax.rsqrt(ms + EPS) * g_ref[...]).astype(h_ref.dtype)

    _sub_dots(a_ref, wbf_ref[...], emit)


def _out_proj_norm(a, w, x, g_all, layer):
    s, k = a.shape
    tm = 512
    return pl.pallas_call(
        _out_proj_norm_kernel,
        out_shape=(jax.ShapeDtypeStruct((s, D_MODEL), F32),
                   jax.ShapeDtypeStruct((s, D_MODEL), BF16)),
        grid=(s // tm,),
        in_specs=[pl.BlockSpec((tm, k), lambda m: (m, 0)),
                  pl.BlockSpec((None, k, D_MODEL), lambda m: (layer, 0, 0),
                               pipeline_mode=pl.Buffered(1)),
                  pl.BlockSpec((tm, D_MODEL), lambda m: (m, 0)),
                  pl.BlockSpec((None, 1, D_MODEL), lambda m: (layer, 0, 0))],
        out_specs=(pl.BlockSpec((tm, D_MODEL), lambda m: (m, 0)),
                   pl.BlockSpec((tm, D_MODEL), lambda m: (m, 0))),
        scratch_shapes=[pltpu.VMEM((k, D_MODEL), BF16)],
        compiler_params=_params("arbitrary"),
        name="out_proj_norm",
    )(a, w, x, g_all)


FFN_TN = 512


def _ffn_up_kernel(h_ref, wg_ref, wu_ref, o_ref, wgbf_ref, wubf_ref):
    @pl.when(pl.program_id(1) == 0)
    def _():
        wgbf_ref[...] = wg_ref[...].astype(BF16)
        wubf_ref[...] = wu_ref[...].astype(BF16)
    for r in range(0, h_ref.shape[0], SUB_ROWS):
        rows = slice(r, r + SUB_ROWS)
        gt = jnp.dot(h_ref[rows, :], wgbf_ref[...], preferred_element_type=F32)
        up = jnp.dot(h_ref[rows, :], wubf_ref[...], preferred_element_type=F32)
        o_ref[rows, :] = (jax.nn.silu(gt) * up).astype(o_ref.dtype)


def _ffn_up(h, w_gate_up, layer):
    s = h.shape[0]
    tn = FFN_TN
    nt = D_FF // tn
    tm = TM_BIG
    return pl.pallas_call(
        _ffn_up_kernel,
        out_shape=jax.ShapeDtypeStruct((s, D_FF), BF16),
        grid=(nt, s // tm),
        in_specs=[pl.BlockSpec((tm, D_MODEL), lambda n, m: (m, 0)),
                  pl.BlockSpec((None, D_MODEL, tn), lambda n, m: (layer, 0, n)),
                  pl.BlockSpec((None, D_MODEL, tn), lambda n, m: (layer, 0, nt + n))],
        out_specs=pl.BlockSpec((tm, tn), lambda n, m: (m, n)),
        scratch_shapes=[pltpu.VMEM((D_MODEL, tn), BF16), pltpu.VMEM((D_MODEL, tn), BF16)],
        compiler_params=_params("arbitrary", "arbitrary"),
        name="ffn_up",
    )(h, w_gate_up, w_gate_up)


def kernel(x, mem, g_mix, w_in, gq_a, gk_a, g_sgu, w_sgu, b_sgu, gq_m, gk_m, g_mem,
           w_mem_kv, w_branch, w_out, g_ffn, w_gate_up, w_down):
    b, s, d = x.shape
    assert (b, s, d) == (1, SEQ, D_MODEL) and mem.shape == (1, N_MEM, D_MODEL)
    x2 = x.reshape(s, d)
    mem2 = mem.reshape(N_MEM, d)

    def row(p):
        return p.reshape(DEPTH, 1, p.shape[-1])

    g_mix3, g_ffn3, g_mem3, g_sgu3 = row(g_mix), row(g_ffn), row(g_mem), row(g_sgu)
    gq_a3, gk_a3, gq_m3, gk_m3 = row(gq_a), row(gk_a), row(gq_m), row(gk_m)
    b_sgu_t = jnp.swapaxes(b_sgu, 1, 2)
    slopes = 2.0 ** (-8.0 * jnp.arange(1, N_HEADS_A + 1, dtype=F32) / N_HEADS_A)
    slopes = jnp.broadcast_to(slopes[:, None, None], (N_HEADS_A, 1, MOBA_BLOCK))

    for layer in range(DEPTH):
        h = _rmsnorm_bf16(x2, g_mix3, layer)
        qaug, kaug, vt = _qkv_proj(h, w_in, gq_a3, gk_a3, slopes, layer)
        yb = _sgu_branch(h, w_in, g_sgu3, w_sgu, b_sgu_t, layer)
        kvm = _memkv_proj(mem2, g_mem3, w_mem_kv, gk_m3, layer)
        ym = _qm_memattn(h, w_in, gq_m3, kvm, layer)
        gates = _gates_proj(h, w_in, layer)
        ya = _moba(qaug, kaug, vt)
        merged = _merge(ya, yb, ym, gates, w_branch, layer)
        x2, hf = _out_proj_norm(merged, w_out, x2, g_ffn3, layer)
        act = _ffn_up(hf, w_gate_up, layer)
        x2 = _resid_proj(act, w_down, x2, layer, 512, 512, "ffn_down")
    return x2.reshape(b, s, d)
```

```python
import functools

import jax
import jax.numpy as jnp
from jax import lax
from jax.experimental import pallas as pl
from jax.experimental.pallas import tpu as pltpu

F32 = jnp.float32
BF16 = jnp.bfloat16

D_MODEL = 2048
SEQ = 8192
DEPTH = 2
N_HEADS_A = 8
HEAD_DIM_A = 128
WIDTH_A = N_HEADS_A * HEAD_DIM_A
MOBA_BLOCK = 256
MOBA_TOPK = 3
N_BLOCKS = SEQ // MOBA_BLOCK
SGU_WIDTH = 1024
SGU_GROUPS = 8
SGU_CHUNK = 128
N_MEM = 256
N_HEADS_M = 4
HEAD_DIM_M = 256
WIDTH_M = N_HEADS_M * HEAD_DIM_M
D_FF = 5632
NEG_INF = -1e30
EPS = 1e-6

COL_QKV = 0
COL_UV = 3 * WIDTH_A
COL_QM = COL_UV + 2 * SGU_WIDTH
COL_GATES = COL_QM + WIDTH_M

VMEM_LIMIT_BYTES = 60 * 1024 * 1024

TM = 1024
TM_BIG = 2048
TN = 1024
SUB_ROWS = 256


def _params(*semantics):
    return pltpu.CompilerParams(dimension_semantics=semantics,
                                vmem_limit_bytes=VMEM_LIMIT_BYTES)


def _cast_weight_once(w_ref, wbf_ref):
    @pl.when(pl.program_id(1) == 0)
    def _():
        wbf_ref[...] = w_ref[...].astype(BF16)


def _group_rmsnorm(a, g, width):
    outs = []
    for s in range(0, a.shape[1], width):
        blk = a[:, s:s + width]
        ms = jnp.mean(blk * blk, axis=-1, keepdims=True)
        outs.append(blk * lax.rsqrt(ms + EPS) * g)
    return jnp.concatenate(outs, axis=1)


def _norm_kernel(x_ref, g_ref, o_ref):
    x = x_ref[...]
    ms = jnp.mean(x * x, axis=-1, keepdims=True)
    o_ref[...] = (x * lax.rsqrt(ms + EPS) * g_ref[...]).astype(o_ref.dtype)


def _rmsnorm_bf16(x, g_all, layer):
    tm = 512
    return pl.pallas_call(
        _norm_kernel,
        out_shape=jax.ShapeDtypeStruct(x.shape, BF16),
        grid=(x.shape[0] // tm,),
        in_specs=[pl.BlockSpec((tm, D_MODEL), lambda m: (m, 0)),
                  pl.BlockSpec((None, 1, D_MODEL), lambda m: (layer, 0, 0))],
        out_specs=pl.BlockSpec((tm, D_MODEL), lambda m: (m, 0)),
        compiler_params=_params("arbitrary"),
        name="rmsnorm",
    )(x, g_all)


def _sub_dots(lhs_ref, rhs, emit):
    def sub_dot(r):
        rows = slice(r, r + SUB_ROWS)
        return rows, jnp.dot(lhs_ref[rows, :], rhs, preferred_element_type=F32)

    n_rows = lhs_ref.shape[0]
    cur = sub_dot(0)
    for r in range(0, n_rows, SUB_ROWS):
        nxt = sub_dot(r + SUB_ROWS) if r + SUB_ROWS < n_rows else None
        emit(*cur)
        cur = nxt


COL_KEY_OFF = N_BLOCKS
COL_BLK_OFF = N_BLOCKS + 1
COL_ONES = N_BLOCKS + 2
SUM_ROWS = 16


def _head_norm(a, g):
    ms = jnp.mean(a * a, axis=-1, keepdims=True)
    return a * lax.rsqrt(ms + EPS) * g


def _k_kernel(h_ref, w_ref, gk_ref, slope_ref, kaug_ref, kmean_ref, wbf_ref):
    m = pl.program_id(1)
    _cast_weight_once(w_ref, wbf_ref)
    assert SUB_ROWS == MOBA_BLOCK
    blk, dh = MOBA_BLOCK, HEAD_DIM_A
    lane = lax.broadcasted_iota(jnp.int32, (blk, dh), 1)
    lane_row = lax.broadcasted_iota(jnp.int32, (1, dh), 1)
    key_off = lax.broadcasted_iota(jnp.int32, (blk, dh), 0).astype(F32)

    def emit(rows, acc):
        b_local = rows.start // blk
        b = m * (h_ref.shape[0] // blk) + b_local
        blk_start = jnp.full((1, dh), b * blk, jnp.int32).astype(F32)
        onehot = jnp.where(lane == b, 1.0, jnp.where(lane == COL_ONES, 1.0, 0.0))
        for hh in range(N_HEADS_A):
            cols = slice(hh * dh, (hh + 1) * dh)
            kn = _head_norm(acc[:, cols], gk_ref[...])
            kmean_ref[b_local:b_local + 1, cols] = jnp.mean(kn, axis=0, keepdims=True)
            sl = slope_ref[hh][:, :dh]
            extra = (onehot + jnp.where(lane == COL_KEY_OFF, sl * key_off, 0.0)
                     + jnp.where(lane_row == COL_BLK_OFF, sl * blk_start, 0.0))
            kaug_ref[hh, rows, 0:dh] = kn.astype(BF16)
            kaug_ref[hh, rows, dh:2 * dh] = extra.astype(BF16)

    _sub_dots(h_ref, wbf_ref[...], emit)


def _q_kernel(h_ref, w_ref, gq_ref, kmean_ref, slope_ref, qaug_ref, wbf_ref):
    m = pl.program_id(1)
    _cast_weight_once(w_ref, wbf_ref)
    assert SUB_ROWS == MOBA_BLOCK
    blk, dh = MOBA_BLOCK, HEAD_DIM_A
    blk_id = lax.broadcasted_iota(jnp.int32, (N_BLOCKS, blk), 0)
    blk_id_f = blk_id.astype(F32)
    rest = lax.broadcasted_iota(jnp.int32, (dh - N_BLOCKS, blk), 0) + N_BLOCKS

    def emit(rows, acc):
        i = m * (h_ref.shape[0] // blk) + rows.start // blk
        past = blk_id < i
        q_start = jnp.full((1, blk), i * blk, jnp.int32).astype(F32)
        for hh in range(N_HEADS_A):
            cols = slice(hh * dh, (hh + 1) * dh)
            q = _head_norm(acc[:, cols], gq_ref[...]) * dh ** -0.5
            q_t = q.T
            gate = jnp.dot(kmean_ref[:, cols].astype(BF16), q_t.astype(BF16),
                           preferred_element_type=F32)
            gate = jnp.where(past, gate, NEG_INF)
            pen = jnp.full((N_BLOCKS, blk), NEG_INF, F32)
            for _ in range(MOBA_TOPK):
                mx = jnp.max(gate, axis=0, keepdims=True)
                first = jnp.min(jnp.where(gate == mx, blk_id_f, float(N_BLOCKS)),
                                axis=0, keepdims=True)
                hit = blk_id_f == first
                pen = jnp.where(jnp.logical_and(hit, mx > 0.5 * NEG_INF), 0.0, pen)
                gate = jnp.where(hit, -3e38, gate)
            pen = jnp.where(blk_id == i, 0.0, pen)
            extra_q = jnp.where(rest == COL_ONES, -slope_ref[hh] * q_start,
                                jnp.where(rest < COL_ONES, 1.0, 0.0))
            qaug_ref[hh, :, rows] = jnp.concatenate([q_t, pen, extra_q], axis=0).astype(BF16)

    _sub_dots(h_ref, wbf_ref[...], emit)


def _v_kernel(h_ref, w_ref, vt_ref, wbf_ref):
    _cast_weight_once(w_ref, wbf_ref)
    assert SUB_ROWS == MOBA_BLOCK
    blk, dh = MOBA_BLOCK, HEAD_DIM_A

    def emit(rows, acc):
        b_local = rows.start // blk
        for hh in range(N_HEADS_A):
            vt_ref[hh, b_local, 0:dh, :] = acc[:, hh * dh:(hh + 1) * dh].T.astype(BF16)
            vt_ref[hh, b_local, dh:dh + SUM_ROWS, :] = jnp.ones((SUM_ROWS, blk), BF16)

    _sub_dots(h_ref, wbf_ref[...], emit)


def _qkv_proj(h, w_in, gq_a, gk_a, slopes, layer):
    s = h.shape[0]
    tm, dh, nh = TM, HEAD_DIM_A, N_HEADS_A
    grid = (1, s // tm)
    h_spec = pl.BlockSpec((tm, D_MODEL), lambda n, m: (m, 0))
    gain = pl.BlockSpec((None, 1, dh), lambda n, m: (layer, 0, 0))
    slope = pl.BlockSpec((nh, 1, MOBA_BLOCK), lambda n, m: (0, 0, 0))
    scratch = [pltpu.VMEM((D_MODEL, TN), BF16)]
    params = _params("arbitrary", "arbitrary")

    def w_spec(col):
        return pl.BlockSpec((None, D_MODEL, TN), lambda n, m: (layer, 0, col))

    kaug, kmeans = pl.pallas_call(
        _k_kernel,
        out_shape=(jax.ShapeDtypeStruct((nh, s, 2 * dh), BF16),
                   jax.ShapeDtypeStruct((s // tm, tm // MOBA_BLOCK, WIDTH_A), F32)),
        grid=grid,
        in_specs=[h_spec, w_spec(1), gain, slope],
        out_specs=(pl.BlockSpec((nh, tm, 2 * dh), lambda n, m: (0, m, 0)),
                   pl.BlockSpec((None, tm // MOBA_BLOCK, WIDTH_A), lambda n, m: (m, 0, 0))),
        scratch_shapes=scratch, compiler_params=params, name="k_proj",
    )(h, w_in, gk_a, slopes)
    kmean = kmeans.reshape(N_BLOCKS, WIDTH_A)
    qaug = pl.pallas_call(
        _q_kernel,
        out_shape=jax.ShapeDtypeStruct((nh, 2 * dh, s), BF16),
        grid=grid,
        in_specs=[h_spec, w_spec(0), gain,
                  pl.BlockSpec((N_BLOCKS, WIDTH_A), lambda n, m: (0, 0)), slope],
        out_specs=pl.BlockSpec((nh, 2 * dh, tm), lambda n, m: (0, 0, m)),
        scratch_shapes=scratch, compiler_params=params, name="q_proj",
    )(h, w_in, gq_a, kmean, slopes)
    vt = pl.pallas_call(
        _v_kernel,
        out_shape=jax.ShapeDtypeStruct((nh, N_BLOCKS, dh + SUM_ROWS, MOBA_BLOCK), BF16),
        grid=grid,
        in_specs=[h_spec, w_spec(2)],
        out_specs=pl.BlockSpec((nh, tm // MOBA_BLOCK, dh + SUM_ROWS, MOBA_BLOCK),
                               lambda n, m: (0, m, 0, 0)),
        scratch_shapes=scratch, compiler_params=params, name="v_proj",
    )(h, w_in)
    return qaug, kaug, vt


def _sgu_kernel(h_ref, wu_ref, wv_ref, g_ref, ws_ref, bt_ref, o_ref,
                wubf_ref, wvbf_ref, wsbf_ref):
    @pl.when(pl.program_id(0) == 0)
    def _():
        wubf_ref[...] = wu_ref[...].astype(BF16)
        wvbf_ref[...] = wv_ref[...].astype(BF16)
        row = lax.broadcasted_iota(jnp.int32, (SGU_CHUNK, SGU_CHUNK), 0)
        col = lax.broadcasted_iota(jnp.int32, (SGU_CHUNK, SGU_CHUNK), 1)
        for g in range(SGU_GROUPS):
            wsbf_ref[g] = jnp.where(col <= row, ws_ref[g], 0.0).astype(BF16)

    def proj(r):
        rows = slice(r, r + SUB_ROWS)
        return (rows,
                jnp.dot(h_ref[rows, :], wubf_ref[...], preferred_element_type=F32),
                jnp.dot(h_ref[rows, :], wvbf_ref[...], preferred_element_type=F32))

    def mix(rows, u_acc, v_acc):
        u = jax.nn.gelu(u_acc)
        v = jax.nn.gelu(v_acc)
        ms = jnp.mean(v * v, axis=-1, keepdims=True)
        vn = (v * lax.rsqrt(ms + EPS) * g_ref[...]).astype(BF16)
        chunks = [slice(c, c + SGU_CHUNK) for c in range(0, SUB_ROWS, SGU_CHUNK)]
        for g in range(SGU_GROUPS):
            cols = slice(g * SGU_CHUNK, (g + 1) * SGU_CHUNK)
            v_all = jnp.concatenate([vn[c, cols] for c in chunks], axis=1)
            mixed = jnp.dot(wsbf_ref[g], v_all, preferred_element_type=F32) + bt_ref[:, g:g + 1]
            for c in chunks:
                o_ref[rows.start + c.start:rows.start + c.stop, cols] = (
                    u[c, cols] * mixed[:, c]).astype(o_ref.dtype)

    n_rows = h_ref.shape[0]
    cur = proj(0)
    for r in range(0, n_rows, SUB_ROWS):
        nxt = proj(r + SUB_ROWS) if r + SUB_ROWS < n_rows else None
        mix(*cur)
        cur = nxt


def _sgu_branch(h, w_in, g_sgu, w_sgu, b_sgu_t, layer):
    s = h.shape[0]
    off = COL_UV // TN
    once = pl.Buffered(1)
    return pl.pallas_call(
        _sgu_kernel,
        out_shape=jax.ShapeDtypeStruct((s, SGU_WIDTH), BF16),
        grid=(s // TM,),
        in_specs=[pl.BlockSpec((TM, D_MODEL), lambda m: (m, 0)),
                  pl.BlockSpec((None, D_MODEL, TN), lambda m: (layer, 0, off),
                               pipeline_mode=once),
                  pl.BlockSpec((None, D_MODEL, TN), lambda m: (layer, 0, off + 1),
                               pipeline_mode=once),
                  pl.BlockSpec((None, 1, SGU_WIDTH), lambda m: (layer, 0, 0)),
                  pl.BlockSpec((None, SGU_GROUPS, SGU_CHUNK, SGU_CHUNK),
                               lambda m: (layer, 0, 0, 0)),
                  pl.BlockSpec((None, SGU_CHUNK, SGU_GROUPS), lambda m: (layer, 0, 0))],
        out_specs=pl.BlockSpec((TM, SGU_WIDTH), lambda m: (m, 0)),
        scratch_shapes=[pltpu.VMEM((D_MODEL, TN), BF16), pltpu.VMEM((D_MODEL, TN), BF16),
                        pltpu.VMEM((SGU_GROUPS, SGU_CHUNK, SGU_CHUNK), BF16)],
        compiler_params=_params("arbitrary"),
        name="sgu_branch",
    )(h, w_in, w_in, g_sgu, w_sgu, b_sgu_t)


def _memkv_kernel(mem_ref, gmem_ref, w_ref, gk_ref, o_ref):
    n = pl.program_id(0)
    x = mem_ref[...]
    ms = jnp.mean(x * x, axis=-1, keepdims=True)
    hm = (x * lax.rsqrt(ms + EPS) * gmem_ref[...]).astype(BF16)
    acc = jnp.dot(hm, w_ref[...].astype(BF16), preferred_element_type=F32)

    @pl.when(n == 0)
    def _():
        o_ref[...] = _group_rmsnorm(acc, gk_ref[...], HEAD_DIM_M).astype(o_ref.dtype)

    @pl.when(n == 1)
    def _():
        o_ref[...] = acc.astype(o_ref.dtype)


def _memkv_proj(mem, g_mem, w_mem_kv, gk_m, layer):
    return pl.pallas_call(
        _memkv_kernel,
        out_shape=jax.ShapeDtypeStruct((2, N_MEM, WIDTH_M), BF16),
        grid=(2,),
        in_specs=[pl.BlockSpec((N_MEM, D_MODEL), lambda n: (0, 0)),
                  pl.BlockSpec((None, 1, D_MODEL), lambda n: (layer, 0, 0)),
                  pl.BlockSpec((None, D_MODEL, WIDTH_M), lambda n: (layer, 0, n)),
                  pl.BlockSpec((None, 1, HEAD_DIM_M), lambda n: (layer, 0, 0))],
        out_specs=pl.BlockSpec((None, N_MEM, WIDTH_M), lambda n: (n, 0, 0)),
        compiler_params=_params("arbitrary"),
        name="memkv_proj",
    )(mem, g_mem, w_mem_kv, gk_m)


def _qm_kernel(h_ref, w_ref, gq_ref, km_ref, vm_ref, o_ref, wbf_ref):
    _cast_weight_once(w_ref, wbf_ref)

    def emit(rows, acc):
        heads = [slice(hh * HEAD_DIM_M, (hh + 1) * HEAD_DIM_M) for hh in range(N_HEADS_M)]
        scores = []
        for cols in heads:
            qh = acc[:, cols]
            ms = jnp.mean(qh * qh, axis=-1, keepdims=True)
            qn = (qh * lax.rsqrt(ms + EPS) * gq_ref[...]).astype(BF16)
            scores.append(lax.dot_general(qn, km_ref[:, cols], (((1,), (1,)), ((), ())),
                                          preferred_element_type=F32) * HEAD_DIM_M ** -0.5)
        weights = []
        for sc in scores:
            mx = jnp.max(sc, axis=-1, keepdims=True)
            e = jnp.exp(sc - mx)
            weights.append((e / jnp.sum(e, axis=-1, keepdims=True)).astype(BF16))
        for cols, p in zip(heads, weights):
            o_ref[rows, cols] = jnp.dot(
                p, vm_ref[:, cols], preferred_element_type=F32).astype(o_ref.dtype)

    _sub_dots(h_ref, wbf_ref[...], emit)


def _qm_memattn(h, w_in, gq_m, kvm, layer):
    s = h.shape[0]
    off = COL_QM // TN
    tm = TM_BIG
    return pl.pallas_call(
        _qm_kernel,
        out_shape=jax.ShapeDtypeStruct((s, WIDTH_M), BF16),
        grid=(1, s // tm),
        in_specs=[pl.BlockSpec((tm, D_MODEL), lambda n, m: (m, 0)),
                  pl.BlockSpec((None, D_MODEL, TN), lambda n, m: (layer, 0, off)),
                  pl.BlockSpec((None, 1, HEAD_DIM_M), lambda n, m: (layer, 0, 0)),
                  pl.BlockSpec((None, N_MEM, WIDTH_M), lambda n, m: (0, 0, 0)),
                  pl.BlockSpec((None, N_MEM, WIDTH_M), lambda n, m: (1, 0, 0))],
        out_specs=pl.BlockSpec((tm, WIDTH_M), lambda n, m: (m, 0)),
        scratch_shapes=[pltpu.VMEM((D_MODEL, TN), BF16)],
        compiler_params=_params("arbitrary", "arbitrary"),
        name="qm_memattn",
    )(h, w_in, gq_m, kvm, kvm)


def _gates_kernel(h_ref, w_ref, o_ref, wbf_ref):
    _cast_weight_once(w_ref, wbf_ref)
    def emit(rows, acc):
        o_ref[rows, :] = jax.nn.sigmoid(acc).astype(o_ref.dtype)

    _sub_dots(h_ref, wbf_ref[...], emit)


def _gates_proj(h, w_in, layer):
    s = h.shape[0]
    off = COL_GATES // TN
    n_tiles = 3 * D_MODEL // TN
    return pl.pallas_call(
        _gates_kernel,
        out_shape=jax.ShapeDtypeStruct((s, 3 * D_MODEL), BF16),
        grid=(n_tiles, s // TM_BIG),
        in_specs=[pl.BlockSpec((TM_BIG, D_MODEL), lambda n, m: (m, 0)),
                  pl.BlockSpec((None, D_MODEL, TN), lambda n, m: (layer, 0, off + n))],
        out_specs=pl.BlockSpec((TM_BIG, TN), lambda n, m: (m, n)),
        scratch_shapes=[pltpu.VMEM((D_MODEL, TN), BF16)],
        compiler_params=_params("arbitrary", "arbitrary"),
        name="gates_proj",
    )(h, w_in)


MOBA_HALF = 2
MOBA_HEADS = 2
M_INIT = -3e38


def _moba_kernel(qaug_ref, kaug_ref, vt_ref, o_ref, s_ref):
    i = pl.program_id(1)
    blk, dh = MOBA_BLOCK, HEAD_DIM_A
    heads = range(MOBA_HEADS)
    half = MOBA_HALF
    half_rows = half * blk
    key_minus_qry = (lax.broadcasted_iota(jnp.int32, (half_rows, blk), 0)
                     - lax.broadcasted_iota(jnp.int32, (half_rows, blk), 1))

    def score(hh, first_blk):
        rows = pl.ds(pl.multiple_of(first_blk * blk, half_rows), half_rows)
        return jnp.dot(kaug_ref[hh, rows, :], qaug_ref[hh], preferred_element_type=F32)

    def attend(hh, st, s, first_blk, causal):
        m, acc = st
        if causal:
            s = jnp.where(key_minus_qry <= (i - first_blk) * blk, s, NEG_INF)
        m_new = jnp.maximum(m, jnp.max(s, axis=0, keepdims=True))
        alpha = jnp.exp(m - m_new)
        p = jnp.exp(s - m_new)
        v_t = jnp.concatenate([vt_ref[hh, first_blk + g] for g in range(half)], axis=1)
        acc = alpha * acc + jnp.dot(v_t, p.astype(BF16), preferred_element_type=F32)
        return m_new, acc

    def trip(t, carry, last):
        first = t * (2 * half)
        if last:
            carry = tuple(attend(hh, carry[hh], s_ref[hh], first, True) for hh in heads)

            def second_half(c):
                s_second = [score(hh, first + half) for hh in heads]
                return tuple(attend(hh, c[hh], s_second[hh], first + half, True)
                             for hh in heads)

            return lax.cond(i >= first + half, second_half, lambda c: c, carry)
        s_second = [score(hh, first + half) for hh in heads]
        carry = [attend(hh, carry[hh], s_ref[hh], first, False) for hh in heads]
        for hh in heads:
            s_ref[hh] = score(hh, first + 2 * half)
        carry = [attend(hh, carry[hh], s_second[hh], first + half, False) for hh in heads]
        return tuple(carry)

    for hh in heads:
        s_ref[hh] = score(hh, 0)

    state = tuple((jnp.full((1, blk), M_INIT, F32), jnp.zeros((dh + SUM_ROWS, blk), F32))
                  for _ in heads)
    last_trip = i // (2 * half)
    state = lax.fori_loop(0, last_trip // 2,
                          lambda u, c: trip(2 * u + 1, trip(2 * u, c, False), False), state)
    state = lax.fori_loop(0, last_trip % 2, lambda _, c: trip(last_trip - 1, c, False), state)
    state = trip(last_trip, state, True)
    for hh in heads:
        _, acc = state[hh]
        out = acc[0:dh] / acc[dh:dh + 1]
        o_ref[:, hh * dh:(hh + 1) * dh] = out.T.astype(o_ref.dtype)


def _moba(qaug, kaug, vt):
    s = kaug.shape[1]
    nh = MOBA_HEADS
    return pl.pallas_call(
        _moba_kernel,
        out_shape=jax.ShapeDtypeStruct((s, WIDTH_A), BF16),
        grid=(N_HEADS_A // nh, s // MOBA_BLOCK),
        in_specs=[pl.BlockSpec((nh, 2 * HEAD_DIM_A, MOBA_BLOCK), lambda h, i: (h, 0, i)),
                  pl.BlockSpec((nh, s, 2 * HEAD_DIM_A), lambda h, i: (h, 0, 0)),
                  pl.BlockSpec((nh, N_BLOCKS, HEAD_DIM_A + SUM_ROWS, MOBA_BLOCK),
                               lambda h, i: (h, 0, 0, 0))],
        out_specs=pl.BlockSpec((MOBA_BLOCK, nh * HEAD_DIM_A), lambda h, i: (i, h)),
        scratch_shapes=[pltpu.VMEM((nh, MOBA_HALF * MOBA_BLOCK, MOBA_BLOCK), F32)],
        compiler_params=_params("arbitrary", "arbitrary"),
        name="moba_attention",
    )(qaug, kaug, vt)


MERGE_TN = 1024
MERGE_TM = 512


def _merge_kernel(ya_ref, yb_ref, ym_ref, ga_ref, gb_ref, gm_ref, w_ref, o_ref, wbf_ref):
    _cast_weight_once(w_ref, wbf_ref)
    for r in range(0, ya_ref.shape[0], SUB_ROWS):
        rows = slice(r, r + SUB_ROWS)
        merged = ga_ref[rows, :] * jnp.dot(ya_ref[rows, :], wbf_ref[0],
                                           preferred_element_type=F32)
        merged += gb_ref[rows, :] * jnp.dot(yb_ref[rows, :], wbf_ref[1],
                                            preferred_element_type=F32)
        merged += gm_ref[rows, :] * jnp.dot(ym_ref[rows, :], wbf_ref[2],
                                            preferred_element_type=F32)
        o_ref[rows, :] = merged.astype(o_ref.dtype)


def _merge(ya, yb, ym, gates, w_branch, layer):
    s = ya.shape[0]
    tn = MERGE_TN
    nt = D_MODEL // tn
    tm = MERGE_TM
    act = pl.BlockSpec((tm, WIDTH_A), lambda n, m: (m, 0))
    return pl.pallas_call(
        _merge_kernel,
        out_shape=jax.ShapeDtypeStruct((s, D_MODEL), BF16),
        grid=(nt, s // tm),
        in_specs=[act, act, act,
                  pl.BlockSpec((tm, tn), lambda n, m: (m, n)),
                  pl.BlockSpec((tm, tn), lambda n, m: (m, nt + n)),
                  pl.BlockSpec((tm, tn), lambda n, m: (m, 2 * nt + n)),
                  pl.BlockSpec((None, 3, WIDTH_A, tn), lambda n, m: (layer, 0, 0, n))],
        out_specs=pl.BlockSpec((tm, tn), lambda n, m: (m, n)),
        scratch_shapes=[pltpu.VMEM((3, WIDTH_A, tn), BF16)],
        compiler_params=_params("arbitrary", "arbitrary"),
        name="branch_merge",
    )(ya, yb, ym, gates, gates, gates, w_branch)


def _resid_proj_kernel(a_ref, w_ref, x_ref, o_ref, wbf_ref):
    _cast_weight_once(w_ref, wbf_ref)
    def emit(rows, acc):
        o_ref[rows, :] = x_ref[rows, :] + acc

    _sub_dots(a_ref, wbf_ref[...], emit)


def _resid_proj(a, w, x, layer, tm, tn, name):
    s, k = a.shape
    n_out = w.shape[2]
    return pl.pallas_call(
        _resid_proj_kernel,
        out_shape=jax.ShapeDtypeStruct((s, n_out), F32),
        grid=(n_out // tn, s // tm),
        in_specs=[pl.BlockSpec((tm, k), lambda n, m: (m, 0)),
                  pl.BlockSpec((None, k, tn), lambda n, m: (layer, 0, n)),
                  pl.BlockSpec((tm, tn), lambda n, m: (m, n))],
        out_specs=pl.BlockSpec((tm, tn), lambda n, m: (m, n)),
        scratch_shapes=[pltpu.VMEM((k, tn), BF16)],
        compiler_params=_params("arbitrary", "arbitrary"),
        name=name,
    )(a, w, x)


def _out_proj_norm_kernel(a_ref, w_ref, x_ref, g_ref, o_ref, h_ref, wbf_ref):
    @pl.when(pl.program_id(0) == 0)
    def _():
        wbf_ref[...] = w_ref[...].astype(BF16)

    def emit(rows, acc):
        x = x_ref[rows, :] + acc
        o_ref[rows, :] = x
        ms = jnp.mean(x * x, axis=-1, keepdims=True)
        h_ref[rows, :] = (x * lax.rsqrt(ms + EPS) * g_ref[...]).astype(h_ref.dtype)

    _sub_dots(a_ref, wbf_ref[...], emit)


def _out_proj_norm(a, w, x, g_all, layer):
    s, k = a.shape
    tm = 512
    return pl.pallas_call(
        _out_proj_norm_kernel,
        out_shape=(jax.ShapeDtypeStruct((s, D_MODEL), F32),
                   jax.ShapeDtypeStruct((s, D_MODEL), BF16)),
        grid=(s // tm,),
        in_specs=[pl.BlockSpec((tm, k), lambda m: (m, 0)),
                  pl.BlockSpec((None, k, D_MODEL), lambda m: (layer, 0, 0),
                               pipeline_mode=pl.Buffered(1)),
                  pl.BlockSpec((tm, D_MODEL), lambda m: (m, 0)),
                  pl.BlockSpec((None, 1, D_MODEL), lambda m: (layer, 0, 0))],
        out_specs=(pl.BlockSpec((tm, D_MODEL), lambda m: (m, 0)),
                   pl.BlockSpec((tm, D_MODEL), lambda m: (m, 0))),
        scratch_shapes=[pltpu.VMEM((k, D_MODEL), BF16)],
        compiler_params=_params("arbitrary"),
        name="out_proj_norm",
    )(a, w, x, g_all)


FFN_TN = 512


def _ffn_up_kernel(h_ref, wg_ref, wu_ref, o_ref, wgbf_ref, wubf_ref):
    @pl.when(pl.program_id(1) == 0)
    def _():
        wgbf_ref[...] = wg_ref[...].astype(BF16)
        wubf_ref[...] = wu_ref[...].astype(BF16)
    for r in range(0, h_ref.shape[0], SUB_ROWS):
        rows = slice(r, r + SUB_ROWS)
        gt = jnp.dot(h_ref[rows, :], wgbf_ref[...], preferred_element_type=F32)
        up = jnp.dot(h_ref[rows, :], wubf_ref[...], preferred_element_type=F32)
        o_ref[rows, :] = (jax.nn.silu(gt) * up).astype(o_ref.dtype)


def _ffn_up(h, w_gate_up, layer):
    s = h.shape[0]
    tn = FFN_TN
    nt = D_FF // tn
    tm = TM_BIG
    return pl.pallas_call(
        _ffn_up_kernel,
        out_shape=jax.ShapeDtypeStruct((s, D_FF), BF16),
        grid=(nt, s // tm),
        in_specs=[pl.BlockSpec((tm, D_MODEL), lambda n, m: (m, 0)),
                  pl.BlockSpec((None, D_MODEL, tn), lambda n, m: (layer, 0, n)),
                  pl.BlockSpec((None, D_MODEL, tn), lambda n, m: (layer, 0, nt + n))],
        out_specs=pl.BlockSpec((tm, tn), lambda n, m: (m, n)),
        scratch_shapes=[pltpu.VMEM((D_MODEL, tn), BF16), pltpu.VMEM((D_MODEL, tn), BF16)],
        compiler_params=_params("arbitrary", "arbitrary"),
        name="ffn_up",
    )(h, w_gate_up, w_gate_up)


def kernel(x, mem, g_mix, w_in, gq_a, gk_a, g_sgu, w_sgu, b_sgu, gq_m, gk_m, g_mem,
           w_mem_kv, w_branch, w_out, g_ffn, w_gate_up, w_down):
    b, s, d = x.shape
    assert (b, s, d) == (1, SEQ, D_MODEL) and mem.shape == (1, N_MEM, D_MODEL)
    x2 = x.reshape(s, d)
    mem2 = mem.reshape(N_MEM, d)

    def row(p):
        return p.reshape(DEPTH, 1, p.shape[-1])

    g_mix3, g_ffn3, g_mem3, g_sgu3 = row(g_mix), row(g_ffn), row(g_mem), row(g_sgu)
    gq_a3, gk_a3, gq_m3, gk_m3 = row(gq_a), row(gk_a), row(gq_m), row(gk_m)
    b_sgu_t = jnp.swapaxes(b_sgu, 1, 2)
    slopes = 2.0 ** (-8.0 * jnp.arange(1, N_HEADS_A + 1, dtype=F32) / N_HEADS_A)
    slopes = jnp.broadcast_to(slopes[:, None, None], (N_HEADS_A, 1, MOBA_BLOCK))

    for layer in range(DEPTH):
        h = _rmsnorm_bf16(x2, g_mix3, layer)
        qaug, kaug, vt = _qkv_proj(h, w_in, gq_a3, gk_a3, slopes, layer)
        yb = _sgu_branch(h, w_in, g_sgu3, w_sgu, b_sgu_t, layer)
        kvm = _memkv_proj(mem2, g_mem3, w_mem_kv, gk_m3, layer)
        ym = _qm_memattn(h, w_in, gq_m3, kvm, layer)
        gates = _gates_proj(h, w_in, layer)
        ya = _moba(qaug, kaug, vt)
        merged = _merge(ya, yb, ym, gates, w_branch, layer)
        x2, hf = _out_proj_norm(merged, w_out, x2, g_ffn3, layer)
        act = _ffn_up(hf, w_gate_up, layer)
        x2 = _resid_proj(act, w_down, x2, layer, 512, 512, "ffn_down")
    return x2.reshape(b, s, d)
```

```python
import functools

import jax
import jax.numpy as jnp
from jax import lax
from jax.experimental import pallas as pl
from jax.experimental.pallas import tpu as pltpu

F32 = jnp.float32
BF16 = jnp.bfloat16

D_MODEL = 2048
SEQ = 8192
DEPTH = 2
N_HEADS_A = 8
HEAD_DIM_A = 128
WIDTH_A = N_HEADS_A * HEAD_DIM_A
MOBA_BLOCK = 256
MOBA_TOPK = 3
N_BLOCKS = SEQ // MOBA_BLOCK
SGU_WIDTH = 1024
SGU_GROUPS = 8
SGU_CHUNK = 128
N_MEM = 256
N_HEADS_M = 4
HEAD_DIM_M = 256
WIDTH_M = N_HEADS_M * HEAD_DIM_M
D_FF = 5632
NEG_INF = -1e30
EPS = 1e-6

COL_QKV = 0
COL_UV = 3 * WIDTH_A
COL_QM = COL_UV + 2 * SGU_WIDTH
COL_GATES = COL_QM + WIDTH_M

VMEM_LIMIT_BYTES = 60 * 1024 * 1024

TM = 1024
TM_BIG = 2048
TN = 1024
SUB_ROWS = 256


def _params(*semantics):
    return pltpu.CompilerParams(dimension_semantics=semantics,
                                vmem_limit_bytes=VMEM_LIMIT_BYTES)


def _cast_weight_once(w_ref, wbf_ref):
    @pl.when(pl.program_id(1) == 0)
    def _():
        wbf_ref[...] = w_ref[...].astype(BF16)


def _group_rmsnorm(a, g, width):
    outs = []
    for s in range(0, a.shape[1], width):
        blk = a[:, s:s + width]
        ms = jnp.mean(blk * blk, axis=-1, keepdims=True)
        outs.append(blk * lax.rsqrt(ms + EPS) * g)
    return jnp.concatenate(outs, axis=1)


def _sub_dots(lhs_ref, rhs, emit, lhs_rows=None):
    def sub_dot(r):
        rows = slice(r, r + SUB_ROWS)
        lhs = lhs_rows(rows) if lhs_rows is not None else lhs_ref[rows, :]
        return rows, jnp.dot(lhs, rhs, preferred_element_type=F32)

    n_rows = lhs_ref.shape[0]
    cur = sub_dot(0)
    for r in range(0, n_rows, SUB_ROWS):
        nxt = sub_dot(r + SUB_ROWS) if r + SUB_ROWS < n_rows else None
        emit(*cur)
        cur = nxt


COL_KEY_OFF = N_BLOCKS
COL_BLK_OFF = N_BLOCKS + 1
COL_ONES = N_BLOCKS + 2
SUM_ROWS = 16


def _head_norm(a, g):
    ms = jnp.mean(a * a, axis=-1, keepdims=True)
    return a * lax.rsqrt(ms + EPS) * g


def _k_kernel(x_ref, g_ref, w_ref, gk_ref, slope_ref, kaug_ref, kmean_ref, h_ref, wbf_ref):
    m = pl.program_id(1)

    def norm_rows(rows):
        x = x_ref[rows, :]
        ms = jnp.mean(x * x, axis=-1, keepdims=True)
        h = (x * lax.rsqrt(ms + EPS) * g_ref[...]).astype(BF16)
        h_ref[rows, :] = h
        return h

    _cast_weight_once(w_ref, wbf_ref)
    assert SUB_ROWS == MOBA_BLOCK
    blk, dh = MOBA_BLOCK, HEAD_DIM_A
    lane = lax.broadcasted_iota(jnp.int32, (blk, dh), 1)
    lane_row = lax.broadcasted_iota(jnp.int32, (1, dh), 1)
    key_off = lax.broadcasted_iota(jnp.int32, (blk, dh), 0).astype(F32)

    def emit(rows, acc):
        b_local = rows.start // blk
        b = m * (h_ref.shape[0] // blk) + b_local
        blk_start = jnp.full((1, dh), b * blk, jnp.int32).astype(F32)
        onehot = jnp.where(lane == b, 1.0, jnp.where(lane == COL_ONES, 1.0, 0.0))
        for hh in range(N_HEADS_A):
            cols = slice(hh * dh, (hh + 1) * dh)
            kn = _head_norm(acc[:, cols], gk_ref[...])
            kmean_ref[b_local:b_local + 1, cols] = jnp.mean(kn, axis=0, keepdims=True)
            sl = slope_ref[hh][:, :dh]
            extra = (onehot + jnp.where(lane == COL_KEY_OFF, sl * key_off, 0.0)
                     + jnp.where(lane_row == COL_BLK_OFF, sl * blk_start, 0.0))
            kaug_ref[hh, rows, 0:dh] = kn.astype(BF16)
            kaug_ref[hh, rows, dh:2 * dh] = extra.astype(BF16)

    _sub_dots(x_ref, wbf_ref[...], emit, lhs_rows=norm_rows)


def _q_kernel(h_ref, w_ref, gq_ref, kmean_ref, slope_ref, qaug_ref, wbf_ref):
    m = pl.program_id(1)
    _cast_weight_once(w_ref, wbf_ref)
    assert SUB_ROWS == MOBA_BLOCK
    blk, dh = MOBA_BLOCK, HEAD_DIM_A
    blk_id = lax.broadcasted_iota(jnp.int32, (N_BLOCKS, blk), 0)
    blk_id_f = blk_id.astype(F32)
    rest = lax.broadcasted_iota(jnp.int32, (dh - N_BLOCKS, blk), 0) + N_BLOCKS

    def emit(rows, acc):
        i = m * (h_ref.shape[0] // blk) + rows.start // blk
        past = blk_id < i
        q_start = jnp.full((1, blk), i * blk, jnp.int32).astype(F32)
        for hh in range(N_HEADS_A):
            cols = slice(hh * dh, (hh + 1) * dh)
            q = _head_norm(acc[:, cols], gq_ref[...]) * dh ** -0.5
            q_t = q.T
            gate = jnp.dot(kmean_ref[:, cols].astype(BF16), q_t.astype(BF16),
                           preferred_element_type=F32)
            gate = jnp.where(past, gate, NEG_INF)
            pen = jnp.full((N_BLOCKS, blk), NEG_INF, F32)
            for _ in range(MOBA_TOPK):
                mx = jnp.max(gate, axis=0, keepdims=True)
                first = jnp.min(jnp.where(gate == mx, blk_id_f, float(N_BLOCKS)),
                                axis=0, keepdims=True)
                hit = blk_id_f == first
                pen = jnp.where(jnp.logical_and(hit, mx > 0.5 * NEG_INF), 0.0, pen)
                gate = jnp.where(hit, -3e38, gate)
            pen = jnp.where(blk_id == i, 0.0, pen)
            extra_q = jnp.where(rest == COL_ONES, -slope_ref[hh] * q_start,
                                jnp.where(rest < COL_ONES, 1.0, 0.0))
            qaug_ref[hh, :, rows] = jnp.concatenate([q_t, pen, extra_q], axis=0).astype(BF16)

    _sub_dots(h_ref, wbf_ref[...], emit)


def _v_kernel(h_ref, w_ref, vt_ref, wbf_ref):
    _cast_weight_once(w_ref, wbf_ref)
    assert SUB_ROWS == MOBA_BLOCK
    blk, dh = MOBA_BLOCK, HEAD_DIM_A

    def emit(rows, acc):
        b_local = rows.start // blk
        for hh in range(N_HEADS_A):
            vt_ref[hh, b_local, 0:dh, :] = acc[:, hh * dh:(hh + 1) * dh].T.astype(BF16)
            vt_ref[hh, b_local, dh:dh + SUM_ROWS, :] = jnp.ones((SUM_ROWS, blk), BF16)

    _sub_dots(h_ref, wbf_ref[...], emit)


def _qkv_proj(x, g_mix, w_in, gq_a, gk_a, slopes, layer):
    s = x.shape[0]
    tm, dh, nh = TM, HEAD_DIM_A, N_HEADS_A
    grid = (1, s // tm)
    h_spec = pl.BlockSpec((tm, D_MODEL), lambda n, m: (m, 0))
    gain = pl.BlockSpec((None, 1, dh), lambda n, m: (layer, 0, 0))
    slope = pl.BlockSpec((nh, 1, MOBA_BLOCK), lambda n, m: (0, 0, 0))
    scratch = [pltpu.VMEM((D_MODEL, TN), BF16)]
    params = _params("arbitrary", "arbitrary")

    def w_spec(col):
        return pl.BlockSpec((None, D_MODEL, TN), lambda n, m: (layer, 0, col))

    kaug, kmeans, h = pl.pallas_call(
        _k_kernel,
        out_shape=(jax.ShapeDtypeStruct((nh, s, 2 * dh), BF16),
                   jax.ShapeDtypeStruct((s // tm, tm // MOBA_BLOCK, WIDTH_A), F32),
                   jax.ShapeDtypeStruct((s, D_MODEL), BF16)),
        grid=grid,
        in_specs=[h_spec, pl.BlockSpec((None, 1, D_MODEL), lambda n, m: (layer, 0, 0)),
                  w_spec(1), gain, slope],
        out_specs=(pl.BlockSpec((nh, tm, 2 * dh), lambda n, m: (0, m, 0)),
                   pl.BlockSpec((None, tm // MOBA_BLOCK, WIDTH_A), lambda n, m: (m, 0, 0)),
                   h_spec),
        scratch_shapes=scratch, compiler_params=params, name="k_proj",
    )(x, g_mix, w_in, gk_a, slopes)
    kmean = kmeans.reshape(N_BLOCKS, WIDTH_A)
    qaug = pl.pallas_call(
        _q_kernel,
        out_shape=jax.ShapeDtypeStruct((nh, 2 * dh, s), BF16),
        grid=grid,
        in_specs=[h_spec, w_spec(0), gain,
                  pl.BlockSpec((N_BLOCKS, WIDTH_A), lambda n, m: (0, 0)), slope],
        out_specs=pl.BlockSpec((nh, 2 * dh, tm), lambda n, m: (0, 0, m)),
        scratch_shapes=scratch, compiler_params=params, name="q_proj",
    )(h, w_in, gq_a, kmean, slopes)
    vt = pl.pallas_call(
        _v_kernel,
        out_shape=jax.ShapeDtypeStruct((nh, N_BLOCKS, dh + SUM_ROWS, MOBA_BLOCK), BF16),
        grid=grid,
        in_specs=[h_spec, w_spec(2)],
        out_specs=pl.BlockSpec((nh, tm // MOBA_BLOCK, dh + SUM_ROWS, MOBA_BLOCK),
                               lambda n, m: (0, m, 0, 0)),
        scratch_shapes=scratch, compiler_params=params, name="v_proj",
    )(h, w_in)
    return h, qaug, kaug, vt


def _sgu_kernel(h_ref, wu_ref, wv_ref, g_ref, ws_ref, bt_ref, o_ref,
                wubf_ref, wvbf_ref, wsbf_ref):
    @pl.when(pl.program_id(0) == 0)
    def _():
        wubf_ref[...] = wu_ref[...].astype(BF16)
        wvbf_ref[...] = wv_ref[...].astype(BF16)
        row = lax.broadcasted_iota(jnp.int32, (SGU_CHUNK, SGU_CHUNK), 0)
        col = lax.broadcasted_iota(jnp.int32, (SGU_CHUNK, SGU_CHUNK), 1)
        for g in range(SGU_GROUPS):
            wsbf_ref[g] = jnp.where(col <= row, ws_ref[g], 0.0).astype(BF16)

    def proj(r):
        rows = slice(r, r + SUB_ROWS)
        return (rows,
                jnp.dot(h_ref[rows, :], wubf_ref[...], preferred_element_type=F32),
                jnp.dot(h_ref[rows, :], wvbf_ref[...], preferred_element_type=F32))

    def mix(rows, u_acc, v_acc):
        u = jax.nn.gelu(u_acc)
        v = jax.nn.gelu(v_acc)
        ms = jnp.mean(v * v, axis=-1, keepdims=True)
        vn = (v * lax.rsqrt(ms + EPS) * g_ref[...]).astype(BF16)
        chunks = [slice(c, c + SGU_CHUNK) for c in range(0, SUB_ROWS, SGU_CHUNK)]
        for g in range(SGU_GROUPS):
            cols = slice(g * SGU_CHUNK, (g + 1) * SGU_CHUNK)
            v_all = jnp.concatenate([vn[c, cols] for c in chunks], axis=1)
            mixed = jnp.dot(wsbf_ref[g], v_all, preferred_element_type=F32) + bt_ref[:, g:g + 1]
            for c in chunks:
                o_ref[rows.start + c.start:rows.start + c.stop, cols] = (
                    u[c, cols] * mixed[:, c]).astype(o_ref.dtype)

    n_rows = h_ref.shape[0]
    cur = proj(0)
    for r in range(0, n_rows, SUB_ROWS):
        nxt = proj(r + SUB_ROWS) if r + SUB_ROWS < n_rows else None
        mix(*cur)
        cur = nxt


def _sgu_branch(h, w_in, g_sgu, w_sgu, b_sgu_t, layer):
    s = h.shape[0]
    off = COL_UV // TN
    once = pl.Buffered(1)
    return pl.pallas_call(
        _sgu_kernel,
        out_shape=jax.ShapeDtypeStruct((s, SGU_WIDTH), BF16),
        grid=(s // TM,),
        in_specs=[pl.BlockSpec((TM, D_MODEL), lambda m: (m, 0)),
                  pl.BlockSpec((None, D_MODEL, TN), lambda m: (layer, 0, off),
                               pipeline_mode=once),
                  pl.BlockSpec((None, D_MODEL, TN), lambda m: (layer, 0, off + 1),
                               pipeline_mode=once),
                  pl.BlockSpec((None, 1, SGU_WIDTH), lambda m: (layer, 0, 0)),
                  pl.BlockSpec((None, SGU_GROUPS, SGU_CHUNK, SGU_CHUNK),
                               lambda m: (layer, 0, 0, 0)),
                  pl.BlockSpec((None, SGU_CHUNK, SGU_GROUPS), lambda m: (layer, 0, 0))],
        out_specs=pl.BlockSpec((TM, SGU_WIDTH), lambda m: (m, 0)),
        scratch_shapes=[pltpu.VMEM((D_MODEL, TN), BF16), pltpu.VMEM((D_MODEL, TN), BF16),
                        pltpu.VMEM((SGU_GROUPS, SGU_CHUNK, SGU_CHUNK), BF16)],
        compiler_params=_params("arbitrary"),
        name="sgu_branch",
    )(h, w_in, w_in, g_sgu, w_sgu, b_sgu_t)


def _memkv_kernel(mem_ref, gmem_ref, w_ref, gk_ref, o_ref):
    n = pl.program_id(0)
    x = mem_ref[...]
    ms = jnp.mean(x * x, axis=-1, keepdims=True)
    hm = (x * lax.rsqrt(ms + EPS) * gmem_ref[...]).astype(BF16)
    acc = jnp.dot(hm, w_ref[...].astype(BF16), preferred_element_type=F32)

    @pl.when(n == 0)
    def _():
        o_ref[...] = _group_rmsnorm(acc, gk_ref[...], HEAD_DIM_M).astype(o_ref.dtype)

    @pl.when(n == 1)
    def _():
        o_ref[...] = acc.astype(o_ref.dtype)


def _memkv_proj(mem, g_mem, w_mem_kv, gk_m, layer):
    return pl.pallas_call(
        _memkv_kernel,
        out_shape=jax.ShapeDtypeStruct((2, N_MEM, WIDTH_M), BF16),
        grid=(2,),
        in_specs=[pl.BlockSpec((N_MEM, D_MODEL), lambda n: (0, 0)),
                  pl.BlockSpec((None, 1, D_MODEL), lambda n: (layer, 0, 0)),
                  pl.BlockSpec((None, D_MODEL, WIDTH_M), lambda n: (layer, 0, n)),
                  pl.BlockSpec((None, 1, HEAD_DIM_M), lambda n: (layer, 0, 0))],
        out_specs=pl.BlockSpec((None, N_MEM, WIDTH_M), lambda n: (n, 0, 0)),
        compiler_params=_params("arbitrary"),
        name="memkv_proj",
    )(mem, g_mem, w_mem_kv, gk_m)


def _qm_kernel(h_ref, w_ref, gq_ref, km_ref, vm_ref, o_ref, wbf_ref):
    _cast_weight_once(w_ref, wbf_ref)

    def emit(rows, acc):
        heads = [slice(hh * HEAD_DIM_M, (hh + 1) * HEAD_DIM_M) for hh in range(N_HEADS_M)]
        scores = []
        for cols in heads:
            qh = acc[:, cols]
            ms = jnp.mean(qh * qh, axis=-1, keepdims=True)
            qn = (qh * lax.rsqrt(ms + EPS) * gq_ref[...]).astype(BF16)
            scores.append(lax.dot_general(qn, km_ref[:, cols], (((1,), (1,)), ((), ())),
                                          preferred_element_type=F32) * HEAD_DIM_M ** -0.5)
        weights = []
        for sc in scores:
            mx = jnp.max(sc, axis=-1, keepdims=True)
            e = jnp.exp(sc - mx)
            weights.append((e / jnp.sum(e, axis=-1, keepdims=True)).astype(BF16))
        for cols, p in zip(heads, weights):
            o_ref[rows, cols] = jnp.dot(
                p, vm_ref[:, cols], preferred_element_type=F32).astype(o_ref.dtype)

    _sub_dots(h_ref, wbf_ref[...], emit)


def _qm_memattn(h, w_in, gq_m, kvm, layer):
    s = h.shape[0]
    off = COL_QM // TN
    tm = TM_BIG
    return pl.pallas_call(
        _qm_kernel,
        out_shape=jax.ShapeDtypeStruct((s, WIDTH_M), BF16),
        grid=(1, s // tm),
        in_specs=[pl.BlockSpec((tm, D_MODEL), lambda n, m: (m, 0)),
                  pl.BlockSpec((None, D_MODEL, TN), lambda n, m: (layer, 0, off)),
                  pl.BlockSpec((None, 1, HEAD_DIM_M), lambda n, m: (layer, 0, 0)),
                  pl.BlockSpec((None, N_MEM, WIDTH_M), lambda n, m: (0, 0, 0)),
                  pl.BlockSpec((None, N_MEM, WIDTH_M), lambda n, m: (1, 0, 0))],
        out_specs=pl.BlockSpec((tm, WIDTH_M), lambda n, m: (m, 0)),
        scratch_shapes=[pltpu.VMEM((D_MODEL, TN), BF16)],
        compiler_params=_params("arbitrary", "arbitrary"),
        name="qm_memattn",
    )(h, w_in, gq_m, kvm, kvm)


def _gates_kernel(h_ref, w_ref, o_ref, wbf_ref):
    _cast_weight_once(w_ref, wbf_ref)
    def emit(rows, acc):
        o_ref[rows, :] = jax.nn.sigmoid(acc).astype(o_ref.dtype)

    _sub_dots(h_ref, wbf_ref[...], emit)


def _gates_proj(h, w_in, layer):
    s = h.shape[0]
    off = COL_GATES // TN
    n_tiles = 3 * D_MODEL // TN
    return pl.pallas_call(
        _gates_kernel,
        out_shape=jax.ShapeDtypeStruct((s, 3 * D_MODEL), BF16),
        grid=(n_tiles, s // TM_BIG),
        in_specs=[pl.BlockSpec((TM_BIG, D_MODEL), lambda n, m: (m, 0)),
                  pl.BlockSpec((None, D_MODEL, TN), lambda n, m: (layer, 0, off + n))],
        out_specs=pl.BlockSpec((TM_BIG, TN), lambda n, m: (m, n)),
        scratch_shapes=[pltpu.VMEM((D_MODEL, TN), BF16)],
        compiler_params=_params("arbitrary", "arbitrary"),
        name="gates_proj",
    )(h, w_in)


MOBA_HALF = 2
MOBA_HEADS = 2
M_INIT = -3e38


def _moba_kernel(qaug_ref, kaug_ref, vt_ref, o_ref, s_ref):
    i = pl.program_id(1)
    blk, dh = MOBA_BLOCK, HEAD_DIM_A
    heads = range(MOBA_HEADS)
    half = MOBA_HALF
    half_rows = half * blk
    key_minus_qry = (lax.broadcasted_iota(jnp.int32, (half_rows, blk), 0)
                     - lax.broadcasted_iota(jnp.int32, (half_rows, blk), 1))

    def score(hh, first_blk):
        rows = pl.ds(pl.multiple_of(first_blk * blk, half_rows), half_rows)
        return jnp.dot(kaug_ref[hh, rows, :], qaug_ref[hh], preferred_element_type=F32)

    def attend(hh, st, s, first_blk, causal):
        m, acc = st
        if causal:
            s = jnp.where(key_minus_qry <= (i - first_blk) * blk, s, NEG_INF)
        m_new = jnp.maximum(m, jnp.max(s, axis=0, keepdims=True))
        alpha = jnp.exp(m - m_new)
        p = jnp.exp(s - m_new)
        v_t = jnp.concatenate([vt_ref[hh, first_blk + g] for g in range(half)], axis=1)
        acc = alpha * acc + jnp.dot(v_t, p.astype(BF16), preferred_element_type=F32)
        return m_new, acc

    def trip(t, carry, last):
        first = t * (2 * half)
        if last:
            carry = tuple(attend(hh, carry[hh], s_ref[hh], first, True) for hh in heads)

            def second_half(c):
                s_second = [score(hh, first + half) for hh in heads]
                return tuple(attend(hh, c[hh], s_second[hh], first + half, True)
                             for hh in heads)

            return lax.cond(i >= first + half, second_half, lambda c: c, carry)
        s_second = [score(hh, first + half) for hh in heads]
        carry = [attend(hh, carry[hh], s_ref[hh], first, False) for hh in heads]
        for hh in heads:
            s_ref[hh] = score(hh, first + 2 * half)
        carry = [attend(hh, carry[hh], s_second[hh], first + half, False) for hh in heads]
        return tuple(carry)

    for hh in heads:
        s_ref[hh] = score(hh, 0)

    state = tuple((jnp.full((1, blk), M_INIT, F32), jnp.zeros((dh + SUM_ROWS, blk), F32))
                  for _ in heads)
    last_trip = i // (2 * half)
    state = lax.fori_loop(0, last_trip // 2,
                          lambda u, c: trip(2 * u + 1, trip(2 * u, c, False), False), state)
    state = lax.fori_loop(0, last_trip % 2, lambda _, c: trip(last_trip - 1, c, False), state)
    state = trip(last_trip, state, True)
    for hh in heads:
        _, acc = state[hh]
        out = acc[0:dh] / acc[dh:dh + 1]
        o_ref[:, hh * dh:(hh + 1) * dh] = out.T.astype(o_ref.dtype)


def _moba(qaug, kaug, vt):
    s = kaug.shape[1]
    nh = MOBA_HEADS
    return pl.pallas_call(
        _moba_kernel,
        out_shape=jax.ShapeDtypeStruct((s, WIDTH_A), BF16),
        grid=(N_HEADS_A // nh, s // MOBA_BLOCK),
        in_specs=[pl.BlockSpec((nh, 2 * HEAD_DIM_A, MOBA_BLOCK), lambda h, i: (h, 0, i)),
                  pl.BlockSpec((nh, s, 2 * HEAD_DIM_A), lambda h, i: (h, 0, 0)),
                  pl.BlockSpec((nh, N_BLOCKS, HEAD_DIM_A + SUM_ROWS, MOBA_BLOCK),
                               lambda h, i: (h, 0, 0, 0))],
        out_specs=pl.BlockSpec((MOBA_BLOCK, nh * HEAD_DIM_A), lambda h, i: (i, h)),
        scratch_shapes=[pltpu.VMEM((nh, MOBA_HALF * MOBA_BLOCK, MOBA_BLOCK), F32)],
        compiler_params=_params("arbitrary", "arbitrary"),
        name="moba_attention",
    )(qaug, kaug, vt)


MERGE_TN = 1024
MERGE_TM = 512


def _merge_kernel(ya_ref, yb_ref, ym_ref, ga_ref, gb_ref, gm_ref, w_ref, o_ref, wbf_ref):
    _cast_weight_once(w_ref, wbf_ref)
    for r in range(0, ya_ref.shape[0], SUB_ROWS):
        rows = slice(r, r + SUB_ROWS)
        merged = ga_ref[rows, :] * jnp.dot(ya_ref[rows, :], wbf_ref[0],
                                           preferred_element_type=F32)
        merged += gb_ref[rows, :] * jnp.dot(yb_ref[rows, :], wbf_ref[1],
                                            preferred_element_type=F32)
        merged += gm_ref[rows, :] * jnp.dot(ym_ref[rows, :], wbf_ref[2],
                                            preferred_element_type=F32)
        o_ref[rows, :] = merged.astype(o_ref.dtype)


def _merge(ya, yb, ym, gates, w_branch, layer):
    s = ya.shape[0]
    tn = MERGE_TN
    nt = D_MODEL // tn
    tm = MERGE_TM
    act = pl.BlockSpec((tm, WIDTH_A), lambda n, m: (m, 0))
    return pl.pallas_call(
        _merge_kernel,
        out_shape=jax.ShapeDtypeStruct((s, D_MODEL), BF16),
        grid=(nt, s // tm),
        in_specs=[act, act, act,
                  pl.BlockSpec((tm, tn), lambda n, m: (m, n)),
                  pl.BlockSpec((tm, tn), lambda n, m: (m, nt + n)),
                  pl.BlockSpec((tm, tn), lambda n, m: (m, 2 * nt + n)),
                  pl.BlockSpec((None, 3, WIDTH_A, tn), lambda n, m: (layer, 0, 0, n))],
        out_specs=pl.BlockSpec((tm, tn), lambda n, m: (m, n)),
        scratch_shapes=[pltpu.VMEM((3, WIDTH_A, tn), BF16)],
        compiler_params=_params("arbitrary", "arbitrary"),
        name="branch_merge",
    )(ya, yb, ym, gates, gates, gates, w_branch)


def _resid_proj_kernel(a_ref, w_ref, x_ref, o_ref, wbf_ref):
    _cast_weight_once(w_ref, wbf_ref)
    def emit(rows, acc):
        o_ref[rows, :] = x_ref[rows, :] + acc

    _sub_dots(a_ref, wbf_ref[...], emit)


def _resid_proj(a, w, x, layer, tm, tn, name):
    s, k = a.shape
    n_out = w.shape[2]
    return pl.pallas_call(
        _resid_proj_kernel,
        out_shape=jax.ShapeDtypeStruct((s, n_out), F32),
        grid=(n_out // tn, s // tm),
        in_specs=[pl.BlockSpec((tm, k), lambda n, m: (m, 0)),
                  pl.BlockSpec((None, k, tn), lambda n, m: (layer, 0, n)),
                  pl.BlockSpec((tm, tn), lambda n, m: (m, n))],
        out_specs=pl.BlockSpec((tm, tn), lambda n, m: (m, n)),
        scratch_shapes=[pltpu.VMEM((k, tn), BF16)],
        compiler_params=_params("arbitrary", "arbitrary"),
        name=name,
    )(a, w, x)


def _out_proj_norm_kernel(a_ref, w_ref, x_ref, g_ref, o_ref, h_ref, wbf_ref):
    @pl.when(pl.program_id(0) == 0)
    def _():
        wbf_ref[...] = w_ref[...].astype(BF16)

    def emit(rows, acc):
        x = x_ref[rows, :] + acc
        o_ref[rows, :] = x
        ms = jnp.mean(x * x, axis=-1, keepdims=True)
        h_ref[rows, :] = (x * lax.rsqrt(ms + EPS) * g_ref[...]).astype(h_ref.dtype)

    _sub_dots(a_ref, wbf_ref[...], emit)


def _out_proj_norm(a, w, x, g_all, layer):
    s, k = a.shape
    tm = 512
    return pl.pallas_call(
        _out_proj_norm_kernel,
        out_shape=(jax.ShapeDtypeStruct((s, D_MODEL), F32),
                   jax.ShapeDtypeStruct((s, D_MODEL), BF16)),
        grid=(s // tm,),
        in_specs=[pl.BlockSpec((tm, k), lambda m: (m, 0)),
                  pl.BlockSpec((None, k, D_MODEL), lambda m: (layer, 0, 0),
                               pipeline_mode=pl.Buffered(1)),
                  pl.BlockSpec((tm, D_MODEL), lambda m: (m, 0)),
                  pl.BlockSpec((None, 1, D_MODEL), lambda m: (layer, 0, 0))],
        out_specs=(pl.BlockSpec((tm, D_MODEL), lambda m: (m, 0)),
                   pl.BlockSpec((tm, D_MODEL), lambda m: (m, 0))),
        scratch_shapes=[pltpu.VMEM((k, D_MODEL), BF16)],
        compiler_params=_params("arbitrary"),
        name="out_proj_norm",
    )(a, w, x, g_all)


FFN_TN = 512


def _ffn_up_kernel(h_ref, wg_ref, wu_ref, o_ref, wgbf_ref, wubf_ref):
    @pl.when(pl.program_id(1) == 0)
    def _():
        wgbf_ref[...] = wg_ref[...].astype(BF16)
        wubf_ref[...] = wu_ref[...].astype(BF16)
    for r in range(0, h_ref.shape[0], SUB_ROWS):
        rows = slice(r, r + SUB_ROWS)
        gt = jnp.dot(h_ref[rows, :], wgbf_ref[...], preferred_element_type=F32)
        up = jnp.dot(h_ref[rows, :], wubf_ref[...], preferred_element_type=F32)
        o_ref[rows, :] = (jax.nn.silu(gt) * up).astype(o_ref.dtype)


def _ffn_up(h, w_gate_up, layer):
    s = h.shape[0]
    tn = FFN_TN
    nt = D_FF // tn
    tm = TM_BIG
    return pl.pallas_call(
        _ffn_up_kernel,
        out_shape=jax.ShapeDtypeStruct((s, D_FF), BF16),
        grid=(nt, s // tm),
        in_specs=[pl.BlockSpec((tm, D_MODEL), lambda n, m: (m, 0)),
                  pl.BlockSpec((None, D_MODEL, tn), lambda n, m: (layer, 0, n)),
                  pl.BlockSpec((None, D_MODEL, tn), lambda n, m: (layer, 0, nt + n))],
        out_specs=pl.BlockSpec((tm, tn), lambda n, m: (m, n)),
        scratch_shapes=[pltpu.VMEM((D_MODEL, tn), BF16), pltpu.VMEM((D_MODEL, tn), BF16)],
        compiler_params=_params("arbitrary", "arbitrary"),
        name="ffn_up",
    )(h, w_gate_up, w_gate_up)


def kernel(x, mem, g_mix, w_in, gq_a, gk_a, g_sgu, w_sgu, b_sgu, gq_m, gk_m, g_mem,
           w_mem_kv, w_branch, w_out, g_ffn, w_gate_up, w_down):
    b, s, d = x.shape
    assert (b, s, d) == (1, SEQ, D_MODEL) and mem.shape == (1, N_MEM, D_MODEL)
    x2 = x.reshape(s, d)
    mem2 = mem.reshape(N_MEM, d)

    def row(p):
        return p.reshape(DEPTH, 1, p.shape[-1])

    g_mix3, g_ffn3, g_mem3, g_sgu3 = row(g_mix), row(g_ffn), row(g_mem), row(g_sgu)
    gq_a3, gk_a3, gq_m3, gk_m3 = row(gq_a), row(gk_a), row(gq_m), row(gk_m)
    b_sgu_t = jnp.swapaxes(b_sgu, 1, 2)
    slopes = 2.0 ** (-8.0 * jnp.arange(1, N_HEADS_A + 1, dtype=F32) / N_HEADS_A)
    slopes = jnp.broadcast_to(slopes[:, None, None], (N_HEADS_A, 1, MOBA_BLOCK))

    for layer in range(DEPTH):
        h, qaug, kaug, vt = _qkv_proj(x2, g_mix3, w_in, gq_a3, gk_a3, slopes, layer)
        yb = _sgu_branch(h, w_in, g_sgu3, w_sgu, b_sgu_t, layer)
        kvm = _memkv_proj(mem2, g_mem3, w_mem_kv, gk_m3, layer)
        ym = _qm_memattn(h, w_in, gq_m3, kvm, layer)
        gates = _gates_proj(h, w_in, layer)
        ya = _moba(qaug, kaug, vt)
        merged = _merge(ya, yb, ym, gates, w_branch, layer)
        x2, hf = _out_proj_norm(merged, w_out, x2, g_ffn3, layer)
        act = _ffn_up(hf, w_gate_up, layer)
        x2 = _resid_proj(act, w_down, x2, layer, 512, 512, "ffn_down")
    return x2.reshape(b, s, d)
```

```python
import jax
import jax.numpy as jnp
from jax import lax
from jax.experimental import pallas as pl
from jax.experimental.pallas import tpu as pltpu

F32 = jnp.float32
BF16 = jnp.bfloat16

D_MODEL = 2048
SEQ = 8192
DEPTH = 2
N_HEADS_A = 8
HEAD_DIM_A = 128
WIDTH_A = N_HEADS_A * HEAD_DIM_A
MOBA_BLOCK = 256
MOBA_TOPK = 3
N_BLOCKS = SEQ // MOBA_BLOCK
SGU_WIDTH = 1024
SGU_GROUPS = 8
SGU_CHUNK = 128
N_MEM = 256
N_HEADS_M = 4
HEAD_DIM_M = 256
WIDTH_M = N_HEADS_M * HEAD_DIM_M
D_FF = 5632
NEG_INF = -1e30
EPS = 1e-6

COL_UV = 3 * WIDTH_A
COL_QM = COL_UV + 2 * SGU_WIDTH
COL_GATES = COL_QM + WIDTH_M

VMEM_LIMIT_BYTES = 60 * 1024 * 1024

TM = 1024
TM_BIG = 2048
TN = 1024
SUB_ROWS = 256


def _params(*semantics):
    return pltpu.CompilerParams(dimension_semantics=semantics,
                                vmem_limit_bytes=VMEM_LIMIT_BYTES)


def _cast_weight_once(w_ref, wbf_ref):
    @pl.when(pl.program_id(1) == 0)
    def _():
        wbf_ref[...] = w_ref[...].astype(BF16)


def _group_rmsnorm(a, g, width):
    outs = []
    for s in range(0, a.shape[1], width):
        blk = a[:, s:s + width]
        ms = jnp.mean(blk * blk, axis=-1, keepdims=True)
        outs.append(blk * lax.rsqrt(ms + EPS) * g)
    return jnp.concatenate(outs, axis=1)


def _sub_dots(lhs_ref, rhs, emit, lhs_rows=None):
    def sub_dot(r):
        rows = slice(r, r + SUB_ROWS)
        lhs = lhs_rows(rows) if lhs_rows is not None else lhs_ref[rows, :]
        return rows, jnp.dot(lhs, rhs, preferred_element_type=F32)

    n_rows = lhs_ref.shape[0]
    cur = sub_dot(0)
    for r in range(0, n_rows, SUB_ROWS):
        nxt = sub_dot(r + SUB_ROWS) if r + SUB_ROWS < n_rows else None
        emit(*cur)
        cur = nxt


COL_KEY_OFF = N_BLOCKS
COL_BLK_OFF = N_BLOCKS + 1
COL_ONES = N_BLOCKS + 2
SUM_ROWS = 16
GATE_TAKEN = -3e38


def _head_norm(a, g):
    ms = jnp.mean(a * a, axis=-1, keepdims=True)
    return a * lax.rsqrt(ms + EPS) * g


def _k_kernel(x_ref, g_ref, w_ref, gk_ref, slope_ref, kaug_ref, kmean_ref, h_ref, wbf_ref):
    m = pl.program_id(1)

    def norm_rows(rows):
        x = x_ref[rows, :]
        ms = jnp.mean(x * x, axis=-1, keepdims=True)
        h = (x * lax.rsqrt(ms + EPS) * g_ref[...]).astype(BF16)
        h_ref[rows, :] = h
        return h

    _cast_weight_once(w_ref, wbf_ref)
    assert SUB_ROWS == MOBA_BLOCK
    blk, dh = MOBA_BLOCK, HEAD_DIM_A
    lane = lax.broadcasted_iota(jnp.int32, (blk, dh), 1)
    lane_row = lax.broadcasted_iota(jnp.int32, (1, dh), 1)
    key_off = lax.broadcasted_iota(jnp.int32, (blk, dh), 0).astype(F32)

    def emit(rows, acc):
        b_local = rows.start // blk
        b = m * (h_ref.shape[0] // blk) + b_local
        blk_start = jnp.full((1, dh), b * blk, jnp.int32).astype(F32)
        onehot = jnp.where(lane == b, 1.0, jnp.where(lane == COL_ONES, 1.0, 0.0))
        for hh in range(N_HEADS_A):
            cols = slice(hh * dh, (hh + 1) * dh)
            kn = _head_norm(acc[:, cols], gk_ref[...])
            kmean_ref[b_local:b_local + 1, cols] = jnp.mean(kn, axis=0, keepdims=True)
            sl = slope_ref[hh][:, :dh]
            extra = (onehot + jnp.where(lane == COL_KEY_OFF, sl * key_off, 0.0)
                     + jnp.where(lane_row == COL_BLK_OFF, sl * blk_start, 0.0))
            kaug_ref[hh, rows, 0:dh] = kn.astype(BF16)
            kaug_ref[hh, rows, dh:2 * dh] = extra.astype(BF16)

    _sub_dots(x_ref, wbf_ref[...], emit, lhs_rows=norm_rows)


def _q_kernel(h_ref, w_ref, gq_ref, kmean_ref, slope_ref, qaug_ref, wbf_ref):
    m = pl.program_id(1)
    _cast_weight_once(w_ref, wbf_ref)
    assert SUB_ROWS == MOBA_BLOCK
    blk, dh = MOBA_BLOCK, HEAD_DIM_A
    blk_id = lax.broadcasted_iota(jnp.int32, (N_BLOCKS, blk), 0)
    blk_id_f = blk_id.astype(F32)
    rest = lax.broadcasted_iota(jnp.int32, (dh - N_BLOCKS, blk), 0) + N_BLOCKS

    def emit(rows, acc):
        i = m * (h_ref.shape[0] // blk) + rows.start // blk
        past = blk_id < i
        q_start = jnp.full((1, blk), i * blk, jnp.int32).astype(F32)
        for hh in range(N_HEADS_A):
            cols = slice(hh * dh, (hh + 1) * dh)
            q = _head_norm(acc[:, cols], gq_ref[...]) * dh ** -0.5
            q_t = q.T
            gate = jnp.dot(kmean_ref[:, cols].astype(BF16), q_t.astype(BF16),
                           preferred_element_type=F32)
            gate = jnp.where(past, gate, NEG_INF)
            pen = jnp.full((N_BLOCKS, blk), NEG_INF, F32)
            for _ in range(MOBA_TOPK):
                mx = jnp.max(gate, axis=0, keepdims=True)
                first = jnp.min(jnp.where(gate == mx, blk_id_f, float(N_BLOCKS)),
                                axis=0, keepdims=True)
                hit = blk_id_f == first
                pen = jnp.where(jnp.logical_and(hit, mx > 0.5 * NEG_INF), 0.0, pen)
                gate = jnp.where(hit, GATE_TAKEN, gate)
            pen = jnp.where(blk_id == i, 0.0, pen)
            extra_q = jnp.where(rest == COL_ONES, -slope_ref[hh] * q_start,
                                jnp.where(rest < COL_ONES, 1.0, 0.0))
            qaug_ref[hh, :, rows] = jnp.concatenate([q_t, pen, extra_q], axis=0).astype(BF16)

    _sub_dots(h_ref, wbf_ref[...], emit)


def _v_kernel(h_ref, w_ref, vt_ref, wbf_ref):
    _cast_weight_once(w_ref, wbf_ref)
    assert SUB_ROWS == MOBA_BLOCK
    blk, dh = MOBA_BLOCK, HEAD_DIM_A

    def emit(rows, acc):
        b_local = rows.start // blk
        for hh in range(N_HEADS_A):
            vt_ref[hh, b_local, 0:dh, :] = acc[:, hh * dh:(hh + 1) * dh].T.astype(BF16)
            vt_ref[hh, b_local, dh:dh + SUM_ROWS, :] = jnp.ones((SUM_ROWS, blk), BF16)

    _sub_dots(h_ref, wbf_ref[...], emit)


def _qkv_proj(x, g_mix, w_in, gq_a, gk_a, slopes, layer):
    s = x.shape[0]
    tm, dh, nh = TM, HEAD_DIM_A, N_HEADS_A
    grid = (1, s // tm)
    h_spec = pl.BlockSpec((tm, D_MODEL), lambda n, m: (m, 0))
    gain = pl.BlockSpec((None, 1, dh), lambda n, m: (layer, 0, 0))
    slope = pl.BlockSpec((nh, 1, MOBA_BLOCK), lambda n, m: (0, 0, 0))
    scratch = [pltpu.VMEM((D_MODEL, TN), BF16)]
    params = _params("arbitrary", "arbitrary")

    def w_spec(col):
        return pl.BlockSpec((None, D_MODEL, TN), lambda n, m: (layer, 0, col))

    kaug, kmeans, h = pl.pallas_call(
        _k_kernel,
        out_shape=(jax.ShapeDtypeStruct((nh, s, 2 * dh), BF16),
                   jax.ShapeDtypeStruct((s // tm, tm // MOBA_BLOCK, WIDTH_A), F32),
                   jax.ShapeDtypeStruct((s, D_MODEL), BF16)),
        grid=grid,
        in_specs=[h_spec, pl.BlockSpec((None, 1, D_MODEL), lambda n, m: (layer, 0, 0)),
                  w_spec(1), gain, slope],
        out_specs=(pl.BlockSpec((nh, tm, 2 * dh), lambda n, m: (0, m, 0)),
                   pl.BlockSpec((None, tm // MOBA_BLOCK, WIDTH_A), lambda n, m: (m, 0, 0)),
                   h_spec),
        scratch_shapes=scratch, compiler_params=params, name="k_proj",
    )(x, g_mix, w_in, gk_a, slopes)
    kmean = kmeans.reshape(N_BLOCKS, WIDTH_A)
    qaug = pl.pallas_call(
        _q_kernel,
        out_shape=jax.ShapeDtypeStruct((nh, 2 * dh, s), BF16),
        grid=grid,
        in_specs=[h_spec, w_spec(0), gain,
                  pl.BlockSpec((N_BLOCKS, WIDTH_A), lambda n, m: (0, 0)), slope],
        out_specs=pl.BlockSpec((nh, 2 * dh, tm), lambda n, m: (0, 0, m)),
        scratch_shapes=scratch, compiler_params=params, name="q_proj",
    )(h, w_in, gq_a, kmean, slopes)
    vt = pl.pallas_call(
        _v_kernel,
        out_shape=jax.ShapeDtypeStruct((nh, N_BLOCKS, dh + SUM_ROWS, MOBA_BLOCK), BF16),
        grid=grid,
        in_specs=[h_spec, w_spec(2)],
        out_specs=pl.BlockSpec((nh, tm // MOBA_BLOCK, dh + SUM_ROWS, MOBA_BLOCK),
                               lambda n, m: (0, m, 0, 0)),
        scratch_shapes=scratch, compiler_params=params, name="v_proj",
    )(h, w_in)
    return h, qaug, kaug, vt


def _sgu_kernel(h_ref, wu_ref, wv_ref, g_ref, ws_ref, bt_ref, o_ref,
                wubf_ref, wvbf_ref, wsbf_ref):
    @pl.when(pl.program_id(0) == 0)
    def _():
        wubf_ref[...] = wu_ref[...].astype(BF16)
        wvbf_ref[...] = wv_ref[...].astype(BF16)
        row = lax.broadcasted_iota(jnp.int32, (SGU_CHUNK, SGU_CHUNK), 0)
        col = lax.broadcasted_iota(jnp.int32, (SGU_CHUNK, SGU_CHUNK), 1)
        for g in range(SGU_GROUPS):
            wsbf_ref[g] = jnp.where(col <= row, ws_ref[g], 0.0).astype(BF16)

    def proj(r):
        rows = slice(r, r + SUB_ROWS)
        return (rows,
                jnp.dot(h_ref[rows, :], wubf_ref[...], preferred_element_type=F32),
                jnp.dot(h_ref[rows, :], wvbf_ref[...], preferred_element_type=F32))

    def mix(rows, u_acc, v_acc):
        u = jax.nn.gelu(u_acc)
        v = jax.nn.gelu(v_acc)
        ms = jnp.mean(v * v, axis=-1, keepdims=True)
        vn = (v * lax.rsqrt(ms + EPS) * g_ref[...]).astype(BF16)
        chunks = [slice(c, c + SGU_CHUNK) for c in range(0, SUB_ROWS, SGU_CHUNK)]
        for g in range(SGU_GROUPS):
            cols = slice(g * SGU_CHUNK, (g + 1) * SGU_CHUNK)
            v_all = jnp.concatenate([vn[c, cols] for c in chunks], axis=1)
            mixed = jnp.dot(wsbf_ref[g], v_all, preferred_element_type=F32) + bt_ref[:, g:g + 1]
            for c in chunks:
                o_ref[rows.start + c.start:rows.start + c.stop, cols] = (
                    u[c, cols] * mixed[:, c]).astype(o_ref.dtype)

    n_rows = h_ref.shape[0]
    cur = proj(0)
    for r in range(0, n_rows, SUB_ROWS):
        nxt = proj(r + SUB_ROWS) if r + SUB_ROWS < n_rows else None
        mix(*cur)
        cur = nxt


def _sgu_branch(h, w_in, g_sgu, w_sgu, b_sgu_t, layer):
    s = h.shape[0]
    off = COL_UV // TN
    once = pl.Buffered(1)
    return pl.pallas_call(
        _sgu_kernel,
        out_shape=jax.ShapeDtypeStruct((s, SGU_WIDTH), BF16),
        grid=(s // TM,),
        in_specs=[pl.BlockSpec((TM, D_MODEL), lambda m: (m, 0)),
                  pl.BlockSpec((None, D_MODEL, TN), lambda m: (layer, 0, off),
                               pipeline_mode=once),
                  pl.BlockSpec((None, D_MODEL, TN), lambda m: (layer, 0, off + 1),
                               pipeline_mode=once),
                  pl.BlockSpec((None, 1, SGU_WIDTH), lambda m: (layer, 0, 0)),
                  pl.BlockSpec((None, SGU_GROUPS, SGU_CHUNK, SGU_CHUNK),
                               lambda m: (layer, 0, 0, 0)),
                  pl.BlockSpec((None, SGU_CHUNK, SGU_GROUPS), lambda m: (layer, 0, 0))],
        out_specs=pl.BlockSpec((TM, SGU_WIDTH), lambda m: (m, 0)),
        scratch_shapes=[pltpu.VMEM((D_MODEL, TN), BF16), pltpu.VMEM((D_MODEL, TN), BF16),
                        pltpu.VMEM((SGU_GROUPS, SGU_CHUNK, SGU_CHUNK), BF16)],
        compiler_params=_params("arbitrary"),
        name="sgu_branch",
    )(h, w_in, w_in, g_sgu, w_sgu, b_sgu_t)


def _memkv_kernel(mem_ref, gmem_ref, w_ref, gk_ref, o_ref):
    n = pl.program_id(0)
    x = mem_ref[...]
    ms = jnp.mean(x * x, axis=-1, keepdims=True)
    hm = (x * lax.rsqrt(ms + EPS) * gmem_ref[...]).astype(BF16)
    acc = jnp.dot(hm, w_ref[...].astype(BF16), preferred_element_type=F32)

    @pl.when(n == 0)
    def _():
        o_ref[...] = _group_rmsnorm(acc, gk_ref[...], HEAD_DIM_M).astype(o_ref.dtype)

    @pl.when(n == 1)
    def _():
        o_ref[...] = acc.astype(o_ref.dtype)


def _memkv_proj(mem, g_mem, w_mem_kv, gk_m, layer):
    return pl.pallas_call(
        _memkv_kernel,
        out_shape=jax.ShapeDtypeStruct((2, N_MEM, WIDTH_M), BF16),
        grid=(2,),
        in_specs=[pl.BlockSpec((N_MEM, D_MODEL), lambda n: (0, 0)),
                  pl.BlockSpec((None, 1, D_MODEL), lambda n: (layer, 0, 0)),
                  pl.BlockSpec((None, D_MODEL, WIDTH_M), lambda n: (layer, 0, n)),
                  pl.BlockSpec((None, 1, HEAD_DIM_M), lambda n: (layer, 0, 0))],
        out_specs=pl.BlockSpec((None, N_MEM, WIDTH_M), lambda n: (n, 0, 0)),
        compiler_params=_params("arbitrary"),
        name="memkv_proj",
    )(mem, g_mem, w_mem_kv, gk_m)


def _qm_kernel(h_ref, w_ref, gq_ref, km_ref, vm_ref, o_ref, wbf_ref):
    _cast_weight_once(w_ref, wbf_ref)

    def emit(rows, acc):
        heads = [slice(hh * HEAD_DIM_M, (hh + 1) * HEAD_DIM_M) for hh in range(N_HEADS_M)]
        scores = []
        for cols in heads:
            qh = acc[:, cols]
            ms = jnp.mean(qh * qh, axis=-1, keepdims=True)
            qn = (qh * lax.rsqrt(ms + EPS) * gq_ref[...]).astype(BF16)
            scores.append(lax.dot_general(qn, km_ref[:, cols], (((1,), (1,)), ((), ())),
                                          preferred_element_type=F32) * HEAD_DIM_M ** -0.5)
        weights = []
        for sc in scores:
            mx = jnp.max(sc, axis=-1, keepdims=True)
            e = jnp.exp(sc - mx)
            weights.append((e / jnp.sum(e, axis=-1, keepdims=True)).astype(BF16))
        for cols, p in zip(heads, weights):
            o_ref[rows, cols] = jnp.dot(
                p, vm_ref[:, cols], preferred_element_type=F32).astype(o_ref.dtype)

    _sub_dots(h_ref, wbf_ref[...], emit)


def _qm_memattn(h, w_in, gq_m, kvm, layer):
    s = h.shape[0]
    off = COL_QM // TN
    tm = TM_BIG
    return pl.pallas_call(
        _qm_kernel,
        out_shape=jax.ShapeDtypeStruct((s, WIDTH_M), BF16),
        grid=(1, s // tm),
        in_specs=[pl.BlockSpec((tm, D_MODEL), lambda n, m: (m, 0)),
                  pl.BlockSpec((None, D_MODEL, TN), lambda n, m: (layer, 0, off)),
                  pl.BlockSpec((None, 1, HEAD_DIM_M), lambda n, m: (layer, 0, 0)),
                  pl.BlockSpec((None, N_MEM, WIDTH_M), lambda n, m: (0, 0, 0)),
                  pl.BlockSpec((None, N_MEM, WIDTH_M), lambda n, m: (1, 0, 0))],
        out_specs=pl.BlockSpec((tm, WIDTH_M), lambda n, m: (m, 0)),
        scratch_shapes=[pltpu.VMEM((D_MODEL, TN), BF16)],
        compiler_params=_params("arbitrary", "arbitrary"),
        name="qm_memattn",
    )(h, w_in, gq_m, kvm, kvm)


def _gates_kernel(h_ref, w_ref, o_ref, wbf_ref):
    _cast_weight_once(w_ref, wbf_ref)
    def emit(rows, acc):
        o_ref[rows, :] = jax.nn.sigmoid(acc).astype(o_ref.dtype)

    _sub_dots(h_ref, wbf_ref[...], emit)


def _gates_proj(h, w_in, layer):
    s = h.shape[0]
    off = COL_GATES // TN
    n_tiles = 3 * D_MODEL // TN
    return pl.pallas_call(
        _gates_kernel,
        out_shape=jax.ShapeDtypeStruct((s, 3 * D_MODEL), BF16),
        grid=(n_tiles, s // TM_BIG),
        in_specs=[pl.BlockSpec((TM_BIG, D_MODEL), lambda n, m: (m, 0)),
                  pl.BlockSpec((None, D_MODEL, TN), lambda n, m: (layer, 0, off + n))],
        out_specs=pl.BlockSpec((TM_BIG, TN), lambda n, m: (m, n)),
        scratch_shapes=[pltpu.VMEM((D_MODEL, TN), BF16)],
        compiler_params=_params("arbitrary", "arbitrary"),
        name="gates_proj",
    )(h, w_in)


MOBA_HALF = 2
MOBA_HEADS = 2
M_INIT = -3e38


def _moba_kernel(qaug_ref, kaug_ref, vt_ref, o_ref, s_ref, s2_ref):
    i = pl.program_id(1)
    blk, dh = MOBA_BLOCK, HEAD_DIM_A
    heads = range(MOBA_HEADS)
    half = MOBA_HALF
    half_rows = half * blk
    key_minus_qry = (lax.broadcasted_iota(jnp.int32, (half_rows, blk), 0)
                     - lax.broadcasted_iota(jnp.int32, (half_rows, blk), 1))

    def score(hh, first_blk):
        rows = pl.ds(pl.multiple_of(first_blk * blk, half_rows), half_rows)
        return jnp.dot(kaug_ref[hh, rows, :], qaug_ref[hh], preferred_element_type=F32)

    def attend(hh, st, s, first_blk, causal):
        m, acc = st
        if causal:
            s = jnp.where(key_minus_qry <= (i - first_blk) * blk, s, NEG_INF)
        m_new = jnp.maximum(m, jnp.max(s, axis=0, keepdims=True))
        alpha = jnp.exp(m - m_new)
        p = jnp.exp(s - m_new)
        v_t = jnp.concatenate([vt_ref[hh, first_blk + g] for g in range(half)], axis=1)
        acc = alpha * acc + jnp.dot(v_t, p.astype(BF16), preferred_element_type=F32)
        return m_new, acc

    def trip(t, carry, last):
        first = t * (2 * half)
        if last:
            for hh in heads:
                s2_ref[hh] = score(hh, first + half)
            carry = tuple(attend(hh, carry[hh], s_ref[hh], first, True) for hh in heads)

            def second_half(c):
                return tuple(attend(hh, c[hh], s2_ref[hh], first + half, True) for hh in heads)

            return lax.cond(i >= first + half, second_half, lambda c: c, carry)
        s_second = [score(hh, first + half) for hh in heads]
        carry = [attend(hh, carry[hh], s_ref[hh], first, False) for hh in heads]
        for hh in heads:
            s_ref[hh] = score(hh, first + 2 * half)
        carry = [attend(hh, carry[hh], s_second[hh], first + half, False) for hh in heads]
        return tuple(carry)

    for hh in heads:
        s_ref[hh] = score(hh, 0)

    state = tuple((jnp.full((1, blk), M_INIT, F32), jnp.zeros((dh + SUM_ROWS, blk), F32))
                  for _ in heads)
    last_trip = i // (2 * half)
    state = lax.fori_loop(0, last_trip // 2,
                          lambda u, c: trip(2 * u + 1, trip(2 * u, c, False), False), state)
    state = lax.fori_loop(0, last_trip % 2, lambda _, c: trip(last_trip - 1, c, False), state)
    state = trip(last_trip, state, True)
    for hh in heads:
        _, acc = state[hh]
        out = acc[0:dh] / acc[dh:dh + 1]
        o_ref[:, hh * dh:(hh + 1) * dh] = out.T.astype(o_ref.dtype)


def _moba(qaug, kaug, vt):
    s = kaug.shape[1]
    nh = MOBA_HEADS
    return pl.pallas_call(
        _moba_kernel,
        out_shape=jax.ShapeDtypeStruct((s, WIDTH_A), BF16),
        grid=(N_HEADS_A // nh, s // MOBA_BLOCK),
        in_specs=[pl.BlockSpec((nh, 2 * HEAD_DIM_A, MOBA_BLOCK), lambda h, i: (h, 0, i)),
                  pl.BlockSpec((nh, s, 2 * HEAD_DIM_A), lambda h, i: (h, 0, 0)),
                  pl.BlockSpec((nh, N_BLOCKS, HEAD_DIM_A + SUM_ROWS, MOBA_BLOCK),
                               lambda h, i: (h, 0, 0, 0))],
        out_specs=pl.BlockSpec((MOBA_BLOCK, nh * HEAD_DIM_A), lambda h, i: (i, h)),
        scratch_shapes=[pltpu.VMEM((nh, MOBA_HALF * MOBA_BLOCK, MOBA_BLOCK), F32),
                        pltpu.VMEM((nh, MOBA_HALF * MOBA_BLOCK, MOBA_BLOCK), F32)],
        compiler_params=_params("arbitrary", "arbitrary"),
        name="moba_attention",
    )(qaug, kaug, vt)


MERGE_TN = 1024
MERGE_TM = 512


def _merge_kernel(ya_ref, yb_ref, ym_ref, ga_ref, gb_ref, gm_ref, w_ref, o_ref, wbf_ref):
    _cast_weight_once(w_ref, wbf_ref)
    for r in range(0, ya_ref.shape[0], SUB_ROWS):
        rows = slice(r, r + SUB_ROWS)
        merged = ga_ref[rows, :] * jnp.dot(ya_ref[rows, :], wbf_ref[0],
                                           preferred_element_type=F32)
        merged += gb_ref[rows, :] * jnp.dot(yb_ref[rows, :], wbf_ref[1],
                                            preferred_element_type=F32)
        merged += gm_ref[rows, :] * jnp.dot(ym_ref[rows, :], wbf_ref[2],
                                            preferred_element_type=F32)
        o_ref[rows, :] = merged.astype(o_ref.dtype)


def _merge(ya, yb, ym, gates, w_branch, layer):
    s = ya.shape[0]
    tn = MERGE_TN
    nt = D_MODEL // tn
    tm = MERGE_TM
    act = pl.BlockSpec((tm, WIDTH_A), lambda n, m: (m, 0))
    return pl.pallas_call(
        _merge_kernel,
        out_shape=jax.ShapeDtypeStruct((s, D_MODEL), BF16),
        grid=(nt, s // tm),
        in_specs=[act, act, act,
                  pl.BlockSpec((tm, tn), lambda n, m: (m, n)),
                  pl.BlockSpec((tm, tn), lambda n, m: (m, nt + n)),
                  pl.BlockSpec((tm, tn), lambda n, m: (m, 2 * nt + n)),
                  pl.BlockSpec((None, 3, WIDTH_A, tn), lambda n, m: (layer, 0, 0, n))],
        out_specs=pl.BlockSpec((tm, tn), lambda n, m: (m, n)),
        scratch_shapes=[pltpu.VMEM((3, WIDTH_A, tn), BF16)],
        compiler_params=_params("arbitrary", "arbitrary"),
        name="branch_merge",
    )(ya, yb, ym, gates, gates, gates, w_branch)


def _resid_proj_kernel(a_ref, w_ref, x_ref, o_ref, wbf_ref):
    _cast_weight_once(w_ref, wbf_ref)
    def emit(rows, acc):
        o_ref[rows, :] = x_ref[rows, :] + acc

    _sub_dots(a_ref, wbf_ref[...], emit)


def _resid_proj(a, w, x, layer, tm, tn, name):
    s, k = a.shape
    n_out = w.shape[2]
    return pl.pallas_call(
        _resid_proj_kernel,
        out_shape=jax.ShapeDtypeStruct((s, n_out), F32),
        grid=(n_out // tn, s // tm),
        in_specs=[pl.BlockSpec((tm, k), lambda n, m: (m, 0)),
                  pl.BlockSpec((None, k, tn), lambda n, m: (layer, 0, n)),
                  pl.BlockSpec((tm, tn), lambda n, m: (m, n))],
        out_specs=pl.BlockSpec((tm, tn), lambda n, m: (m, n)),
        scratch_shapes=[pltpu.VMEM((k, tn), BF16)],
        compiler_params=_params("arbitrary", "arbitrary"),
        name=name,
    )(a, w, x)


OUT_TM = 512


def _out_proj_norm_kernel(a_ref, w_ref, x_ref, g_ref, o_ref, h_ref, wbf_ref):
    @pl.when(pl.program_id(0) == 0)
    def _():
        wbf_ref[...] = w_ref[...].astype(BF16)

    def emit(rows, acc):
        x = x_ref[rows, :] + acc
        o_ref[rows, :] = x
        ms = jnp.mean(x * x, axis=-1, keepdims=True)
        h_ref[rows, :] = (x * lax.rsqrt(ms + EPS) * g_ref[...]).astype(h_ref.dtype)

    _sub_dots(a_ref, wbf_ref[...], emit)


def _out_proj_norm(a, w, x, g_all, layer):
    s, k = a.shape
    tm = OUT_TM
    return pl.pallas_call(
        _out_proj_norm_kernel,
        out_shape=(jax.ShapeDtypeStruct((s, D_MODEL), F32),
                   jax.ShapeDtypeStruct((s, D_MODEL), BF16)),
        grid=(s // tm,),
        in_specs=[pl.BlockSpec((tm, k), lambda m: (m, 0)),
                  pl.BlockSpec((None, k, D_MODEL), lambda m: (layer, 0, 0),
                               pipeline_mode=pl.Buffered(1)),
                  pl.BlockSpec((tm, D_MODEL), lambda m: (m, 0)),
                  pl.BlockSpec((None, 1, D_MODEL), lambda m: (layer, 0, 0))],
        out_specs=(pl.BlockSpec((tm, D_MODEL), lambda m: (m, 0)),
                   pl.BlockSpec((tm, D_MODEL), lambda m: (m, 0))),
        scratch_shapes=[pltpu.VMEM((k, D_MODEL), BF16)],
        compiler_params=_params("arbitrary"),
        name="out_proj_norm",
    )(a, w, x, g_all)


FFN_TN = 512
FFN_DOWN_TM = 512
FFN_DOWN_TN = 512


def _ffn_up_kernel(h_ref, wg_ref, wu_ref, o_ref, wgbf_ref, wubf_ref):
    @pl.when(pl.program_id(1) == 0)
    def _():
        wgbf_ref[...] = wg_ref[...].astype(BF16)
        wubf_ref[...] = wu_ref[...].astype(BF16)
    for r in range(0, h_ref.shape[0], SUB_ROWS):
        rows = slice(r, r + SUB_ROWS)
        gt = jnp.dot(h_ref[rows, :], wgbf_ref[...], preferred_element_type=F32)
        up = jnp.dot(h_ref[rows, :], wubf_ref[...], preferred_element_type=F32)
        o_ref[rows, :] = (jax.nn.silu(gt) * up).astype(o_ref.dtype)


def _ffn_up(h, w_gate_up, layer):
    s = h.shape[0]
    tn = FFN_TN
    nt = D_FF // tn
    tm = TM_BIG
    return pl.pallas_call(
        _ffn_up_kernel,
        out_shape=jax.ShapeDtypeStruct((s, D_FF), BF16),
        grid=(nt, s // tm),
        in_specs=[pl.BlockSpec((tm, D_MODEL), lambda n, m: (m, 0)),
                  pl.BlockSpec((None, D_MODEL, tn), lambda n, m: (layer, 0, n)),
                  pl.BlockSpec((None, D_MODEL, tn), lambda n, m: (layer, 0, nt + n))],
        out_specs=pl.BlockSpec((tm, tn), lambda n, m: (m, n)),
        scratch_shapes=[pltpu.VMEM((D_MODEL, tn), BF16), pltpu.VMEM((D_MODEL, tn), BF16)],
        compiler_params=_params("arbitrary", "arbitrary"),
        name="ffn_up",
    )(h, w_gate_up, w_gate_up)


def kernel(x, mem, g_mix, w_in, gq_a, gk_a, g_sgu, w_sgu, b_sgu, gq_m, gk_m, g_mem,
           w_mem_kv, w_branch, w_out, g_ffn, w_gate_up, w_down):
    b, s, d = x.shape
    assert (b, s, d) == (1, SEQ, D_MODEL) and mem.shape == (1, N_MEM, D_MODEL)
    x2 = x.reshape(s, d)
    mem2 = mem.reshape(N_MEM, d)

    def row(p):
        return p.reshape(DEPTH, 1, p.shape[-1])

    g_mix3, g_ffn3, g_mem3, g_sgu3 = row(g_mix), row(g_ffn), row(g_mem), row(g_sgu)
    gq_a3, gk_a3, gq_m3, gk_m3 = row(gq_a), row(gk_a), row(gq_m), row(gk_m)
    b_sgu_t = jnp.swapaxes(b_sgu, 1, 2)
    slopes = 2.0 ** (-8.0 * jnp.arange(1, N_HEADS_A + 1, dtype=F32) / N_HEADS_A)
    slopes = jnp.broadcast_to(slopes[:, None, None], (N_HEADS_A, 1, MOBA_BLOCK))

    for layer in range(DEPTH):
        h, qaug, kaug, vt = _qkv_proj(x2, g_mix3, w_in, gq_a3, gk_a3, slopes, layer)
        yb = _sgu_branch(h, w_in, g_sgu3, w_sgu, b_sgu_t, layer)
        kvm = _memkv_proj(mem2, g_mem3, w_mem_kv, gk_m3, layer)
        ym = _qm_memattn(h, w_in, gq_m3, kvm, layer)
        gates = _gates_proj(h, w_in, layer)
        ya = _moba(qaug, kaug, vt)
        merged = _merge(ya, yb, ym, gates, w_branch, layer)
        x2, hf = _out_proj_norm(merged, w_out, x2, g_ffn3, layer)
        act = _ffn_up(hf, w_gate_up, layer)
        x2 = _resid_proj(act, w_down, x2, layer, FFN_DOWN_TM, FFN_DOWN_TN, "ffn_down")
    return x2.reshape(b, s, d)
```

```python
import jax
import jax.numpy as jnp
from jax import lax
from jax.experimental import pallas as pl
from jax.experimental.pallas import tpu as pltpu

F32 = jnp.float32
BF16 = jnp.bfloat16

D_MODEL = 2048
SEQ = 8192
DEPTH = 2
N_HEADS_A = 8
HEAD_DIM_A = 128
WIDTH_A = N_HEADS_A * HEAD_DIM_A
MOBA_BLOCK = 256
MOBA_TOPK = 3
N_BLOCKS = SEQ // MOBA_BLOCK
SGU_WIDTH = 1024
SGU_GROUPS = 8
SGU_CHUNK = 128
N_MEM = 256
N_HEADS_M = 4
HEAD_DIM_M = 256
WIDTH_M = N_HEADS_M * HEAD_DIM_M
D_FF = 5632
NEG_INF = -1e30
EPS = 1e-6

COL_UV = 3 * WIDTH_A
COL_QM = COL_UV + 2 * SGU_WIDTH
COL_GATES = COL_QM + WIDTH_M

VMEM_LIMIT_BYTES = 60 * 1024 * 1024

TM = 1024
TM_BIG = 2048
TN = 1024
SUB_ROWS = 256


def _params(*semantics):
    return pltpu.CompilerParams(dimension_semantics=semantics,
                                vmem_limit_bytes=VMEM_LIMIT_BYTES)


def _cast_weight_once(w_ref, wbf_ref):
    @pl.when(pl.program_id(1) == 0)
    def _():
        wbf_ref[...] = w_ref[...].astype(BF16)


def _group_rmsnorm(a, g, width):
    outs = []
    for s in range(0, a.shape[1], width):
        blk = a[:, s:s + width]
        ms = jnp.mean(blk * blk, axis=-1, keepdims=True)
        outs.append(blk * lax.rsqrt(ms + EPS) * g)
    return jnp.concatenate(outs, axis=1)


def _sub_dots(lhs_ref, rhs, emit, lhs_rows=None):
    def sub_dot(r):
        rows = slice(r, r + SUB_ROWS)
        lhs = lhs_rows(rows) if lhs_rows is not None else lhs_ref[rows, :]
        return rows, jnp.dot(lhs, rhs, preferred_element_type=F32)

    n_rows = lhs_ref.shape[0]
    cur = sub_dot(0)
    for r in range(0, n_rows, SUB_ROWS):
        nxt = sub_dot(r + SUB_ROWS) if r + SUB_ROWS < n_rows else None
        emit(*cur)
        cur = nxt


COL_KEY_OFF = N_BLOCKS
COL_BLK_OFF = N_BLOCKS + 1
COL_ONES = N_BLOCKS + 2
SUM_ROWS = 16
GATE_TAKEN = -3e38


def _head_norm(a, g):
    ms = jnp.mean(a * a, axis=-1, keepdims=True)
    return a * lax.rsqrt(ms + EPS) * g


def _k_kernel(x_ref, g_ref, w_ref, gk_ref, slope_ref, kaug_ref, kmean_ref, h_ref, wbf_ref):
    m = pl.program_id(1)

    def norm_rows(rows):
        x = x_ref[rows, :]
        ms = jnp.mean(x * x, axis=-1, keepdims=True)
        h = (x * lax.rsqrt(ms + EPS) * g_ref[...]).astype(BF16)
        h_ref[rows, :] = h
        return h

    _cast_weight_once(w_ref, wbf_ref)
    assert SUB_ROWS == MOBA_BLOCK
    blk, dh = MOBA_BLOCK, HEAD_DIM_A
    lane = lax.broadcasted_iota(jnp.int32, (blk, dh), 1)
    lane_row = lax.broadcasted_iota(jnp.int32, (1, dh), 1)
    key_off = lax.broadcasted_iota(jnp.int32, (blk, dh), 0).astype(F32)

    def emit(rows, acc):
        b_local = rows.start // blk
        b = m * (h_ref.shape[0] // blk) + b_local
        blk_start = jnp.full((1, dh), b * blk, jnp.int32).astype(F32)
        onehot = jnp.where(lane == b, 1.0, jnp.where(lane == COL_ONES, 1.0, 0.0))
        for hh in range(N_HEADS_A):
            cols = slice(hh * dh, (hh + 1) * dh)
            kn = _head_norm(acc[:, cols], gk_ref[...])
            kmean_ref[b_local:b_local + 1, cols] = jnp.mean(kn, axis=0, keepdims=True)
            sl = slope_ref[hh][:, :dh]
            extra = (onehot + jnp.where(lane == COL_KEY_OFF, sl * key_off, 0.0)
                     + jnp.where(lane_row == COL_BLK_OFF, sl * blk_start, 0.0))
            kaug_ref[hh, rows, 0:dh] = kn.astype(BF16)
            kaug_ref[hh, rows, dh:2 * dh] = extra.astype(BF16)

    _sub_dots(x_ref, wbf_ref[...], emit, lhs_rows=norm_rows)


def _q_kernel(h_ref, w_ref, gq_ref, kmean_ref, slope_ref, qaug_ref, wbf_ref):
    m = pl.program_id(1)
    _cast_weight_once(w_ref, wbf_ref)
    assert SUB_ROWS == MOBA_BLOCK
    blk, dh = MOBA_BLOCK, HEAD_DIM_A
    blk_id = lax.broadcasted_iota(jnp.int32, (N_BLOCKS, blk), 0)
    blk_id_f = blk_id.astype(F32)
    rest = lax.broadcasted_iota(jnp.int32, (dh - N_BLOCKS, blk), 0) + N_BLOCKS

    def emit(rows, acc):
        i = m * (h_ref.shape[0] // blk) + rows.start // blk
        past = blk_id < i
        q_start = jnp.full((1, blk), i * blk, jnp.int32).astype(F32)
        for hh in range(N_HEADS_A):
            cols = slice(hh * dh, (hh + 1) * dh)
            q = _head_norm(acc[:, cols], gq_ref[...]) * dh ** -0.5
            q_t = q.T
            gate = jnp.dot(kmean_ref[:, cols].astype(BF16), q_t.astype(BF16),
                           preferred_element_type=F32)
            gate = jnp.where(past, gate, NEG_INF)
            pen = jnp.full((N_BLOCKS, blk), NEG_INF, F32)
            for _ in range(MOBA_TOPK):
                mx = jnp.max(gate, axis=0, keepdims=True)
                first = jnp.min(jnp.where(gate == mx, blk_id_f, float(N_BLOCKS)),
                                axis=0, keepdims=True)
                hit = blk_id_f == first
                pen = jnp.where(jnp.logical_and(hit, mx > 0.5 * NEG_INF), 0.0, pen)
                gate = jnp.where(hit, GATE_TAKEN, gate)
            pen = jnp.where(blk_id == i, 0.0, pen)
            extra_q = jnp.where(rest == COL_ONES, -slope_ref[hh] * q_start,
                                jnp.where(rest < COL_ONES, 1.0, 0.0))
            qaug_ref[hh, :, rows] = jnp.concatenate([q_t, pen, extra_q], axis=0).astype(BF16)

    _sub_dots(h_ref, wbf_ref[...], emit)


def _v_kernel(h_ref, w_ref, vt_ref, wbf_ref):
    _cast_weight_once(w_ref, wbf_ref)
    assert SUB_ROWS == MOBA_BLOCK
    blk, dh = MOBA_BLOCK, HEAD_DIM_A

    def emit(rows, acc):
        b_local = rows.start // blk
        for hh in range(N_HEADS_A):
            vt_ref[hh, b_local, 0:dh, :] = acc[:, hh * dh:(hh + 1) * dh].T.astype(BF16)
            vt_ref[hh, b_local, dh:dh + SUM_ROWS, :] = jnp.ones((SUM_ROWS, blk), BF16)

    _sub_dots(h_ref, wbf_ref[...], emit)


def _qkv_proj(x, g_mix, w_in, gq_a, gk_a, slopes, layer):
    s = x.shape[0]
    tm, dh, nh = TM, HEAD_DIM_A, N_HEADS_A
    grid = (1, s // tm)
    h_spec = pl.BlockSpec((tm, D_MODEL), lambda n, m: (m, 0))
    gain = pl.BlockSpec((None, 1, dh), lambda n, m: (layer, 0, 0))
    slope = pl.BlockSpec((nh, 1, MOBA_BLOCK), lambda n, m: (0, 0, 0))
    scratch = [pltpu.VMEM((D_MODEL, TN), BF16)]
    params = _params("arbitrary", "arbitrary")

    def w_spec(col):
        return pl.BlockSpec((None, D_MODEL, TN), lambda n, m: (layer, 0, col))

    kaug, kmeans, h = pl.pallas_call(
        _k_kernel,
        out_shape=(jax.ShapeDtypeStruct((nh, s, 2 * dh), BF16),
                   jax.ShapeDtypeStruct((s // tm, tm // MOBA_BLOCK, WIDTH_A), F32),
                   jax.ShapeDtypeStruct((s, D_MODEL), BF16)),
        grid=grid,
        in_specs=[h_spec, pl.BlockSpec((None, 1, D_MODEL), lambda n, m: (layer, 0, 0)),
                  w_spec(1), gain, slope],
        out_specs=(pl.BlockSpec((nh, tm, 2 * dh), lambda n, m: (0, m, 0)),
                   pl.BlockSpec((None, tm // MOBA_BLOCK, WIDTH_A), lambda n, m: (m, 0, 0)),
                   h_spec),
        scratch_shapes=scratch, compiler_params=params, name="k_proj",
    )(x, g_mix, w_in, gk_a, slopes)
    kmean = kmeans.reshape(N_BLOCKS, WIDTH_A)
    qaug = pl.pallas_call(
        _q_kernel,
        out_shape=jax.ShapeDtypeStruct((nh, 2 * dh, s), BF16),
        grid=grid,
        in_specs=[h_spec, w_spec(0), gain,
                  pl.BlockSpec((N_BLOCKS, WIDTH_A), lambda n, m: (0, 0)), slope],
        out_specs=pl.BlockSpec((nh, 2 * dh, tm), lambda n, m: (0, 0, m)),
        scratch_shapes=scratch, compiler_params=params, name="q_proj",
    )(h, w_in, gq_a, kmean, slopes)
    vt = pl.pallas_call(
        _v_kernel,
        out_shape=jax.ShapeDtypeStruct((nh, N_BLOCKS, dh + SUM_ROWS, MOBA_BLOCK), BF16),
        grid=grid,
        in_specs=[h_spec, w_spec(2)],
        out_specs=pl.BlockSpec((nh, tm // MOBA_BLOCK, dh + SUM_ROWS, MOBA_BLOCK),
                               lambda n, m: (0, m, 0, 0)),
        scratch_shapes=scratch, compiler_params=params, name="v_proj",
    )(h, w_in)
    return h, qaug, kaug, vt


def _sgu_kernel(h_ref, wu_ref, wv_ref, g_ref, ws_ref, bt_ref, o_ref,
                wubf_ref, wvbf_ref, wsbf_ref):
    @pl.when(pl.program_id(0) == 0)
    def _():
        wubf_ref[...] = wu_ref[...].astype(BF16)
        wvbf_ref[...] = wv_ref[...].astype(BF16)
        row = lax.broadcasted_iota(jnp.int32, (SGU_CHUNK, SGU_CHUNK), 0)
        col = lax.broadcasted_iota(jnp.int32, (SGU_CHUNK, SGU_CHUNK), 1)
        for g in range(SGU_GROUPS):
            wsbf_ref[g] = jnp.where(col <= row, ws_ref[g], 0.0).astype(BF16)

    def proj(r):
        rows = slice(r, r + SUB_ROWS)
        return (rows,
                jnp.dot(h_ref[rows, :], wubf_ref[...], preferred_element_type=F32),
                jnp.dot(h_ref[rows, :], wvbf_ref[...], preferred_element_type=F32))

    def mix(rows, u_acc, v_acc):
        u = jax.nn.gelu(u_acc)
        v = jax.nn.gelu(v_acc)
        ms = jnp.mean(v * v, axis=-1, keepdims=True)
        vn = (v * lax.rsqrt(ms + EPS) * g_ref[...]).astype(BF16)
        chunks = [slice(c, c + SGU_CHUNK) for c in range(0, SUB_ROWS, SGU_CHUNK)]
        for g in range(SGU_GROUPS):
            cols = slice(g * SGU_CHUNK, (g + 1) * SGU_CHUNK)
            v_all = jnp.concatenate([vn[c, cols] for c in chunks], axis=1)
            mixed = jnp.dot(wsbf_ref[g], v_all, preferred_element_type=F32) + bt_ref[:, g:g + 1]
            for c in chunks:
                o_ref[rows.start + c.start:rows.start + c.stop, cols] = (
                    u[c, cols] * mixed[:, c]).astype(o_ref.dtype)

    n_rows = h_ref.shape[0]
    cur = proj(0)
    for r in range(0, n_rows, SUB_ROWS):
        nxt = proj(r + SUB_ROWS) if r + SUB_ROWS < n_rows else None
        mix(*cur)
        cur = nxt


def _sgu_branch(h, w_in, g_sgu, w_sgu, b_sgu_t, layer):
    s = h.shape[0]
    off = COL_UV // TN
    once = pl.Buffered(1)
    return pl.pallas_call(
        _sgu_kernel,
        out_shape=jax.ShapeDtypeStruct((s, SGU_WIDTH), BF16),
        grid=(s // TM,),
        in_specs=[pl.BlockSpec((TM, D_MODEL), lambda m: (m, 0)),
                  pl.BlockSpec((None, D_MODEL, TN), lambda m: (layer, 0, off),
                               pipeline_mode=once),
                  pl.BlockSpec((None, D_MODEL, TN), lambda m: (layer, 0, off + 1),
                               pipeline_mode=once),
                  pl.BlockSpec((None, 1, SGU_WIDTH), lambda m: (layer, 0, 0)),
                  pl.BlockSpec((None, SGU_GROUPS, SGU_CHUNK, SGU_CHUNK),
                               lambda m: (layer, 0, 0, 0)),
                  pl.BlockSpec((None, SGU_CHUNK, SGU_GROUPS), lambda m: (layer, 0, 0))],
        out_specs=pl.BlockSpec((TM, SGU_WIDTH), lambda m: (m, 0)),
        scratch_shapes=[pltpu.VMEM((D_MODEL, TN), BF16), pltpu.VMEM((D_MODEL, TN), BF16),
                        pltpu.VMEM((SGU_GROUPS, SGU_CHUNK, SGU_CHUNK), BF16)],
        compiler_params=_params("arbitrary"),
        name="sgu_branch",
    )(h, w_in, w_in, g_sgu, w_sgu, b_sgu_t)


def _memkv_kernel(mem_ref, gmem_ref, w_ref, gk_ref, o_ref):
    n = pl.program_id(0)
    x = mem_ref[...]
    ms = jnp.mean(x * x, axis=-1, keepdims=True)
    hm = (x * lax.rsqrt(ms + EPS) * gmem_ref[...]).astype(BF16)
    acc = jnp.dot(hm, w_ref[...].astype(BF16), preferred_element_type=F32)

    @pl.when(n == 0)
    def _():
        o_ref[...] = _group_rmsnorm(acc, gk_ref[...], HEAD_DIM_M).astype(o_ref.dtype)

    @pl.when(n == 1)
    def _():
        o_ref[...] = acc.astype(o_ref.dtype)


def _memkv_proj(mem, g_mem, w_mem_kv, gk_m, layer):
    return pl.pallas_call(
        _memkv_kernel,
        out_shape=jax.ShapeDtypeStruct((2, N_MEM, WIDTH_M), BF16),
        grid=(2,),
        in_specs=[pl.BlockSpec((N_MEM, D_MODEL), lambda n: (0, 0)),
                  pl.BlockSpec((None, 1, D_MODEL), lambda n: (layer, 0, 0)),
                  pl.BlockSpec((None, D_MODEL, WIDTH_M), lambda n: (layer, 0, n)),
                  pl.BlockSpec((None, 1, HEAD_DIM_M), lambda n: (layer, 0, 0))],
        out_specs=pl.BlockSpec((None, N_MEM, WIDTH_M), lambda n: (n, 0, 0)),
        compiler_params=_params("arbitrary"),
        name="memkv_proj",
    )(mem, g_mem, w_mem_kv, gk_m)


def _qm_kernel(h_ref, w_ref, gq_ref, km_ref, vm_ref, o_ref, wbf_ref):
    _cast_weight_once(w_ref, wbf_ref)

    def emit(rows, acc):
        heads = [slice(hh * HEAD_DIM_M, (hh + 1) * HEAD_DIM_M) for hh in range(N_HEADS_M)]
        scores = []
        for cols in heads:
            qh = acc[:, cols]
            ms = jnp.mean(qh * qh, axis=-1, keepdims=True)
            qn = (qh * lax.rsqrt(ms + EPS) * gq_ref[...]).astype(BF16)
            scores.append(lax.dot_general(qn, km_ref[:, cols], (((1,), (1,)), ((), ())),
                                          preferred_element_type=F32) * HEAD_DIM_M ** -0.5)
        weights = []
        for sc in scores:
            mx = jnp.max(sc, axis=-1, keepdims=True)
            e = jnp.exp(sc - mx)
            weights.append((e / jnp.sum(e, axis=-1, keepdims=True)).astype(BF16))
        for cols, p in zip(heads, weights):
            o_ref[rows, cols] = jnp.dot(
                p, vm_ref[:, cols], preferred_element_type=F32).astype(o_ref.dtype)

    _sub_dots(h_ref, wbf_ref[...], emit)


def _qm_memattn(h, w_in, gq_m, kvm, layer):
    s = h.shape[0]
    off = COL_QM // TN
    tm = TM_BIG
    return pl.pallas_call(
        _qm_kernel,
        out_shape=jax.ShapeDtypeStruct((s, WIDTH_M), BF16),
        grid=(1, s // tm),
        in_specs=[pl.BlockSpec((tm, D_MODEL), lambda n, m: (m, 0)),
                  pl.BlockSpec((None, D_MODEL, TN), lambda n, m: (layer, 0, off)),
                  pl.BlockSpec((None, 1, HEAD_DIM_M), lambda n, m: (layer, 0, 0)),
                  pl.BlockSpec((None, N_MEM, WIDTH_M), lambda n, m: (0, 0, 0)),
                  pl.BlockSpec((None, N_MEM, WIDTH_M), lambda n, m: (1, 0, 0))],
        out_specs=pl.BlockSpec((tm, WIDTH_M), lambda n, m: (m, 0)),
        scratch_shapes=[pltpu.VMEM((D_MODEL, TN), BF16)],
        compiler_params=_params("arbitrary", "arbitrary"),
        name="qm_memattn",
    )(h, w_in, gq_m, kvm, kvm)


def _gates_kernel(h_ref, w_ref, o_ref, wbf_ref):
    _cast_weight_once(w_ref, wbf_ref)
    def emit(rows, acc):
        o_ref[rows, :] = jax.nn.sigmoid(acc).astype(o_ref.dtype)

    _sub_dots(h_ref, wbf_ref[...], emit)


def _gates_proj(h, w_in, layer):
    s = h.shape[0]
    off = COL_GATES // TN
    n_tiles = 3 * D_MODEL // TN
    return pl.pallas_call(
        _gates_kernel,
        out_shape=jax.ShapeDtypeStruct((s, 3 * D_MODEL), BF16),
        grid=(n_tiles, s // TM_BIG),
        in_specs=[pl.BlockSpec((TM_BIG, D_MODEL), lambda n, m: (m, 0)),
                  pl.BlockSpec((None, D_MODEL, TN), lambda n, m: (layer, 0, off + n))],
        out_specs=pl.BlockSpec((TM_BIG, TN), lambda n, m: (m, n)),
        scratch_shapes=[pltpu.VMEM((D_MODEL, TN), BF16)],
        compiler_params=_params("arbitrary", "arbitrary"),
        name="gates_proj",
    )(h, w_in)


MOBA_HALF = 2
MOBA_HEADS = 2
M_INIT = -3e38


def _moba_kernel(qaug_ref, qnext_ref, kaug_ref, vt_ref, o_ref, s_ref, s2_ref, snext_ref):
    i = pl.program_id(1)
    blk, dh = MOBA_BLOCK, HEAD_DIM_A
    heads = range(MOBA_HEADS)
    half = MOBA_HALF
    half_rows = half * blk
    key_minus_qry = (lax.broadcasted_iota(jnp.int32, (half_rows, blk), 0)
                     - lax.broadcasted_iota(jnp.int32, (half_rows, blk), 1))

    def score(hh, first_blk):
        rows = pl.ds(pl.multiple_of(first_blk * blk, half_rows), half_rows)
        return jnp.dot(kaug_ref[hh, rows, :], qaug_ref[hh], preferred_element_type=F32)

    def attend(hh, st, s, first_blk, causal):
        m, acc = st
        if causal:
            s = jnp.where(key_minus_qry <= (i - first_blk) * blk, s, NEG_INF)
        m_new = jnp.maximum(m, jnp.max(s, axis=0, keepdims=True))
        alpha = jnp.exp(m - m_new)
        p = jnp.exp(s - m_new)
        v_t = jnp.concatenate([vt_ref[hh, first_blk + g] for g in range(half)], axis=1)
        acc = alpha * acc + jnp.dot(v_t, p.astype(BF16), preferred_element_type=F32)
        return m_new, acc

    def trip(t, carry, last):
        first = t * (2 * half)
        if last:
            for hh in heads:
                s2_ref[hh] = score(hh, first + half)
            for hh in heads:
                snext_ref[hh] = jnp.dot(kaug_ref[hh, 0:half_rows, :], qnext_ref[hh],
                                        preferred_element_type=F32)
            carry = tuple(attend(hh, carry[hh], s_ref[hh], first, True) for hh in heads)

            def second_half(c):
                return tuple(attend(hh, c[hh], s2_ref[hh], first + half, True) for hh in heads)

            return lax.cond(i >= first + half, second_half, lambda c: c, carry)
        s_second = [score(hh, first + half) for hh in heads]
        carry = [attend(hh, carry[hh], s_ref[hh], first, False) for hh in heads]
        for hh in heads:
            s_ref[hh] = score(hh, first + 2 * half)
        carry = [attend(hh, carry[hh], s_second[hh], first + half, False) for hh in heads]
        return tuple(carry)

    @pl.when(i == 0)
    def _():
        for hh in heads:
            s_ref[hh] = score(hh, 0)

    @pl.when(i > 0)
    def _():
        for hh in heads:
            s_ref[hh] = snext_ref[hh]

    state = tuple((jnp.full((1, blk), M_INIT, F32), jnp.zeros((dh + SUM_ROWS, blk), F32))
                  for _ in heads)
    last_trip = i // (2 * half)
    state = lax.fori_loop(0, last_trip // 2,
                          lambda u, c: trip(2 * u + 1, trip(2 * u, c, False), False), state)
    state = lax.fori_loop(0, last_trip % 2, lambda _, c: trip(last_trip - 1, c, False), state)
    state = trip(last_trip, state, True)
    for hh in heads:
        _, acc = state[hh]
        out = acc[0:dh] / acc[dh:dh + 1]
        o_ref[:, hh * dh:(hh + 1) * dh] = out.T.astype(o_ref.dtype)


def _moba(qaug, kaug, vt):
    s = kaug.shape[1]
    nh = MOBA_HEADS
    return pl.pallas_call(
        _moba_kernel,
        out_shape=jax.ShapeDtypeStruct((s, WIDTH_A), BF16),
        grid=(N_HEADS_A // nh, s // MOBA_BLOCK),
        in_specs=[pl.BlockSpec((nh, 2 * HEAD_DIM_A, MOBA_BLOCK), lambda h, i: (h, 0, i)),
                  pl.BlockSpec((nh, 2 * HEAD_DIM_A, MOBA_BLOCK),
                               lambda h, i: (h, 0, jnp.minimum(i + 1, N_BLOCKS - 1))),
                  pl.BlockSpec((nh, s, 2 * HEAD_DIM_A), lambda h, i: (h, 0, 0)),
                  pl.BlockSpec((nh, N_BLOCKS, HEAD_DIM_A + SUM_ROWS, MOBA_BLOCK),
                               lambda h, i: (h, 0, 0, 0))],
        out_specs=pl.BlockSpec((MOBA_BLOCK, nh * HEAD_DIM_A), lambda h, i: (i, h)),
        scratch_shapes=[pltpu.VMEM((nh, MOBA_HALF * MOBA_BLOCK, MOBA_BLOCK), F32)] * 3,
        compiler_params=_params("arbitrary", "arbitrary"),
        name="moba_attention",
    )(qaug, qaug, kaug, vt)


MERGE_TN = 1024
MERGE_TM = 512


def _merge_kernel(ya_ref, yb_ref, ym_ref, ga_ref, gb_ref, gm_ref, w_ref, o_ref, wbf_ref):
    _cast_weight_once(w_ref, wbf_ref)
    for r in range(0, ya_ref.shape[0], SUB_ROWS):
        rows = slice(r, r + SUB_ROWS)
        merged = ga_ref[rows, :] * jnp.dot(ya_ref[rows, :], wbf_ref[0],
                                           preferred_element_type=F32)
        merged += gb_ref[rows, :] * jnp.dot(yb_ref[rows, :], wbf_ref[1],
                                            preferred_element_type=F32)
        merged += gm_ref[rows, :] * jnp.dot(ym_ref[rows, :], wbf_ref[2],
                                            preferred_element_type=F32)
        o_ref[rows, :] = merged.astype(o_ref.dtype)


def _merge(ya, yb, ym, gates, w_branch, layer):
    s = ya.shape[0]
    tn = MERGE_TN
    nt = D_MODEL // tn
    tm = MERGE_TM
    act = pl.BlockSpec((tm, WIDTH_A), lambda n, m: (m, 0))
    return pl.pallas_call(
        _merge_kernel,
        out_shape=jax.ShapeDtypeStruct((s, D_MODEL), BF16),
        grid=(nt, s // tm),
        in_specs=[act, act, act,
                  pl.BlockSpec((tm, tn), lambda n, m: (m, n)),
                  pl.BlockSpec((tm, tn), lambda n, m: (m, nt + n)),
                  pl.BlockSpec((tm, tn), lambda n, m: (m, 2 * nt + n)),
                  pl.BlockSpec((None, 3, WIDTH_A, tn), lambda n, m: (layer, 0, 0, n))],
        out_specs=pl.BlockSpec((tm, tn), lambda n, m: (m, n)),
        scratch_shapes=[pltpu.VMEM((3, WIDTH_A, tn), BF16)],
        compiler_params=_params("arbitrary", "arbitrary"),
        name="branch_merge",
    )(ya, yb, ym, gates, gates, gates, w_branch)


def _resid_proj_kernel(a_ref, w_ref, x_ref, o_ref, wbf_ref):
    _cast_weight_once(w_ref, wbf_ref)
    def emit(rows, acc):
        o_ref[rows, :] = x_ref[rows, :] + acc

    _sub_dots(a_ref, wbf_ref[...], emit)


def _resid_proj(a, w, x, layer, tm, tn, name):
    s, k = a.shape
    n_out = w.shape[2]
    return pl.pallas_call(
        _resid_proj_kernel,
        out_shape=jax.ShapeDtypeStruct((s, n_out), F32),
        grid=(n_out // tn, s // tm),
        in_specs=[pl.BlockSpec((tm, k), lambda n, m: (m, 0)),
                  pl.BlockSpec((None, k, tn), lambda n, m: (layer, 0, n)),
                  pl.BlockSpec((tm, tn), lambda n, m: (m, n))],
        out_specs=pl.BlockSpec((tm, tn), lambda n, m: (m, n)),
        scratch_shapes=[pltpu.VMEM((k, tn), BF16)],
        compiler_params=_params("arbitrary", "arbitrary"),
        name=name,
    )(a, w, x)


OUT_TM = 512


def _out_proj_norm_kernel(a_ref, w_ref, x_ref, g_ref, o_ref, h_ref, wbf_ref):
    @pl.when(pl.program_id(0) == 0)
    def _():
        wbf_ref[...] = w_ref[...].astype(BF16)

    def emit(rows, acc):
        x = x_ref[rows, :] + acc
        o_ref[rows, :] = x
        ms = jnp.mean(x * x, axis=-1, keepdims=True)
        h_ref[rows, :] = (x * lax.rsqrt(ms + EPS) * g_ref[...]).astype(h_ref.dtype)

    _sub_dots(a_ref, wbf_ref[...], emit)


def _out_proj_norm(a, w, x, g_all, layer):
    s, k = a.shape
    tm = OUT_TM
    return pl.pallas_call(
        _out_proj_norm_kernel,
        out_shape=(jax.ShapeDtypeStruct((s, D_MODEL), F32),
                   jax.ShapeDtypeStruct((s, D_MODEL), BF16)),
        grid=(s // tm,),
        in_specs=[pl.BlockSpec((tm, k), lambda m: (m, 0)),
                  pl.BlockSpec((None, k, D_MODEL), lambda m: (layer, 0, 0),
                               pipeline_mode=pl.Buffered(1)),
                  pl.BlockSpec((tm, D_MODEL), lambda m: (m, 0)),
                  pl.BlockSpec((None, 1, D_MODEL), lambda m: (layer, 0, 0))],
        out_specs=(pl.BlockSpec((tm, D_MODEL), lambda m: (m, 0)),
                   pl.BlockSpec((tm, D_MODEL), lambda m: (m, 0))),
        scratch_shapes=[pltpu.VMEM((k, D_MODEL), BF16)],
        compiler_params=_params("arbitrary"),
        name="out_proj_norm",
    )(a, w, x, g_all)


FFN_TN = 512
FFN_DOWN_TM = 512
FFN_DOWN_TN = 512


def _ffn_up_kernel(h_ref, wg_ref, wu_ref, o_ref, wgbf_ref, wubf_ref):
    @pl.when(pl.program_id(1) == 0)
    def _():
        wgbf_ref[...] = wg_ref[...].astype(BF16)
        wubf_ref[...] = wu_ref[...].astype(BF16)
    for r in range(0, h_ref.shape[0], SUB_ROWS):
        rows = slice(r, r + SUB_ROWS)
        gt = jnp.dot(h_ref[rows, :], wgbf_ref[...], preferred_element_type=F32)
        up = jnp.dot(h_ref[rows, :], wubf_ref[...], preferred_element_type=F32)
        o_ref[rows, :] = (jax.nn.silu(gt) * up).astype(o_ref.dtype)


def _ffn_up(h, w_gate_up, layer):
    s = h.shape[0]
    tn = FFN_TN
    nt = D_FF // tn
    tm = TM_BIG
    return pl.pallas_call(
        _ffn_up_kernel,
        out_shape=jax.ShapeDtypeStruct((s, D_FF), BF16),
        grid=(nt, s // tm),
        in_specs=[pl.BlockSpec((tm, D_MODEL), lambda n, m: (m, 0)),
                  pl.BlockSpec((None, D_MODEL, tn), lambda n, m: (layer, 0, n)),
                  pl.BlockSpec((None, D_MODEL, tn), lambda n, m: (layer, 0, nt + n))],
        out_specs=pl.BlockSpec((tm, tn), lambda n, m: (m, n)),
        scratch_shapes=[pltpu.VMEM((D_MODEL, tn), BF16), pltpu.VMEM((D_MODEL, tn), BF16)],
        compiler_params=_params("arbitrary", "arbitrary"),
        name="ffn_up",
    )(h, w_gate_up, w_gate_up)


def kernel(x, mem, g_mix, w_in, gq_a, gk_a, g_sgu, w_sgu, b_sgu, gq_m, gk_m, g_mem,
           w_mem_kv, w_branch, w_out, g_ffn, w_gate_up, w_down):
    b, s, d = x.shape
    assert (b, s, d) == (1, SEQ, D_MODEL) and mem.shape == (1, N_MEM, D_MODEL)
    x2 = x.reshape(s, d)
    mem2 = mem.reshape(N_MEM, d)

    def row(p):
        return p.reshape(DEPTH, 1, p.shape[-1])

    g_mix3, g_ffn3, g_mem3, g_sgu3 = row(g_mix), row(g_ffn), row(g_mem), row(g_sgu)
    gq_a3, gk_a3, gq_m3, gk_m3 = row(gq_a), row(gk_a), row(gq_m), row(gk_m)
    b_sgu_t = jnp.swapaxes(b_sgu, 1, 2)
    slopes = 2.0 ** (-8.0 * jnp.arange(1, N_HEADS_A + 1, dtype=F32) / N_HEADS_A)
    slopes = jnp.broadcast_to(slopes[:, None, None], (N_HEADS_A, 1, MOBA_BLOCK))

    for layer in range(DEPTH):
        h, qaug, kaug, vt = _qkv_proj(x2, g_mix3, w_in, gq_a3, gk_a3, slopes, layer)
        yb = _sgu_branch(h, w_in, g_sgu3, w_sgu, b_sgu_t, layer)
        kvm = _memkv_proj(mem2, g_mem3, w_mem_kv, gk_m3, layer)
        ym = _qm_memattn(h, w_in, gq_m3, kvm, layer)
        gates = _gates_proj(h, w_in, layer)
        ya = _moba(qaug, kaug, vt)
        merged = _merge(ya, yb, ym, gates, w_branch, layer)
        x2, hf = _out_proj_norm(merged, w_out, x2, g_ffn3, layer)
        act = _ffn_up(hf, w_gate_up, layer)
        x2 = _resid_proj(act, w_down, x2, layer, FFN_DOWN_TM, FFN_DOWN_TN, "ffn_down")
    return x2.reshape(b, s, d)
```

```python
import jax
import jax.numpy as jnp
from jax import lax
from jax.experimental import pallas as pl
from jax.experimental.pallas import tpu as pltpu

F32 = jnp.float32
BF16 = jnp.bfloat16

D_MODEL = 2048
SEQ = 8192
DEPTH = 2
N_HEADS_A = 8
HEAD_DIM_A = 128
WIDTH_A = N_HEADS_A * HEAD_DIM_A
MOBA_BLOCK = 256
MOBA_TOPK = 3
N_BLOCKS = SEQ // MOBA_BLOCK
SGU_WIDTH = 1024
SGU_GROUPS = 8
SGU_CHUNK = 128
N_MEM = 256
N_HEADS_M = 4
HEAD_DIM_M = 256
WIDTH_M = N_HEADS_M * HEAD_DIM_M
D_FF = 5632
NEG_INF = -1e30
EPS = 1e-6

COL_UV = 3 * WIDTH_A
COL_QM = COL_UV + 2 * SGU_WIDTH
COL_GATES = COL_QM + WIDTH_M

VMEM_LIMIT_BYTES = 60 * 1024 * 1024

TM = 1024
TM_BIG = 2048
TN = 1024
SUB_ROWS = 256


def _params(*semantics):
    return pltpu.CompilerParams(dimension_semantics=semantics,
                                vmem_limit_bytes=VMEM_LIMIT_BYTES)


def _cast_weight_once(w_ref, wbf_ref):
    @pl.when(pl.program_id(1) == 0)
    def _():
        wbf_ref[...] = w_ref[...].astype(BF16)


def _group_rmsnorm(a, g, width):
    outs = []
    for s in range(0, a.shape[1], width):
        blk = a[:, s:s + width]
        ms = jnp.mean(blk * blk, axis=-1, keepdims=True)
        outs.append(blk * lax.rsqrt(ms + EPS) * g)
    return jnp.concatenate(outs, axis=1)


def _sub_dots(lhs_ref, rhs, emit, lhs_rows=None):
    def sub_dot(r):
        rows = slice(r, r + SUB_ROWS)
        lhs = lhs_rows(rows) if lhs_rows is not None else lhs_ref[rows, :]
        return rows, jnp.dot(lhs, rhs, preferred_element_type=F32)

    n_rows = lhs_ref.shape[0]
    cur = sub_dot(0)
    for r in range(0, n_rows, SUB_ROWS):
        nxt = sub_dot(r + SUB_ROWS) if r + SUB_ROWS < n_rows else None
        emit(*cur)
        cur = nxt


COL_KEY_OFF = N_BLOCKS
COL_BLK_OFF = N_BLOCKS + 1
COL_ONES = N_BLOCKS + 2
SUM_ROWS = 16
GATE_TAKEN = -3e38


def _head_norm(a, g):
    ms = jnp.mean(a * a, axis=-1, keepdims=True)
    return a * lax.rsqrt(ms + EPS) * g


def _k_kernel(x_ref, g_ref, w_ref, gk_ref, slope_ref, kaug_ref, kmean_ref, h_ref, wbf_ref):
    m = pl.program_id(1)

    def norm_rows(rows):
        x = x_ref[rows, :]
        ms = jnp.mean(x * x, axis=-1, keepdims=True)
        h = (x * lax.rsqrt(ms + EPS) * g_ref[...]).astype(BF16)
        h_ref[rows, :] = h
        return h

    _cast_weight_once(w_ref, wbf_ref)
    assert SUB_ROWS == MOBA_BLOCK
    blk, dh = MOBA_BLOCK, HEAD_DIM_A
    lane = lax.broadcasted_iota(jnp.int32, (blk, dh), 1)
    lane_row = lax.broadcasted_iota(jnp.int32, (1, dh), 1)
    key_off = lax.broadcasted_iota(jnp.int32, (blk, dh), 0).astype(F32)

    def emit(rows, acc):
        b_local = rows.start // blk
        b = m * (h_ref.shape[0] // blk) + b_local
        blk_start = jnp.full((1, dh), b * blk, jnp.int32).astype(F32)
        onehot = jnp.where(lane == b, 1.0, jnp.where(lane == COL_ONES, 1.0, 0.0))
        for hh in range(N_HEADS_A):
            cols = slice(hh * dh, (hh + 1) * dh)
            kn = _head_norm(acc[:, cols], gk_ref[...])
            kmean_ref[b_local:b_local + 1, cols] = jnp.mean(kn, axis=0, keepdims=True)
            sl = slope_ref[hh][:, :dh]
            extra = (onehot + jnp.where(lane == COL_KEY_OFF, sl * key_off, 0.0)
                     + jnp.where(lane_row == COL_BLK_OFF, sl * blk_start, 0.0))
            kaug_ref[hh, rows, 0:dh] = kn.astype(BF16)
            kaug_ref[hh, rows, dh:2 * dh] = extra.astype(BF16)

    _sub_dots(x_ref, wbf_ref[...], emit, lhs_rows=norm_rows)


def _q_kernel(h_ref, w_ref, gq_ref, kmean_ref, slope_ref, qaug_ref, wbf_ref):
    m = pl.program_id(1)
    _cast_weight_once(w_ref, wbf_ref)
    assert SUB_ROWS == MOBA_BLOCK
    blk, dh = MOBA_BLOCK, HEAD_DIM_A
    blk_id = lax.broadcasted_iota(jnp.int32, (N_BLOCKS, blk), 0)
    blk_id_f = blk_id.astype(F32)
    rest = lax.broadcasted_iota(jnp.int32, (dh - N_BLOCKS, blk), 0) + N_BLOCKS

    def emit(rows, acc):
        i = m * (h_ref.shape[0] // blk) + rows.start // blk
        past = blk_id < i
        q_start = jnp.full((1, blk), i * blk, jnp.int32).astype(F32)
        for hh in range(N_HEADS_A):
            cols = slice(hh * dh, (hh + 1) * dh)
            q = _head_norm(acc[:, cols], gq_ref[...]) * dh ** -0.5
            q_t = q.T
            gate = jnp.dot(kmean_ref[:, cols].astype(BF16), q_t.astype(BF16),
                           preferred_element_type=F32)
            gate = jnp.where(past, gate, NEG_INF)
            pen = jnp.full((N_BLOCKS, blk), NEG_INF, F32)
            for _ in range(MOBA_TOPK):
                mx = jnp.max(gate, axis=0, keepdims=True)
                first = jnp.min(jnp.where(gate == mx, blk_id_f, float(N_BLOCKS)),
                                axis=0, keepdims=True)
                hit = blk_id_f == first
                pen = jnp.where(jnp.logical_and(hit, mx > 0.5 * NEG_INF), 0.0, pen)
                gate = jnp.where(hit, GATE_TAKEN, gate)
            pen = jnp.where(blk_id == i, 0.0, pen)
            extra_q = jnp.where(rest == COL_ONES, -slope_ref[hh] * q_start,
                                jnp.where(rest < COL_ONES, 1.0, 0.0))
            qaug_ref[hh, :, rows] = jnp.concatenate([q_t, pen, extra_q], axis=0).astype(BF16)

    _sub_dots(h_ref, wbf_ref[...], emit)


def _v_kernel(h_ref, w_ref, vt_ref, wbf_ref):
    _cast_weight_once(w_ref, wbf_ref)
    assert SUB_ROWS == MOBA_BLOCK
    blk, dh = MOBA_BLOCK, HEAD_DIM_A

    def emit(rows, acc):
        b_local = rows.start // blk
        for hh in range(N_HEADS_A):
            vt_ref[hh, b_local, 0:dh, :] = acc[:, hh * dh:(hh + 1) * dh].T.astype(BF16)
            vt_ref[hh, b_local, dh:dh + SUM_ROWS, :] = jnp.ones((SUM_ROWS, blk), BF16)

    _sub_dots(h_ref, wbf_ref[...], emit)


def _qkv_proj(x, g_mix, w_in, gq_a, gk_a, slopes, layer):
    s = x.shape[0]
    tm, dh, nh = TM, HEAD_DIM_A, N_HEADS_A
    grid = (1, s // tm)
    h_spec = pl.BlockSpec((tm, D_MODEL), lambda n, m: (m, 0))
    gain = pl.BlockSpec((None, 1, dh), lambda n, m: (layer, 0, 0))
    slope = pl.BlockSpec((nh, 1, MOBA_BLOCK), lambda n, m: (0, 0, 0))
    scratch = [pltpu.VMEM((D_MODEL, TN), BF16)]
    params = _params("arbitrary", "arbitrary")

    def w_spec(col):
        return pl.BlockSpec((None, D_MODEL, TN), lambda n, m: (layer, 0, col))

    kaug, kmeans, h = pl.pallas_call(
        _k_kernel,
        out_shape=(jax.ShapeDtypeStruct((nh, s, 2 * dh), BF16),
                   jax.ShapeDtypeStruct((s // tm, tm // MOBA_BLOCK, WIDTH_A), F32),
                   jax.ShapeDtypeStruct((s, D_MODEL), BF16)),
        grid=grid,
        in_specs=[h_spec, pl.BlockSpec((None, 1, D_MODEL), lambda n, m: (layer, 0, 0)),
                  w_spec(1), gain, slope],
        out_specs=(pl.BlockSpec((nh, tm, 2 * dh), lambda n, m: (0, m, 0)),
                   pl.BlockSpec((None, tm // MOBA_BLOCK, WIDTH_A), lambda n, m: (m, 0, 0)),
                   h_spec),
        scratch_shapes=scratch, compiler_params=params, name="k_proj",
    )(x, g_mix, w_in, gk_a, slopes)
    kmean = kmeans.reshape(N_BLOCKS, WIDTH_A)
    qaug = pl.pallas_call(
        _q_kernel,
        out_shape=jax.ShapeDtypeStruct((nh, 2 * dh, s), BF16),
        grid=grid,
        in_specs=[h_spec, w_spec(0), gain,
                  pl.BlockSpec((N_BLOCKS, WIDTH_A), lambda n, m: (0, 0)), slope],
        out_specs=pl.BlockSpec((nh, 2 * dh, tm), lambda n, m: (0, 0, m)),
        scratch_shapes=scratch, compiler_params=params, name="q_proj",
    )(h, w_in, gq_a, kmean, slopes)
    vt = pl.pallas_call(
        _v_kernel,
        out_shape=jax.ShapeDtypeStruct((nh, N_BLOCKS, dh + SUM_ROWS, MOBA_BLOCK), BF16),
        grid=grid,
        in_specs=[h_spec, w_spec(2)],
        out_specs=pl.BlockSpec((nh, tm // MOBA_BLOCK, dh + SUM_ROWS, MOBA_BLOCK),
                               lambda n, m: (0, m, 0, 0)),
        scratch_shapes=scratch, compiler_params=params, name="v_proj",
    )(h, w_in)
    return h, qaug, kaug, vt


def _sgu_kernel(h_ref, wu_ref, wv_ref, g_ref, ws_ref, bt_ref, o_ref,
                wubf_ref, wvbf_ref, wsbf_ref):
    @pl.when(pl.program_id(0) == 0)
    def _():
        wubf_ref[...] = wu_ref[...].astype(BF16)
        wvbf_ref[...] = wv_ref[...].astype(BF16)
        row = lax.broadcasted_iota(jnp.int32, (SGU_CHUNK, SGU_CHUNK), 0)
        col = lax.broadcasted_iota(jnp.int32, (SGU_CHUNK, SGU_CHUNK), 1)
        for g in range(SGU_GROUPS):
            wsbf_ref[g] = jnp.where(col <= row, ws_ref[g], 0.0).astype(BF16)

    def proj(r):
        rows = slice(r, r + SUB_ROWS)
        return (rows,
                jnp.dot(h_ref[rows, :], wubf_ref[...], preferred_element_type=F32),
                jnp.dot(h_ref[rows, :], wvbf_ref[...], preferred_element_type=F32))

    def mix(rows, u_acc, v_acc):
        u = jax.nn.gelu(u_acc)
        v = jax.nn.gelu(v_acc)
        ms = jnp.mean(v * v, axis=-1, keepdims=True)
        vn = (v * lax.rsqrt(ms + EPS) * g_ref[...]).astype(BF16)
        chunks = [slice(c, c + SGU_CHUNK) for c in range(0, SUB_ROWS, SGU_CHUNK)]
        for g in range(SGU_GROUPS):
            cols = slice(g * SGU_CHUNK, (g + 1) * SGU_CHUNK)
            v_all = jnp.concatenate([vn[c, cols] for c in chunks], axis=1)
            mixed = jnp.dot(wsbf_ref[g], v_all, preferred_element_type=F32) + bt_ref[:, g:g + 1]
            for c in chunks:
                o_ref[rows.start + c.start:rows.start + c.stop, cols] = (
                    u[c, cols] * mixed[:, c]).astype(o_ref.dtype)

    n_rows = h_ref.shape[0]
    cur = proj(0)
    for r in range(0, n_rows, SUB_ROWS):
        nxt = proj(r + SUB_ROWS) if r + SUB_ROWS < n_rows else None
        mix(*cur)
        cur = nxt


def _sgu_branch(h, w_in, g_sgu, w_sgu, b_sgu_t, layer):
    s = h.shape[0]
    off = COL_UV // TN
    once = pl.Buffered(1)
    return pl.pallas_call(
        _sgu_kernel,
        out_shape=jax.ShapeDtypeStruct((s, SGU_WIDTH), BF16),
        grid=(s // TM,),
        in_specs=[pl.BlockSpec((TM, D_MODEL), lambda m: (m, 0)),
                  pl.BlockSpec((None, D_MODEL, TN), lambda m: (layer, 0, off),
                               pipeline_mode=once),
                  pl.BlockSpec((None, D_MODEL, TN), lambda m: (layer, 0, off + 1),
                               pipeline_mode=once),
                  pl.BlockSpec((None, 1, SGU_WIDTH), lambda m: (layer, 0, 0)),
                  pl.BlockSpec((None, SGU_GROUPS, SGU_CHUNK, SGU_CHUNK),
                               lambda m: (layer, 0, 0, 0)),
                  pl.BlockSpec((None, SGU_CHUNK, SGU_GROUPS), lambda m: (layer, 0, 0))],
        out_specs=pl.BlockSpec((TM, SGU_WIDTH), lambda m: (m, 0)),
        scratch_shapes=[pltpu.VMEM((D_MODEL, TN), BF16), pltpu.VMEM((D_MODEL, TN), BF16),
                        pltpu.VMEM((SGU_GROUPS, SGU_CHUNK, SGU_CHUNK), BF16)],
        compiler_params=_params("arbitrary"),
        name="sgu_branch",
    )(h, w_in, w_in, g_sgu, w_sgu, b_sgu_t)


def _memkv_kernel(mem_ref, gmem_ref, w_ref, gk_ref, o_ref):
    n = pl.program_id(0)
    x = mem_ref[...]
    ms = jnp.mean(x * x, axis=-1, keepdims=True)
    hm = (x * lax.rsqrt(ms + EPS) * gmem_ref[...]).astype(BF16)
    acc = jnp.dot(hm, w_ref[...].astype(BF16), preferred_element_type=F32)

    @pl.when(n == 0)
    def _():
        o_ref[...] = _group_rmsnorm(acc, gk_ref[...], HEAD_DIM_M).astype(o_ref.dtype)

    @pl.when(n == 1)
    def _():
        o_ref[...] = acc.astype(o_ref.dtype)


def _memkv_proj(mem, g_mem, w_mem_kv, gk_m, layer):
    return pl.pallas_call(
        _memkv_kernel,
        out_shape=jax.ShapeDtypeStruct((2, N_MEM, WIDTH_M), BF16),
        grid=(2,),
        in_specs=[pl.BlockSpec((N_MEM, D_MODEL), lambda n: (0, 0)),
                  pl.BlockSpec((None, 1, D_MODEL), lambda n: (layer, 0, 0)),
                  pl.BlockSpec((None, D_MODEL, WIDTH_M), lambda n: (layer, 0, n)),
                  pl.BlockSpec((None, 1, HEAD_DIM_M), lambda n: (layer, 0, 0))],
        out_specs=pl.BlockSpec((None, N_MEM, WIDTH_M), lambda n: (n, 0, 0)),
        compiler_params=_params("arbitrary"),
        name="memkv_proj",
    )(mem, g_mem, w_mem_kv, gk_m)


def _qm_kernel(h_ref, w_ref, gq_ref, km_ref, vm_ref, o_ref, wbf_ref):
    _cast_weight_once(w_ref, wbf_ref)

    def emit(rows, acc):
        heads = [slice(hh * HEAD_DIM_M, (hh + 1) * HEAD_DIM_M) for hh in range(N_HEADS_M)]
        scores = []
        for cols in heads:
            qh = acc[:, cols]
            ms = jnp.mean(qh * qh, axis=-1, keepdims=True)
            qn = (qh * lax.rsqrt(ms + EPS) * gq_ref[...]).astype(BF16)
            scores.append(lax.dot_general(qn, km_ref[:, cols], (((1,), (1,)), ((), ())),
                                          preferred_element_type=F32) * HEAD_DIM_M ** -0.5)
        weights = []
        for sc in scores:
            mx = jnp.max(sc, axis=-1, keepdims=True)
            e = jnp.exp(sc - mx)
            weights.append((e / jnp.sum(e, axis=-1, keepdims=True)).astype(BF16))
        for cols, p in zip(heads, weights):
            o_ref[rows, cols] = jnp.dot(
                p, vm_ref[:, cols], preferred_element_type=F32).astype(o_ref.dtype)

    _sub_dots(h_ref, wbf_ref[...], emit)


def _qm_memattn(h, w_in, gq_m, kvm, layer):
    s = h.shape[0]
    off = COL_QM // TN
    tm = TM_BIG
    return pl.pallas_call(
        _qm_kernel,
        out_shape=jax.ShapeDtypeStruct((s, WIDTH_M), BF16),
        grid=(1, s // tm),
        in_specs=[pl.BlockSpec((tm, D_MODEL), lambda n, m: (m, 0)),
                  pl.BlockSpec((None, D_MODEL, TN), lambda n, m: (layer, 0, off)),
                  pl.BlockSpec((None, 1, HEAD_DIM_M), lambda n, m: (layer, 0, 0)),
                  pl.BlockSpec((None, N_MEM, WIDTH_M), lambda n, m: (0, 0, 0)),
                  pl.BlockSpec((None, N_MEM, WIDTH_M), lambda n, m: (1, 0, 0))],
        out_specs=pl.BlockSpec((tm, WIDTH_M), lambda n, m: (m, 0)),
        scratch_shapes=[pltpu.VMEM((D_MODEL, TN), BF16)],
        compiler_params=_params("arbitrary", "arbitrary"),
        name="qm_memattn",
    )(h, w_in, gq_m, kvm, kvm)


def _gates_kernel(h_ref, w_ref, o_ref, wbf_ref):
    _cast_weight_once(w_ref, wbf_ref)
    def emit(rows, acc):
        o_ref[rows, :] = jax.nn.sigmoid(acc).astype(o_ref.dtype)

    _sub_dots(h_ref, wbf_ref[...], emit)


def _gates_proj(h, w_in, layer):
    s = h.shape[0]
    off = COL_GATES // TN
    n_tiles = 3 * D_MODEL // TN
    return pl.pallas_call(
        _gates_kernel,
        out_shape=jax.ShapeDtypeStruct((s, 3 * D_MODEL), BF16),
        grid=(n_tiles, s // TM_BIG),
        in_specs=[pl.BlockSpec((TM_BIG, D_MODEL), lambda n, m: (m, 0)),
                  pl.BlockSpec((None, D_MODEL, TN), lambda n, m: (layer, 0, off + n))],
        out_specs=pl.BlockSpec((TM_BIG, TN), lambda n, m: (m, n)),
        scratch_shapes=[pltpu.VMEM((D_MODEL, TN), BF16)],
        compiler_params=_params("arbitrary", "arbitrary"),
        name="gates_proj",
    )(h, w_in)


MOBA_HALF = 2
MOBA_HEADS = 2
MOBA_QBLOCKS = 2
M_INIT = -3e38


def _moba_block(i, q_aug, q_next, kaug_ref, vt_ref, put_out, s_ref, s2_ref, snext_ref,
                may_start_group):
    blk, dh = MOBA_BLOCK, HEAD_DIM_A
    heads = range(MOBA_HEADS)
    half = MOBA_HALF
    half_rows = half * blk
    key_minus_qry = (lax.broadcasted_iota(jnp.int32, (half_rows, blk), 0)
                     - lax.broadcasted_iota(jnp.int32, (half_rows, blk), 1))

    def score(hh, first_blk):
        rows = pl.ds(pl.multiple_of(first_blk * blk, half_rows), half_rows)
        return jnp.dot(kaug_ref[hh, rows, :], q_aug(hh), preferred_element_type=F32)

    def attend(hh, st, s, first_blk, causal):
        m, acc = st
        if causal:
            s = jnp.where(key_minus_qry <= (i - first_blk) * blk, s, NEG_INF)
        m_new = jnp.maximum(m, jnp.max(s, axis=0, keepdims=True))
        alpha = jnp.exp(m - m_new)
        p = jnp.exp(s - m_new)
        v_t = jnp.concatenate([vt_ref[hh, first_blk + g] for g in range(half)], axis=1)
        acc = alpha * acc + jnp.dot(v_t, p.astype(BF16), preferred_element_type=F32)
        return m_new, acc

    def trip(t, carry, last):
        first = t * (2 * half)
        if last:
            for hh in heads:
                s2_ref[hh] = score(hh, first + half)
            for hh in heads:
                snext_ref[hh] = jnp.dot(kaug_ref[hh, 0:half_rows, :], q_next(hh),
                                        preferred_element_type=F32)
            carry = tuple(attend(hh, carry[hh], s_ref[hh], first, True) for hh in heads)

            def second_half(c):
                return tuple(attend(hh, c[hh], s2_ref[hh], first + half, True) for hh in heads)

            return lax.cond(i >= first + half, second_half, lambda c: c, carry)
        s_second = [score(hh, first + half) for hh in heads]
        carry = [attend(hh, carry[hh], s_ref[hh], first, False) for hh in heads]
        for hh in heads:
            s_ref[hh] = score(hh, first + 2 * half)
        carry = [attend(hh, carry[hh], s_second[hh], first + half, False) for hh in heads]
        return tuple(carry)

    def take_prefetched():
        for hh in heads:
            s_ref[hh] = snext_ref[hh]

    if may_start_group:
        @pl.when(i == 0)
        def _():
            for hh in heads:
                s_ref[hh] = score(hh, 0)

        pl.when(i > 0)(take_prefetched)
    else:
        take_prefetched()

    state = tuple((jnp.full((1, blk), M_INIT, F32), jnp.zeros((dh + SUM_ROWS, blk), F32))
                  for _ in heads)
    last_trip = i // (2 * half)
    state = lax.fori_loop(0, last_trip // 2,
                          lambda u, c: trip(2 * u + 1, trip(2 * u, c, False), False), state)
    state = lax.fori_loop(0, last_trip % 2, lambda _, c: trip(last_trip - 1, c, False), state)
    state = trip(last_trip, state, True)
    for hh in heads:
        _, acc = state[hh]
        put_out(hh, (acc[0:dh] / acc[dh:dh + 1]).T)


def _moba_kernel(qaug_ref, qnext_ref, kaug_ref, vt_ref, o_ref, s_ref, s2_ref, snext_ref):
    blk, dh = MOBA_BLOCK, HEAD_DIM_A
    for sub in range(MOBA_QBLOCKS):
        i = pl.program_id(1) * MOBA_QBLOCKS + sub
        lanes = slice(sub * blk, (sub + 1) * blk)
        after = slice((sub + 1) * blk, (sub + 2) * blk)

        def q_aug(hh, lanes=lanes):
            return qaug_ref[hh, :, lanes]

        def q_next(hh, after=after, in_step=sub + 1 < MOBA_QBLOCKS):
            return qaug_ref[hh, :, after] if in_step else qnext_ref[hh]

        def put_out(hh, x, rows=lanes):
            o_ref[rows, hh * dh:(hh + 1) * dh] = x.astype(o_ref.dtype)

        _moba_block(i, q_aug, q_next, kaug_ref, vt_ref, put_out, s_ref, s2_ref, snext_ref,
                    may_start_group=(sub == 0))


def _moba(qaug, kaug, vt):
    s = kaug.shape[1]
    nh, nq = MOBA_HEADS, MOBA_QBLOCKS
    steps = N_BLOCKS // nq
    return pl.pallas_call(
        _moba_kernel,
        out_shape=jax.ShapeDtypeStruct((s, WIDTH_A), BF16),
        grid=(N_HEADS_A // nh, steps),
        in_specs=[pl.BlockSpec((nh, 2 * HEAD_DIM_A, nq * MOBA_BLOCK), lambda h, i: (h, 0, i)),
                  pl.BlockSpec((nh, 2 * HEAD_DIM_A, MOBA_BLOCK),
                               lambda h, i: (h, 0, jnp.minimum(nq * (i + 1), N_BLOCKS - 1))),
                  pl.BlockSpec((nh, s, 2 * HEAD_DIM_A), lambda h, i: (h, 0, 0)),
                  pl.BlockSpec((nh, N_BLOCKS, HEAD_DIM_A + SUM_ROWS, MOBA_BLOCK),
                               lambda h, i: (h, 0, 0, 0))],
        out_specs=pl.BlockSpec((nq * MOBA_BLOCK, nh * HEAD_DIM_A), lambda h, i: (i, h)),
        scratch_shapes=[pltpu.VMEM((nh, MOBA_HALF * MOBA_BLOCK, MOBA_BLOCK), F32)] * 3,
        compiler_params=_params("arbitrary", "arbitrary"),
        name="moba_attention",
    )(qaug, qaug, kaug, vt)


MERGE_TN = 1024
MERGE_TM = 512


def _merge_kernel(ya_ref, yb_ref, ym_ref, ga_ref, gb_ref, gm_ref, w_ref, o_ref, wbf_ref):
    _cast_weight_once(w_ref, wbf_ref)
    for r in range(0, ya_ref.shape[0], SUB_ROWS):
        rows = slice(r, r + SUB_ROWS)
        merged = ga_ref[rows, :] * jnp.dot(ya_ref[rows, :], wbf_ref[0],
                                           preferred_element_type=F32)
        merged += gb_ref[rows, :] * jnp.dot(yb_ref[rows, :], wbf_ref[1],
                                            preferred_element_type=F32)
        merged += gm_ref[rows, :] * jnp.dot(ym_ref[rows, :], wbf_ref[2],
                                            preferred_element_type=F32)
        o_ref[rows, :] = merged.astype(o_ref.dtype)


def _merge(ya, yb, ym, gates, w_branch, layer):
    s = ya.shape[0]
    tn = MERGE_TN
    nt = D_MODEL // tn
    tm = MERGE_TM
    act = pl.BlockSpec((tm, WIDTH_A), lambda n, m: (m, 0))
    return pl.pallas_call(
        _merge_kernel,
        out_shape=jax.ShapeDtypeStruct((s, D_MODEL), BF16),
        grid=(nt, s // tm),
        in_specs=[act, act, act,
                  pl.BlockSpec((tm, tn), lambda n, m: (m, n)),
                  pl.BlockSpec((tm, tn), lambda n, m: (m, nt + n)),
                  pl.BlockSpec((tm, tn), lambda n, m: (m, 2 * nt + n)),
                  pl.BlockSpec((None, 3, WIDTH_A, tn), lambda n, m: (layer, 0, 0, n))],
        out_specs=pl.BlockSpec((tm, tn), lambda n, m: (m, n)),
        scratch_shapes=[pltpu.VMEM((3, WIDTH_A, tn), BF16)],
        compiler_params=_params("arbitrary", "arbitrary"),
        name="branch_merge",
    )(ya, yb, ym, gates, gates, gates, w_branch)


def _resid_proj_kernel(a_ref, w_ref, x_ref, o_ref, wbf_ref):
    _cast_weight_once(w_ref, wbf_ref)
    def emit(rows, acc):
        o_ref[rows, :] = x_ref[rows, :] + acc

    _sub_dots(a_ref, wbf_ref[...], emit)


def _resid_proj(a, w, x, layer, tm, tn, name):
    s, k = a.shape
    n_out = w.shape[2]
    return pl.pallas_call(
        _resid_proj_kernel,
        out_shape=jax.ShapeDtypeStruct((s, n_out), F32),
        grid=(n_out // tn, s // tm),
        in_specs=[pl.BlockSpec((tm, k), lambda n, m: (m, 0)),
                  pl.BlockSpec((None, k, tn), lambda n, m: (layer, 0, n)),
                  pl.BlockSpec((tm, tn), lambda n, m: (m, n))],
        out_specs=pl.BlockSpec((tm, tn), lambda n, m: (m, n)),
        scratch_shapes=[pltpu.VMEM((k, tn), BF16)],
        compiler_params=_params("arbitrary", "arbitrary"),
        name=name,
    )(a, w, x)


OUT_TM = 512


def _out_proj_norm_kernel(a_ref, w_ref, x_ref, g_ref, o_ref, h_ref, wbf_ref):
    @pl.when(pl.program_id(0) == 0)
    def _():
        wbf_ref[...] = w_ref[...].astype(BF16)

    def emit(rows, acc):
        x = x_ref[rows, :] + acc
        o_ref[rows, :] = x
        ms = jnp.mean(x * x, axis=-1, keepdims=True)
        h_ref[rows, :] = (x * lax.rsqrt(ms + EPS) * g_ref[...]).astype(h_ref.dtype)

    _sub_dots(a_ref, wbf_ref[...], emit)


def _out_proj_norm(a, w, x, g_all, layer):
    s, k = a.shape
    tm = OUT_TM
    return pl.pallas_call(
        _out_proj_norm_kernel,
        out_shape=(jax.ShapeDtypeStruct((s, D_MODEL), F32),
                   jax.ShapeDtypeStruct((s, D_MODEL), BF16)),
        grid=(s // tm,),
        in_specs=[pl.BlockSpec((tm, k), lambda m: (m, 0)),
                  pl.BlockSpec((None, k, D_MODEL), lambda m: (layer, 0, 0),
                               pipeline_mode=pl.Buffered(1)),
                  pl.BlockSpec((tm, D_MODEL), lambda m: (m, 0)),
                  pl.BlockSpec((None, 1, D_MODEL), lambda m: (layer, 0, 0))],
        out_specs=(pl.BlockSpec((tm, D_MODEL), lambda m: (m, 0)),
                   pl.BlockSpec((tm, D_MODEL), lambda m: (m, 0))),
        scratch_shapes=[pltpu.VMEM((k, D_MODEL), BF16)],
        compiler_params=_params("arbitrary"),
        name="out_proj_norm",
    )(a, w, x, g_all)


FFN_TN = 512
FFN_DOWN_TM = 512
FFN_DOWN_TN = 512


def _ffn_up_kernel(h_ref, wg_ref, wu_ref, o_ref, wgbf_ref, wubf_ref):
    @pl.when(pl.program_id(1) == 0)
    def _():
        wgbf_ref[...] = wg_ref[...].astype(BF16)
        wubf_ref[...] = wu_ref[...].astype(BF16)
    for r in range(0, h_ref.shape[0], SUB_ROWS):
        rows = slice(r, r + SUB_ROWS)
        gt = jnp.dot(h_ref[rows, :], wgbf_ref[...], preferred_element_type=F32)
        up = jnp.dot(h_ref[rows, :], wubf_ref[...], preferred_element_type=F32)
        o_ref[rows, :] = (jax.nn.silu(gt) * up).astype(o_ref.dtype)


def _ffn_up(h, w_gate_up, layer):
    s = h.shape[0]
    tn = FFN_TN
    nt = D_FF // tn
    tm = TM_BIG
    return pl.pallas_call(
        _ffn_up_kernel,
        out_shape=jax.ShapeDtypeStruct((s, D_FF), BF16),
        grid=(nt, s // tm),
        in_specs=[pl.BlockSpec((tm, D_MODEL), lambda n, m: (m, 0)),
                  pl.BlockSpec((None, D_MODEL, tn), lambda n, m: (layer, 0, n)),
                  pl.BlockSpec((None, D_MODEL, tn), lambda n, m: (layer, 0, nt + n))],
        out_specs=pl.BlockSpec((tm, tn), lambda n, m: (m, n)),
        scratch_shapes=[pltpu.VMEM((D_MODEL, tn), BF16), pltpu.VMEM((D_MODEL, tn), BF16)],
        compiler_params=_params("arbitrary", "arbitrary"),
        name="ffn_up",
    )(h, w_gate_up, w_gate_up)


def kernel(x, mem, g_mix, w_in, gq_a, gk_a, g_sgu, w_sgu, b_sgu, gq_m, gk_m, g_mem,
           w_mem_kv, w_branch, w_out, g_ffn, w_gate_up, w_down):
    b, s, d = x.shape
    assert (b, s, d) == (1, SEQ, D_MODEL) and mem.shape == (1, N_MEM, D_MODEL)
    x2 = x.reshape(s, d)
    mem2 = mem.reshape(N_MEM, d)

    def row(p):
        return p.reshape(DEPTH, 1, p.shape[-1])

    g_mix3, g_ffn3, g_mem3, g_sgu3 = row(g_mix), row(g_ffn), row(g_mem), row(g_sgu)
    gq_a3, gk_a3, gq_m3, gk_m3 = row(gq_a), row(gk_a), row(gq_m), row(gk_m)
    b_sgu_t = jnp.swapaxes(b_sgu, 1, 2)
    slopes = 2.0 ** (-8.0 * jnp.arange(1, N_HEADS_A + 1, dtype=F32) / N_HEADS_A)
    slopes = jnp.broadcast_to(slopes[:, None, None], (N_HEADS_A, 1, MOBA_BLOCK))

    for layer in range(DEPTH):
        h, qaug, kaug, vt = _qkv_proj(x2, g_mix3, w_in, gq_a3, gk_a3, slopes, layer)
        yb = _sgu_branch(h, w_in, g_sgu3, w_sgu, b_sgu_t, layer)
        kvm = _memkv_proj(mem2, g_mem3, w_mem_kv, gk_m3, layer)
        ym = _qm_memattn(h, w_in, gq_m3, kvm, layer)
        gates = _gates_proj(h, w_in, layer)
        ya = _moba(qaug, kaug, vt)
        merged = _merge(ya, yb, ym, gates, w_branch, layer)
        x2, hf = _out_proj_norm(merged, w_out, x2, g_ffn3, layer)
        act = _ffn_up(hf, w_gate_up, layer)
        x2 = _resid_proj(act, w_down, x2, layer, FFN_DOWN_TM, FFN_DOWN_TN, "ffn_down")
    return x2.reshape(b, s, d)
```

```python
import jax
import jax.numpy as jnp
from jax import lax
from jax.experimental import pallas as pl
from jax.experimental.pallas import tpu as pltpu

F32 = jnp.float32
BF16 = jnp.bfloat16

D_MODEL = 2048
SEQ = 8192
DEPTH = 2
N_HEADS_A = 8
HEAD_DIM_A = 128
WIDTH_A = N_HEADS_A * HEAD_DIM_A
MOBA_BLOCK = 256
MOBA_TOPK = 3
N_BLOCKS = SEQ // MOBA_BLOCK
SGU_WIDTH = 1024
SGU_GROUPS = 8
SGU_CHUNK = 128
N_MEM = 256
N_HEADS_M = 4
HEAD_DIM_M = 256
WIDTH_M = N_HEADS_M * HEAD_DIM_M
D_FF = 5632
NEG_INF = -1e30
EPS = 1e-6

COL_UV = 3 * WIDTH_A
COL_QM = COL_UV + 2 * SGU_WIDTH
COL_GATES = COL_QM + WIDTH_M

VMEM_LIMIT_BYTES = 60 * 1024 * 1024

TM = 1024
TM_BIG = 2048
TN = 1024
SUB_ROWS = 256


def _params(*semantics):
    return pltpu.CompilerParams(dimension_semantics=semantics,
                                vmem_limit_bytes=VMEM_LIMIT_BYTES)


def _cast_weight_once(w_ref, wbf_ref):
    @pl.when(pl.program_id(1) == 0)
    def _():
        wbf_ref[...] = w_ref[...].astype(BF16)


def _group_rmsnorm(a, g, width):
    outs = []
    for s in range(0, a.shape[1], width):
        blk = a[:, s:s + width]
        ms = jnp.mean(blk * blk, axis=-1, keepdims=True)
        outs.append(blk * lax.rsqrt(ms + EPS) * g)
    return jnp.concatenate(outs, axis=1)


def _sub_dots(lhs_ref, rhs, emit, lhs_rows=None):
    def sub_dot(r):
        rows = slice(r, r + SUB_ROWS)
        lhs = lhs_rows(rows) if lhs_rows is not None else lhs_ref[rows, :]
        return rows, jnp.dot(lhs, rhs, preferred_element_type=F32)

    n_rows = lhs_ref.shape[0]
    cur = sub_dot(0)
    for r in range(0, n_rows, SUB_ROWS):
        nxt = sub_dot(r + SUB_ROWS) if r + SUB_ROWS < n_rows else None
        emit(*cur)
        cur = nxt


COL_KEY_OFF = N_BLOCKS
COL_BLK_OFF = N_BLOCKS + 1
COL_ONES = N_BLOCKS + 2
SUM_ROWS = 16
GATE_TAKEN = -3e38


def _head_norm(a, g):
    ms = jnp.mean(a * a, axis=-1, keepdims=True)
    return a * lax.rsqrt(ms + EPS) * g


def _k_kernel(x_ref, g_ref, w_ref, gk_ref, slope_ref, kaug_ref, kmean_ref, h_ref, wbf_ref):
    m = pl.program_id(1)

    def norm_rows(rows):
        x = x_ref[rows, :]
        ms = jnp.mean(x * x, axis=-1, keepdims=True)
        h = (x * lax.rsqrt(ms + EPS) * g_ref[...]).astype(BF16)
        h_ref[rows, :] = h
        return h

    _cast_weight_once(w_ref, wbf_ref)
    assert SUB_ROWS == MOBA_BLOCK
    blk, dh = MOBA_BLOCK, HEAD_DIM_A
    lane = lax.broadcasted_iota(jnp.int32, (blk, dh), 1)
    lane_row = lax.broadcasted_iota(jnp.int32, (1, dh), 1)
    key_off = lax.broadcasted_iota(jnp.int32, (blk, dh), 0).astype(F32)

    def emit(rows, acc):
        b_local = rows.start // blk
        b = m * (h_ref.shape[0] // blk) + b_local
        blk_start = jnp.full((1, dh), b * blk, jnp.int32).astype(F32)
        onehot = jnp.where(lane == b, 1.0, jnp.where(lane == COL_ONES, 1.0, 0.0))
        for hh in range(N_HEADS_A):
            cols = slice(hh * dh, (hh + 1) * dh)
            kn = _head_norm(acc[:, cols], gk_ref[...])
            kmean_ref[b_local:b_local + 1, cols] = jnp.mean(kn, axis=0, keepdims=True)
            sl = slope_ref[hh][:, :dh]
            extra = (onehot + jnp.where(lane == COL_KEY_OFF, sl * key_off, 0.0)
                     + jnp.where(lane_row == COL_BLK_OFF, sl * blk_start, 0.0))
            kaug_ref[hh, rows, 0:dh] = kn.astype(BF16)
            kaug_ref[hh, rows, dh:2 * dh] = extra.astype(BF16)

    _sub_dots(x_ref, wbf_ref[...], emit, lhs_rows=norm_rows)


def _q_kernel(h_ref, w_ref, gq_ref, kmean_ref, slope_ref, qaug_ref, wbf_ref):
    m = pl.program_id(1)
    _cast_weight_once(w_ref, wbf_ref)
    assert SUB_ROWS == MOBA_BLOCK
    blk, dh = MOBA_BLOCK, HEAD_DIM_A
    blk_id = lax.broadcasted_iota(jnp.int32, (N_BLOCKS, blk), 0)
    blk_id_f = blk_id.astype(F32)
    rest = lax.broadcasted_iota(jnp.int32, (dh - N_BLOCKS, blk), 0) + N_BLOCKS

    def emit(rows, acc):
        i = m * (h_ref.shape[0] // blk) + rows.start // blk
        past = blk_id < i
        q_start = jnp.full((1, blk), i * blk, jnp.int32).astype(F32)
        for hh in range(N_HEADS_A):
            cols = slice(hh * dh, (hh + 1) * dh)
            q = _head_norm(acc[:, cols], gq_ref[...]) * dh ** -0.5
            q_t = q.T
            gate = jnp.dot(kmean_ref[:, cols].astype(BF16), q_t.astype(BF16),
                           preferred_element_type=F32)
            gate = jnp.where(past, gate, NEG_INF)
            pen = jnp.full((N_BLOCKS, blk), NEG_INF, F32)
            for _ in range(MOBA_TOPK):
                mx = jnp.max(gate, axis=0, keepdims=True)
                first = jnp.min(jnp.where(gate == mx, blk_id_f, float(N_BLOCKS)),
                                axis=0, keepdims=True)
                hit = blk_id_f == first
                pen = jnp.where(jnp.logical_and(hit, mx > 0.5 * NEG_INF), 0.0, pen)
                gate = jnp.where(hit, GATE_TAKEN, gate)
            pen = jnp.where(blk_id == i, 0.0, pen)
            extra_q = jnp.where(rest == COL_ONES, -slope_ref[hh] * q_start,
                                jnp.where(rest < COL_ONES, 1.0, 0.0))
            qaug_ref[hh, :, rows] = jnp.concatenate([q_t, pen, extra_q], axis=0).astype(BF16)

    _sub_dots(h_ref, wbf_ref[...], emit)


def _v_kernel(h_ref, w_ref, vt_ref, wbf_ref):
    _cast_weight_once(w_ref, wbf_ref)
    assert SUB_ROWS == MOBA_BLOCK
    blk, dh = MOBA_BLOCK, HEAD_DIM_A

    def emit(rows, acc):
        b_local = rows.start // blk
        for hh in range(N_HEADS_A):
            vt_ref[hh, b_local, 0:dh, :] = acc[:, hh * dh:(hh + 1) * dh].T.astype(BF16)
            vt_ref[hh, b_local, dh:dh + SUM_ROWS, :] = jnp.ones((SUM_ROWS, blk), BF16)

    _sub_dots(h_ref, wbf_ref[...], emit)


def _qkv_proj(x, g_mix, w_in, gq_a, gk_a, slopes, layer):
    s = x.shape[0]
    tm, dh, nh = TM, HEAD_DIM_A, N_HEADS_A
    grid = (1, s // tm)
    h_spec = pl.BlockSpec((tm, D_MODEL), lambda n, m: (m, 0))
    gain = pl.BlockSpec((None, 1, dh), lambda n, m: (layer, 0, 0))
    slope = pl.BlockSpec((nh, 1, MOBA_BLOCK), lambda n, m: (0, 0, 0))
    scratch = [pltpu.VMEM((D_MODEL, TN), BF16)]
    params = _params("arbitrary", "arbitrary")

    def w_spec(col):
        return pl.BlockSpec((None, D_MODEL, TN), lambda n, m: (layer, 0, col))

    kaug, kmeans, h = pl.pallas_call(
        _k_kernel,
        out_shape=(jax.ShapeDtypeStruct((nh, s, 2 * dh), BF16),
                   jax.ShapeDtypeStruct((s // tm, tm // MOBA_BLOCK, WIDTH_A), F32),
                   jax.ShapeDtypeStruct((s, D_MODEL), BF16)),
        grid=grid,
        in_specs=[h_spec, pl.BlockSpec((None, 1, D_MODEL), lambda n, m: (layer, 0, 0)),
                  w_spec(1), gain, slope],
        out_specs=(pl.BlockSpec((nh, tm, 2 * dh), lambda n, m: (0, m, 0)),
                   pl.BlockSpec((None, tm // MOBA_BLOCK, WIDTH_A), lambda n, m: (m, 0, 0)),
                   h_spec),
        scratch_shapes=scratch, compiler_params=params, name="k_proj",
    )(x, g_mix, w_in, gk_a, slopes)
    kmean = kmeans.reshape(N_BLOCKS, WIDTH_A)
    qaug = pl.pallas_call(
        _q_kernel,
        out_shape=jax.ShapeDtypeStruct((nh, 2 * dh, s), BF16),
        grid=grid,
        in_specs=[h_spec, w_spec(0), gain,
                  pl.BlockSpec((N_BLOCKS, WIDTH_A), lambda n, m: (0, 0)), slope],
        out_specs=pl.BlockSpec((nh, 2 * dh, tm), lambda n, m: (0, 0, m)),
        scratch_shapes=scratch, compiler_params=params, name="q_proj",
    )(h, w_in, gq_a, kmean, slopes)
    vt = pl.pallas_call(
        _v_kernel,
        out_shape=jax.ShapeDtypeStruct((nh, N_BLOCKS, dh + SUM_ROWS, MOBA_BLOCK), BF16),
        grid=grid,
        in_specs=[h_spec, w_spec(2)],
        out_specs=pl.BlockSpec((nh, tm // MOBA_BLOCK, dh + SUM_ROWS, MOBA_BLOCK),
                               lambda n, m: (0, m, 0, 0)),
        scratch_shapes=scratch, compiler_params=params, name="v_proj",
    )(h, w_in)
    return h, qaug, kaug, vt


def _sgu_kernel(h_ref, wu_ref, wv_ref, g_ref, ws_ref, bt_ref, o_ref,
                wubf_ref, wvbf_ref, wsbf_ref):
    @pl.when(pl.program_id(0) == 0)
    def _():
        wubf_ref[...] = wu_ref[...].astype(BF16)
        wvbf_ref[...] = wv_ref[...].astype(BF16)
        row = lax.broadcasted_iota(jnp.int32, (SGU_CHUNK, SGU_CHUNK), 0)
        col = lax.broadcasted_iota(jnp.int32, (SGU_CHUNK, SGU_CHUNK), 1)
        for g in range(SGU_GROUPS):
            wsbf_ref[g] = jnp.where(col <= row, ws_ref[g], 0.0).astype(BF16)

    def proj(r):
        rows = slice(r, r + SUB_ROWS)
        return (rows,
                jnp.dot(h_ref[rows, :], wubf_ref[...], preferred_element_type=F32),
                jnp.dot(h_ref[rows, :], wvbf_ref[...], preferred_element_type=F32))

    def mix(rows, u_acc, v_acc):
        u = jax.nn.gelu(u_acc)
        v = jax.nn.gelu(v_acc)
        ms = jnp.mean(v * v, axis=-1, keepdims=True)
        vn = (v * lax.rsqrt(ms + EPS) * g_ref[...]).astype(BF16)
        chunks = [slice(c, c + SGU_CHUNK) for c in range(0, SUB_ROWS, SGU_CHUNK)]
        for g in range(SGU_GROUPS):
            cols = slice(g * SGU_CHUNK, (g + 1) * SGU_CHUNK)
            v_all = jnp.concatenate([vn[c, cols] for c in chunks], axis=1)
            mixed = jnp.dot(wsbf_ref[g], v_all, preferred_element_type=F32) + bt_ref[:, g:g + 1]
            for c in chunks:
                o_ref[rows.start + c.start:rows.start + c.stop, cols] = (
                    u[c, cols] * mixed[:, c]).astype(o_ref.dtype)

    n_rows = h_ref.shape[0]
    cur = proj(0)
    for r in range(0, n_rows, SUB_ROWS):
        nxt = proj(r + SUB_ROWS) if r + SUB_ROWS < n_rows else None
        mix(*cur)
        cur = nxt


def _sgu_branch(h, w_in, g_sgu, w_sgu, b_sgu_t, layer):
    s = h.shape[0]
    off = COL_UV // TN
    once = pl.Buffered(1)
    return pl.pallas_call(
        _sgu_kernel,
        out_shape=jax.ShapeDtypeStruct((s, SGU_WIDTH), BF16),
        grid=(s // TM,),
        in_specs=[pl.BlockSpec((TM, D_MODEL), lambda m: (m, 0)),
                  pl.BlockSpec((None, D_MODEL, TN), lambda m: (layer, 0, off),
                               pipeline_mode=once),
                  pl.BlockSpec((None, D_MODEL, TN), lambda m: (layer, 0, off + 1),
                               pipeline_mode=once),
                  pl.BlockSpec((None, 1, SGU_WIDTH), lambda m: (layer, 0, 0)),
                  pl.BlockSpec((None, SGU_GROUPS, SGU_CHUNK, SGU_CHUNK),
                               lambda m: (layer, 0, 0, 0)),
                  pl.BlockSpec((None, SGU_CHUNK, SGU_GROUPS), lambda m: (layer, 0, 0))],
        out_specs=pl.BlockSpec((TM, SGU_WIDTH), lambda m: (m, 0)),
        scratch_shapes=[pltpu.VMEM((D_MODEL, TN), BF16), pltpu.VMEM((D_MODEL, TN), BF16),
                        pltpu.VMEM((SGU_GROUPS, SGU_CHUNK, SGU_CHUNK), BF16)],
        compiler_params=_params("arbitrary"),
        name="sgu_branch",
    )(h, w_in, w_in, g_sgu, w_sgu, b_sgu_t)


def _memkv_kernel(mem_ref, gmem_ref, w_ref, gk_ref, o_ref):
    n = pl.program_id(0)
    x = mem_ref[...]
    ms = jnp.mean(x * x, axis=-1, keepdims=True)
    hm = (x * lax.rsqrt(ms + EPS) * gmem_ref[...]).astype(BF16)
    acc = jnp.dot(hm, w_ref[...].astype(BF16), preferred_element_type=F32)

    @pl.when(n == 0)
    def _():
        o_ref[...] = _group_rmsnorm(acc, gk_ref[...], HEAD_DIM_M).astype(o_ref.dtype)

    @pl.when(n == 1)
    def _():
        o_ref[...] = acc.astype(o_ref.dtype)


def _memkv_proj(mem, g_mem, w_mem_kv, gk_m, layer):
    return pl.pallas_call(
        _memkv_kernel,
        out_shape=jax.ShapeDtypeStruct((2, N_MEM, WIDTH_M), BF16),
        grid=(2,),
        in_specs=[pl.BlockSpec((N_MEM, D_MODEL), lambda n: (0, 0)),
                  pl.BlockSpec((None, 1, D_MODEL), lambda n: (layer, 0, 0)),
                  pl.BlockSpec((None, D_MODEL, WIDTH_M), lambda n: (layer, 0, n)),
                  pl.BlockSpec((None, 1, HEAD_DIM_M), lambda n: (layer, 0, 0))],
        out_specs=pl.BlockSpec((None, N_MEM, WIDTH_M), lambda n: (n, 0, 0)),
        compiler_params=_params("arbitrary"),
        name="memkv_proj",
    )(mem, g_mem, w_mem_kv, gk_m)


def _qm_kernel(h_ref, w_ref, gq_ref, km_ref, vm_ref, o_ref, wbf_ref):
    _cast_weight_once(w_ref, wbf_ref)

    def emit(rows, acc):
        heads = [slice(hh * HEAD_DIM_M, (hh + 1) * HEAD_DIM_M) for hh in range(N_HEADS_M)]
        scores = []
        for cols in heads:
            qh = acc[:, cols]
            ms = jnp.mean(qh * qh, axis=-1, keepdims=True)
            qn = (qh * lax.rsqrt(ms + EPS) * gq_ref[...]).astype(BF16)
            scores.append(lax.dot_general(qn, km_ref[:, cols], (((1,), (1,)), ((), ())),
                                          preferred_element_type=F32) * HEAD_DIM_M ** -0.5)
        weights = []
        for sc in scores:
            mx = jnp.max(sc, axis=-1, keepdims=True)
            e = jnp.exp(sc - mx)
            weights.append((e / jnp.sum(e, axis=-1, keepdims=True)).astype(BF16))
        for cols, p in zip(heads, weights):
            o_ref[rows, cols] = jnp.dot(
                p, vm_ref[:, cols], preferred_element_type=F32).astype(o_ref.dtype)

    _sub_dots(h_ref, wbf_ref[...], emit)


def _qm_memattn(h, w_in, gq_m, kvm, layer):
    s = h.shape[0]
    off = COL_QM // TN
    tm = TM_BIG
    return pl.pallas_call(
        _qm_kernel,
        out_shape=jax.ShapeDtypeStruct((s, WIDTH_M), BF16),
        grid=(1, s // tm),
        in_specs=[pl.BlockSpec((tm, D_MODEL), lambda n, m: (m, 0)),
                  pl.BlockSpec((None, D_MODEL, TN), lambda n, m: (layer, 0, off)),
                  pl.BlockSpec((None, 1, HEAD_DIM_M), lambda n, m: (layer, 0, 0)),
                  pl.BlockSpec((None, N_MEM, WIDTH_M), lambda n, m: (0, 0, 0)),
                  pl.BlockSpec((None, N_MEM, WIDTH_M), lambda n, m: (1, 0, 0))],
        out_specs=pl.BlockSpec((tm, WIDTH_M), lambda n, m: (m, 0)),
        scratch_shapes=[pltpu.VMEM((D_MODEL, TN), BF16)],
        compiler_params=_params("arbitrary", "arbitrary"),
        name="qm_memattn",
    )(h, w_in, gq_m, kvm, kvm)


def _gates_kernel(h_ref, w_ref, o_ref, wbf_ref):
    _cast_weight_once(w_ref, wbf_ref)
    def emit(rows, acc):
        o_ref[rows, :] = jax.nn.sigmoid(acc).astype(o_ref.dtype)

    _sub_dots(h_ref, wbf_ref[...], emit)


def _gates_proj(h, w_in, layer):
    s = h.shape[0]
    off = COL_GATES // TN
    n_tiles = 3 * D_MODEL // TN
    return pl.pallas_call(
        _gates_kernel,
        out_shape=jax.ShapeDtypeStruct((s, 3 * D_MODEL), BF16),
        grid=(n_tiles, s // TM_BIG),
        in_specs=[pl.BlockSpec((TM_BIG, D_MODEL), lambda n, m: (m, 0)),
                  pl.BlockSpec((None, D_MODEL, TN), lambda n, m: (layer, 0, off + n))],
        out_specs=pl.BlockSpec((TM_BIG, TN), lambda n, m: (m, n)),
        scratch_shapes=[pltpu.VMEM((D_MODEL, TN), BF16)],
        compiler_params=_params("arbitrary", "arbitrary"),
        name="gates_proj",
    )(h, w_in)


MOBA_HALF = 2
MOBA_HEADS = 2
MOBA_QBLOCKS = 4
M_INIT = -3e38


def _moba_block(i, q_aug, q_next, kaug_ref, vt_ref, put_out, s_ref, s2_ref, snext_ref,
                may_start_group):
    blk, dh = MOBA_BLOCK, HEAD_DIM_A
    heads = range(MOBA_HEADS)
    half = MOBA_HALF
    half_rows = half * blk
    key_minus_qry = (lax.broadcasted_iota(jnp.int32, (half_rows, blk), 0)
                     - lax.broadcasted_iota(jnp.int32, (half_rows, blk), 1))

    def score(hh, first_blk):
        rows = pl.ds(pl.multiple_of(first_blk * blk, half_rows), half_rows)
        return jnp.dot(kaug_ref[hh, rows, :], q_aug(hh), preferred_element_type=F32)

    def attend(hh, st, s, first_blk, causal):
        m, acc = st
        if causal:
            s = jnp.where(key_minus_qry <= (i - first_blk) * blk, s, NEG_INF)
        m_new = jnp.maximum(m, jnp.max(s, axis=0, keepdims=True))
        alpha = jnp.exp(m - m_new)
        p = jnp.exp(s - m_new)
        v_t = jnp.concatenate([vt_ref[hh, first_blk + g] for g in range(half)], axis=1)
        acc = alpha * acc + jnp.dot(v_t, p.astype(BF16), preferred_element_type=F32)
        return m_new, acc

    def trip(t, carry, last):
        first = t * (2 * half)
        if last:
            for hh in heads:
                s2_ref[hh] = score(hh, first + half)
            for hh in heads:
                snext_ref[hh] = jnp.dot(kaug_ref[hh, 0:half_rows, :], q_next(hh),
                                        preferred_element_type=F32)
            carry = tuple(attend(hh, carry[hh], s_ref[hh], first, True) for hh in heads)

            def second_half(c):
                return tuple(attend(hh, c[hh], s2_ref[hh], first + half, True) for hh in heads)

            return lax.cond(i >= first + half, second_half, lambda c: c, carry)
        s_second = [score(hh, first + half) for hh in heads]
        carry = [attend(hh, carry[hh], s_ref[hh], first, False) for hh in heads]
        for hh in heads:
            s_ref[hh] = score(hh, first + 2 * half)
        carry = [attend(hh, carry[hh], s_second[hh], first + half, False) for hh in heads]
        return tuple(carry)

    if may_start_group:
        @pl.when(i == 0)
        def _():
            for hh in heads:
                s_ref[hh] = score(hh, 0)

    state = tuple((jnp.full((1, blk), M_INIT, F32), jnp.zeros((dh + SUM_ROWS, blk), F32))
                  for _ in heads)
    last_trip = i // (2 * half)
    state = lax.fori_loop(0, last_trip // 2,
                          lambda u, c: trip(2 * u + 1, trip(2 * u, c, False), False), state)
    state = lax.fori_loop(0, last_trip % 2, lambda _, c: trip(last_trip - 1, c, False), state)
    state = trip(last_trip, state, True)
    for hh in heads:
        _, acc = state[hh]
        put_out(hh, (acc[0:dh] / acc[dh:dh + 1]).T)


def _moba_kernel(qaug_ref, qnext_ref, kaug_ref, vt_ref, o_ref, s_ref, s2_ref, snext_ref):
    blk, dh = MOBA_BLOCK, HEAD_DIM_A
    assert MOBA_QBLOCKS % 2 == 0
    for sub in range(MOBA_QBLOCKS):
        i = pl.program_id(1) * MOBA_QBLOCKS + sub
        lanes = slice(sub * blk, (sub + 1) * blk)
        after = slice((sub + 1) * blk, (sub + 2) * blk)

        def q_aug(hh, lanes=lanes):
            return qaug_ref[hh, :, lanes]

        def q_next(hh, after=after, in_step=sub + 1 < MOBA_QBLOCKS):
            return qaug_ref[hh, :, after] if in_step else qnext_ref[hh]

        def put_out(hh, x, rows=lanes):
            o_ref[rows, hh * dh:(hh + 1) * dh] = x.astype(o_ref.dtype)

        s_cur, s_nxt = (s_ref, snext_ref) if sub % 2 == 0 else (snext_ref, s_ref)
        _moba_block(i, q_aug, q_next, kaug_ref, vt_ref, put_out, s_cur, s2_ref, s_nxt,
                    may_start_group=(sub == 0))


def _moba(qaug, kaug, vt):
    s = kaug.shape[1]
    nh, nq = MOBA_HEADS, MOBA_QBLOCKS
    steps = N_BLOCKS // nq
    return pl.pallas_call(
        _moba_kernel,
        out_shape=jax.ShapeDtypeStruct((s, WIDTH_A), BF16),
        grid=(N_HEADS_A // nh, steps),
        in_specs=[pl.BlockSpec((nh, 2 * HEAD_DIM_A, nq * MOBA_BLOCK), lambda h, i: (h, 0, i)),
                  pl.BlockSpec((nh, 2 * HEAD_DIM_A, MOBA_BLOCK),
                               lambda h, i: (h, 0, jnp.minimum(nq * (i + 1), N_BLOCKS - 1))),
                  pl.BlockSpec((nh, s, 2 * HEAD_DIM_A), lambda h, i: (h, 0, 0)),
                  pl.BlockSpec((nh, N_BLOCKS, HEAD_DIM_A + SUM_ROWS, MOBA_BLOCK),
                               lambda h, i: (h, 0, 0, 0))],
        out_specs=pl.BlockSpec((nq * MOBA_BLOCK, nh * HEAD_DIM_A), lambda h, i: (i, h)),
        scratch_shapes=[pltpu.VMEM((nh, MOBA_HALF * MOBA_BLOCK, MOBA_BLOCK), F32)] * 3,
        compiler_params=_params("arbitrary", "arbitrary"),
        name="moba_attention",
    )(qaug, qaug, kaug, vt)


MERGE_TN = 1024
MERGE_TM = 512


def _merge_kernel(ya_ref, yb_ref, ym_ref, ga_ref, gb_ref, gm_ref, w_ref, o_ref, wbf_ref):
    _cast_weight_once(w_ref, wbf_ref)
    for r in range(0, ya_ref.shape[0], SUB_ROWS):
        rows = slice(r, r + SUB_ROWS)
        merged = ga_ref[rows, :] * jnp.dot(ya_ref[rows, :], wbf_ref[0],
                                           preferred_element_type=F32)
        merged += gb_ref[rows, :] * jnp.dot(yb_ref[rows, :], wbf_ref[1],
                                            preferred_element_type=F32)
        merged += gm_ref[rows, :] * jnp.dot(ym_ref[rows, :], wbf_ref[2],
                                            preferred_element_type=F32)
        o_ref[rows, :] = merged.astype(o_ref.dtype)


def _merge(ya, yb, ym, gates, w_branch, layer):
    s = ya.shape[0]
    tn = MERGE_TN
    nt = D_MODEL // tn
    tm = MERGE_TM
    act = pl.BlockSpec((tm, WIDTH_A), lambda n, m: (m, 0))
    return pl.pallas_call(
        _merge_kernel,
        out_shape=jax.ShapeDtypeStruct((s, D_MODEL), BF16),
        grid=(nt, s // tm),
        in_specs=[act, act, act,
                  pl.BlockSpec((tm, tn), lambda n, m: (m, n)),
                  pl.BlockSpec((tm, tn), lambda n, m: (m, nt + n)),
                  pl.BlockSpec((tm, tn), lambda n, m: (m, 2 * nt + n)),
                  pl.BlockSpec((None, 3, WIDTH_A, tn), lambda n, m: (layer, 0, 0, n))],
        out_specs=pl.BlockSpec((tm, tn), lambda n, m: (m, n)),
        scratch_shapes=[pltpu.VMEM((3, WIDTH_A, tn), BF16)],
        compiler_params=_params("arbitrary", "arbitrary"),
        name="branch_merge",
    )(ya, yb, ym, gates, gates, gates, w_branch)


def _resid_proj_kernel(a_ref, w_ref, x_ref, o_ref, wbf_ref):
    _cast_weight_once(w_ref, wbf_ref)
    def emit(rows, acc):
        o_ref[rows, :] = x_ref[rows, :] + acc

    _sub_dots(a_ref, wbf_ref[...], emit)


def _resid_proj(a, w, x, layer, tm, tn, name):
    s, k = a.shape
    n_out = w.shape[2]
    return pl.pallas_call(
        _resid_proj_kernel,
        out_shape=jax.ShapeDtypeStruct((s, n_out), F32),
        grid=(n_out // tn, s // tm),
        in_specs=[pl.BlockSpec((tm, k), lambda n, m: (m, 0)),
                  pl.BlockSpec((None, k, tn), lambda n, m: (layer, 0, n)),
                  pl.BlockSpec((tm, tn), lambda n, m: (m, n))],
        out_specs=pl.BlockSpec((tm, tn), lambda n, m: (m, n)),
        scratch_shapes=[pltpu.VMEM((k, tn), BF16)],
        compiler_params=_params("arbitrary", "arbitrary"),
        name=name,
    )(a, w, x)


OUT_TM = 512


def _out_proj_norm_kernel(a_ref, w_ref, x_ref, g_ref, o_ref, h_ref, wbf_ref):
    @pl.when(pl.program_id(0) == 0)
    def _():
        wbf_ref[...] = w_ref[...].astype(BF16)

    def emit(rows, acc):
        x = x_ref[rows, :] + acc
        o_ref[rows, :] = x
        ms = jnp.mean(x * x, axis=-1, keepdims=True)
        h_ref[rows, :] = (x * lax.rsqrt(ms + EPS) * g_ref[...]).astype(h_ref.dtype)

    _sub_dots(a_ref, wbf_ref[...], emit)


def _out_proj_norm(a, w, x, g_all, layer):
    s, k = a.shape
    tm = OUT_TM
    return pl.pallas_call(
        _out_proj_norm_kernel,
        out_shape=(jax.ShapeDtypeStruct((s, D_MODEL), F32),
                   jax.ShapeDtypeStruct((s, D_MODEL), BF16)),
        grid=(s // tm,),
        in_specs=[pl.BlockSpec((tm, k), lambda m: (m, 0)),
                  pl.BlockSpec((None, k, D_MODEL), lambda m: (layer, 0, 0),
                               pipeline_mode=pl.Buffered(1)),
                  pl.BlockSpec((tm, D_MODEL), lambda m: (m, 0)),
                  pl.BlockSpec((None, 1, D_MODEL), lambda m: (layer, 0, 0))],
        out_specs=(pl.BlockSpec((tm, D_MODEL), lambda m: (m, 0)),
                   pl.BlockSpec((tm, D_MODEL), lambda m: (m, 0))),
        scratch_shapes=[pltpu.VMEM((k, D_MODEL), BF16)],
        compiler_params=_params("arbitrary"),
        name="out_proj_norm",
    )(a, w, x, g_all)


FFN_TN = 512
FFN_DOWN_TM = 512
FFN_DOWN_TN = 512


def _ffn_up_kernel(h_ref, wg_ref, wu_ref, o_ref, wgbf_ref, wubf_ref):
    @pl.when(pl.program_id(1) == 0)
    def _():
        wgbf_ref[...] = wg_ref[...].astype(BF16)
        wubf_ref[...] = wu_ref[...].astype(BF16)
    for r in range(0, h_ref.shape[0], SUB_ROWS):
        rows = slice(r, r + SUB_ROWS)
        gt = jnp.dot(h_ref[rows, :], wgbf_ref[...], preferred_element_type=F32)
        up = jnp.dot(h_ref[rows, :], wubf_ref[...], preferred_element_type=F32)
        o_ref[rows, :] = (jax.nn.silu(gt) * up).astype(o_ref.dtype)


def _ffn_up(h, w_gate_up, layer):
    s = h.shape[0]
    tn = FFN_TN
    nt = D_FF // tn
    tm = TM_BIG
    return pl.pallas_call(
        _ffn_up_kernel,
        out_shape=jax.ShapeDtypeStruct((s, D_FF), BF16),
        grid=(nt, s // tm),
        in_specs=[pl.BlockSpec((tm, D_MODEL), lambda n, m: (m, 0)),
                  pl.BlockSpec((None, D_MODEL, tn), lambda n, m: (layer, 0, n)),
                  pl.BlockSpec((None, D_MODEL, tn), lambda n, m: (layer, 0, nt + n))],
        out_specs=pl.BlockSpec((tm, tn), lambda n, m: (m, n)),
        scratch_shapes=[pltpu.VMEM((D_MODEL, tn), BF16), pltpu.VMEM((D_MODEL, tn), BF16)],
        compiler_params=_params("arbitrary", "arbitrary"),
        name="ffn_up",
    )(h, w_gate_up, w_gate_up)


def kernel(x, mem, g_mix, w_in, gq_a, gk_a, g_sgu, w_sgu, b_sgu, gq_m, gk_m, g_mem,
           w_mem_kv, w_branch, w_out, g_ffn, w_gate_up, w_down):
    b, s, d = x.shape
    assert (b, s, d) == (1, SEQ, D_MODEL) and mem.shape == (1, N_MEM, D_MODEL)
    x2 = x.reshape(s, d)
    mem2 = mem.reshape(N_MEM, d)

    def row(p):
        return p.reshape(DEPTH, 1, p.shape[-1])

    g_mix3, g_ffn3, g_mem3, g_sgu3 = row(g_mix), row(g_ffn), row(g_mem), row(g_sgu)
    gq_a3, gk_a3, gq_m3, gk_m3 = row(gq_a), row(gk_a), row(gq_m), row(gk_m)
    b_sgu_t = jnp.swapaxes(b_sgu, 1, 2)
    slopes = 2.0 ** (-8.0 * jnp.arange(1, N_HEADS_A + 1, dtype=F32) / N_HEADS_A)
    slopes = jnp.broadcast_to(slopes[:, None, None], (N_HEADS_A, 1, MOBA_BLOCK))

    for layer in range(DEPTH):
        h, qaug, kaug, vt = _qkv_proj(x2, g_mix3, w_in, gq_a3, gk_a3, slopes, layer)
        yb = _sgu_branch(h, w_in, g_sgu3, w_sgu, b_sgu_t, layer)
        kvm = _memkv_proj(mem2, g_mem3, w_mem_kv, gk_m3, layer)
        ym = _qm_memattn(h, w_in, gq_m3, kvm, layer)
        gates = _gates_proj(h, w_in, layer)
        ya = _moba(qaug, kaug, vt)
        merged = _merge(ya, yb, ym, gates, w_branch, layer)
        x2, hf = _out_proj_norm(merged, w_out, x2, g_ffn3, layer)
        act = _ffn_up(hf, w_gate_up, layer)
        x2 = _resid_proj(act, w_down, x2, layer, FFN_DOWN_TM, FFN_DOWN_TN, "ffn_down")
    return x2.reshape(b, s, d)
```

```python
import jax
import jax.numpy as jnp
from jax import lax
from jax.experimental import pallas as pl
from jax.experimental.pallas import tpu as pltpu

F32 = jnp.float32
BF16 = jnp.bfloat16

D_MODEL = 2048
SEQ = 8192
DEPTH = 2
N_HEADS_A = 8
HEAD_DIM_A = 128
WIDTH_A = N_HEADS_A * HEAD_DIM_A
MOBA_BLOCK = 256
MOBA_TOPK = 3
N_BLOCKS = SEQ // MOBA_BLOCK
SGU_WIDTH = 1024
SGU_GROUPS = 8
SGU_CHUNK = 128
N_MEM = 256
N_HEADS_M = 4
HEAD_DIM_M = 256
WIDTH_M = N_HEADS_M * HEAD_DIM_M
D_FF = 5632
NEG_INF = -1e30
EPS = 1e-6

COL_UV = 3 * WIDTH_A
COL_QM = COL_UV + 2 * SGU_WIDTH
COL_GATES = COL_QM + WIDTH_M

VMEM_LIMIT_BYTES = 60 * 1024 * 1024

TM = 1024
TM_BIG = 2048
TN = 1024
SUB_ROWS = 256


def _params(*semantics):
    return pltpu.CompilerParams(dimension_semantics=semantics,
                                vmem_limit_bytes=VMEM_LIMIT_BYTES)


def _cast_weight_once(w_ref, wbf_ref):
    @pl.when(pl.program_id(1) == 0)
    def _():
        wbf_ref[...] = w_ref[...].astype(BF16)


def _group_rmsnorm(a, g, width):
    outs = []
    for s in range(0, a.shape[1], width):
        blk = a[:, s:s + width]
        ms = jnp.mean(blk * blk, axis=-1, keepdims=True)
        outs.append(blk * lax.rsqrt(ms + EPS) * g)
    return jnp.concatenate(outs, axis=1)


def _sub_dots(lhs_ref, rhs, emit, lhs_rows=None):
    def sub_dot(r):
        rows = slice(r, r + SUB_ROWS)
        lhs = lhs_rows(rows) if lhs_rows is not None else lhs_ref[rows, :]
        return rows, jnp.dot(lhs, rhs, preferred_element_type=F32)

    n_rows = lhs_ref.shape[0]
    cur = sub_dot(0)
    for r in range(0, n_rows, SUB_ROWS):
        nxt = sub_dot(r + SUB_ROWS) if r + SUB_ROWS < n_rows else None
        emit(*cur)
        cur = nxt


COL_KEY_OFF = N_BLOCKS
COL_BLK_OFF = N_BLOCKS + 1
COL_ONES = N_BLOCKS + 2
SUM_ROWS = 16
GATE_TAKEN = -3e38


def _head_norm(a, g):
    ms = jnp.mean(a * a, axis=-1, keepdims=True)
    return a * lax.rsqrt(ms + EPS) * g


def _k_kernel(x_ref, g_ref, w_ref, gk_ref, slope_ref, kaug_ref, kmean_ref, h_ref, wbf_ref):
    m = pl.program_id(1)

    def norm_rows(rows):
        x = x_ref[rows, :]
        ms = jnp.mean(x * x, axis=-1, keepdims=True)
        h = (x * lax.rsqrt(ms + EPS) * g_ref[...]).astype(BF16)
        h_ref[rows, :] = h
        return h

    _cast_weight_once(w_ref, wbf_ref)
    assert SUB_ROWS == MOBA_BLOCK
    blk, dh = MOBA_BLOCK, HEAD_DIM_A
    lane = lax.broadcasted_iota(jnp.int32, (blk, dh), 1)
    lane_row = lax.broadcasted_iota(jnp.int32, (1, dh), 1)
    key_off = lax.broadcasted_iota(jnp.int32, (blk, dh), 0).astype(F32)

    def emit(rows, acc):
        b_local = rows.start // blk
        b = m * (h_ref.shape[0] // blk) + b_local
        blk_start = jnp.full((1, dh), b * blk, jnp.int32).astype(F32)
        onehot = jnp.where(lane == b, 1.0, jnp.where(lane == COL_ONES, 1.0, 0.0))
        for hh in range(N_HEADS_A):
            cols = slice(hh * dh, (hh + 1) * dh)
            kn = _head_norm(acc[:, cols], gk_ref[...])
            kmean_ref[b_local:b_local + 1, cols] = jnp.mean(kn, axis=0, keepdims=True)
            sl = slope_ref[hh][:, :dh]
            extra = (onehot + jnp.where(lane == COL_KEY_OFF, sl * key_off, 0.0)
                     + jnp.where(lane_row == COL_BLK_OFF, sl * blk_start, 0.0))
            kaug_ref[hh, rows, 0:dh] = kn.astype(BF16)
            kaug_ref[hh, rows, dh:2 * dh] = extra.astype(BF16)

    _sub_dots(x_ref, wbf_ref[...], emit, lhs_rows=norm_rows)


def _q_kernel(h_ref, w_ref, gq_ref, kmean_ref, slope_ref, qaug_ref, wbf_ref):
    m = pl.program_id(1)
    _cast_weight_once(w_ref, wbf_ref)
    assert SUB_ROWS == MOBA_BLOCK
    blk, dh = MOBA_BLOCK, HEAD_DIM_A
    blk_id = lax.broadcasted_iota(jnp.int32, (N_BLOCKS, blk), 0)
    blk_id_f = blk_id.astype(F32)
    rest = lax.broadcasted_iota(jnp.int32, (dh - N_BLOCKS, blk), 0) + N_BLOCKS

    def emit(rows, acc):
        i = m * (h_ref.shape[0] // blk) + rows.start // blk
        past = blk_id < i
        q_start = jnp.full((1, blk), i * blk, jnp.int32).astype(F32)
        for hh in range(N_HEADS_A):
            cols = slice(hh * dh, (hh + 1) * dh)
            q = _head_norm(acc[:, cols], gq_ref[...]) * dh ** -0.5
            q_t = q.T
            gate = jnp.dot(kmean_ref[:, cols].astype(BF16), q_t.astype(BF16),
                           preferred_element_type=F32)
            gate = jnp.where(past, gate, NEG_INF)
            pen = jnp.full((N_BLOCKS, blk), NEG_INF, F32)
            for _ in range(MOBA_TOPK):
                mx = jnp.max(gate, axis=0, keepdims=True)
                first = jnp.min(jnp.where(gate == mx, blk_id_f, float(N_BLOCKS)),
                                axis=0, keepdims=True)
                hit = blk_id_f == first
                pen = jnp.where(jnp.logical_and(hit, mx > 0.5 * NEG_INF), 0.0, pen)
                gate = jnp.where(hit, GATE_TAKEN, gate)
            pen = jnp.where(blk_id == i, 0.0, pen)
            extra_q = jnp.where(rest == COL_ONES, -slope_ref[hh] * q_start,
                                jnp.where(rest < COL_ONES, 1.0, 0.0))
            qaug_ref[hh, :, rows] = jnp.concatenate([q_t, pen, extra_q], axis=0).astype(BF16)

    _sub_dots(h_ref, wbf_ref[...], emit)


def _v_kernel(h_ref, w_ref, vt_ref, wbf_ref):
    _cast_weight_once(w_ref, wbf_ref)
    assert SUB_ROWS == MOBA_BLOCK
    blk, dh = MOBA_BLOCK, HEAD_DIM_A

    def emit(rows, acc):
        b_local = rows.start // blk
        for hh in range(N_HEADS_A):
            vt_ref[hh, b_local, 0:dh, :] = acc[:, hh * dh:(hh + 1) * dh].T.astype(BF16)
            vt_ref[hh, b_local, dh:dh + SUM_ROWS, :] = jnp.ones((SUM_ROWS, blk), BF16)

    _sub_dots(h_ref, wbf_ref[...], emit)


def _qkv_proj(x, g_mix, w_in, gq_a, gk_a, slopes, layer):
    s = x.shape[0]
    tm, dh, nh = TM, HEAD_DIM_A, N_HEADS_A
    grid = (1, s // tm)
    h_spec = pl.BlockSpec((tm, D_MODEL), lambda n, m: (m, 0))
    gain = pl.BlockSpec((None, 1, dh), lambda n, m: (layer, 0, 0))
    slope = pl.BlockSpec((nh, 1, MOBA_BLOCK), lambda n, m: (0, 0, 0))
    scratch = [pltpu.VMEM((D_MODEL, TN), BF16)]
    params = _params("arbitrary", "arbitrary")

    def w_spec(col):
        return pl.BlockSpec((None, D_MODEL, TN), lambda n, m: (layer, 0, col))

    kaug, kmeans, h = pl.pallas_call(
        _k_kernel,
        out_shape=(jax.ShapeDtypeStruct((nh, s, 2 * dh), BF16),
                   jax.ShapeDtypeStruct((s // tm, tm // MOBA_BLOCK, WIDTH_A), F32),
                   jax.ShapeDtypeStruct((s, D_MODEL), BF16)),
        grid=grid,
        in_specs=[h_spec, pl.BlockSpec((None, 1, D_MODEL), lambda n, m: (layer, 0, 0)),
                  w_spec(1), gain, slope],
        out_specs=(pl.BlockSpec((nh, tm, 2 * dh), lambda n, m: (0, m, 0)),
                   pl.BlockSpec((None, tm // MOBA_BLOCK, WIDTH_A), lambda n, m: (m, 0, 0)),
                   h_spec),
        scratch_shapes=scratch, compiler_params=params, name="k_proj",
    )(x, g_mix, w_in, gk_a, slopes)
    kmean = kmeans.reshape(N_BLOCKS, WIDTH_A)
    qaug = pl.pallas_call(
        _q_kernel,
        out_shape=jax.ShapeDtypeStruct((nh, 2 * dh, s), BF16),
        grid=grid,
        in_specs=[h_spec, w_spec(0), gain,
                  pl.BlockSpec((N_BLOCKS, WIDTH_A), lambda n, m: (0, 0)), slope],
        out_specs=pl.BlockSpec((nh, 2 * dh, tm), lambda n, m: (0, 0, m)),
        scratch_shapes=scratch, compiler_params=params, name="q_proj",
    )(h, w_in, gq_a, kmean, slopes)
    vt = pl.pallas_call(
        _v_kernel,
        out_shape=jax.ShapeDtypeStruct((nh, N_BLOCKS, dh + SUM_ROWS, MOBA_BLOCK), BF16),
        grid=grid,
        in_specs=[h_spec, w_spec(2)],
        out_specs=pl.BlockSpec((nh, tm // MOBA_BLOCK, dh + SUM_ROWS, MOBA_BLOCK),
                               lambda n, m: (0, m, 0, 0)),
        scratch_shapes=scratch, compiler_params=params, name="v_proj",
    )(h, w_in)
    return h, qaug, kaug, vt


def _sgu_kernel(h_ref, wu_ref, wv_ref, g_ref, ws_ref, bt_ref, o_ref,
                wubf_ref, wvbf_ref, wsbf_ref):
    @pl.when(pl.program_id(0) == 0)
    def _():
        wubf_ref[...] = wu_ref[...].astype(BF16)
        wvbf_ref[...] = wv_ref[...].astype(BF16)
        row = lax.broadcasted_iota(jnp.int32, (SGU_CHUNK, SGU_CHUNK), 0)
        col = lax.broadcasted_iota(jnp.int32, (SGU_CHUNK, SGU_CHUNK), 1)
        for g in range(SGU_GROUPS):
            wsbf_ref[g] = jnp.where(col <= row, ws_ref[g], 0.0).astype(BF16)

    def proj(r):
        rows = slice(r, r + SUB_ROWS)
        return (rows,
                jnp.dot(h_ref[rows, :], wubf_ref[...], preferred_element_type=F32),
                jnp.dot(h_ref[rows, :], wvbf_ref[...], preferred_element_type=F32))

    def mix(rows, u_acc, v_acc):
        u = jax.nn.gelu(u_acc)
        v = jax.nn.gelu(v_acc)
        ms = jnp.mean(v * v, axis=-1, keepdims=True)
        vn = (v * lax.rsqrt(ms + EPS) * g_ref[...]).astype(BF16)
        chunks = [slice(c, c + SGU_CHUNK) for c in range(0, SUB_ROWS, SGU_CHUNK)]
        for g in range(SGU_GROUPS):
            cols = slice(g * SGU_CHUNK, (g + 1) * SGU_CHUNK)
            v_all = jnp.concatenate([vn[c, cols] for c in chunks], axis=1)
            mixed = jnp.dot(wsbf_ref[g], v_all, preferred_element_type=F32) + bt_ref[:, g:g + 1]
            for c in chunks:
                o_ref[rows.start + c.start:rows.start + c.stop, cols] = (
                    u[c, cols] * mixed[:, c]).astype(o_ref.dtype)

    n_rows = h_ref.shape[0]
    cur = proj(0)
    for r in range(0, n_rows, SUB_ROWS):
        nxt = proj(r + SUB_ROWS) if r + SUB_ROWS < n_rows else None
        mix(*cur)
        cur = nxt


def _sgu_branch(h, w_in, g_sgu, w_sgu, b_sgu_t, layer):
    s = h.shape[0]
    off = COL_UV // TN
    once = pl.Buffered(1)
    return pl.pallas_call(
        _sgu_kernel,
        out_shape=jax.ShapeDtypeStruct((s, SGU_WIDTH), BF16),
        grid=(s // TM,),
        in_specs=[pl.BlockSpec((TM, D_MODEL), lambda m: (m, 0)),
                  pl.BlockSpec((None, D_MODEL, TN), lambda m: (layer, 0, off),
                               pipeline_mode=once),
                  pl.BlockSpec((None, D_MODEL, TN), lambda m: (layer, 0, off + 1),
                               pipeline_mode=once),
                  pl.BlockSpec((None, 1, SGU_WIDTH), lambda m: (layer, 0, 0)),
                  pl.BlockSpec((None, SGU_GROUPS, SGU_CHUNK, SGU_CHUNK),
                               lambda m: (layer, 0, 0, 0)),
                  pl.BlockSpec((None, SGU_CHUNK, SGU_GROUPS), lambda m: (layer, 0, 0))],
        out_specs=pl.BlockSpec((TM, SGU_WIDTH), lambda m: (m, 0)),
        scratch_shapes=[pltpu.VMEM((D_MODEL, TN), BF16), pltpu.VMEM((D_MODEL, TN), BF16),
                        pltpu.VMEM((SGU_GROUPS, SGU_CHUNK, SGU_CHUNK), BF16)],
        compiler_params=_params("arbitrary"),
        name="sgu_branch",
    )(h, w_in, w_in, g_sgu, w_sgu, b_sgu_t)


def _memkv_kernel(mem_ref, gmem_ref, w_ref, gk_ref, o_ref):
    n = pl.program_id(0)
    x = mem_ref[...]
    ms = jnp.mean(x * x, axis=-1, keepdims=True)
    hm = (x * lax.rsqrt(ms + EPS) * gmem_ref[...]).astype(BF16)
    acc = jnp.dot(hm, w_ref[...].astype(BF16), preferred_element_type=F32)

    @pl.when(n == 0)
    def _():
        o_ref[...] = _group_rmsnorm(acc, gk_ref[...], HEAD_DIM_M).astype(o_ref.dtype)

    @pl.when(n == 1)
    def _():
        o_ref[...] = acc.astype(o_ref.dtype)


def _memkv_proj(mem, g_mem, w_mem_kv, gk_m, layer):
    return pl.pallas_call(
        _memkv_kernel,
        out_shape=jax.ShapeDtypeStruct((2, N_MEM, WIDTH_M), BF16),
        grid=(2,),
        in_specs=[pl.BlockSpec((N_MEM, D_MODEL), lambda n: (0, 0)),
                  pl.BlockSpec((None, 1, D_MODEL), lambda n: (layer, 0, 0)),
                  pl.BlockSpec((None, D_MODEL, WIDTH_M), lambda n: (layer, 0, n)),
                  pl.BlockSpec((None, 1, HEAD_DIM_M), lambda n: (layer, 0, 0))],
        out_specs=pl.BlockSpec((None, N_MEM, WIDTH_M), lambda n: (n, 0, 0)),
        compiler_params=_params("arbitrary"),
        name="memkv_proj",
    )(mem, g_mem, w_mem_kv, gk_m)


def _qm_kernel(h_ref, w_ref, gq_ref, km_ref, vm_ref, o_ref, wbf_ref):
    _cast_weight_once(w_ref, wbf_ref)

    def emit(rows, acc):
        heads = [slice(hh * HEAD_DIM_M, (hh + 1) * HEAD_DIM_M) for hh in range(N_HEADS_M)]
        scores = []
        for cols in heads:
            qh = acc[:, cols]
            ms = jnp.mean(qh * qh, axis=-1, keepdims=True)
            qn = (qh * lax.rsqrt(ms + EPS) * gq_ref[...]).astype(BF16)
            scores.append(lax.dot_general(qn, km_ref[:, cols], (((1,), (1,)), ((), ())),
                                          preferred_element_type=F32) * HEAD_DIM_M ** -0.5)
        weights = []
        for sc in scores:
            mx = jnp.max(sc, axis=-1, keepdims=True)
            e = jnp.exp(sc - mx)
            weights.append((e / jnp.sum(e, axis=-1, keepdims=True)).astype(BF16))
        for cols, p in zip(heads, weights):
            o_ref[rows, cols] = jnp.dot(
                p, vm_ref[:, cols], preferred_element_type=F32).astype(o_ref.dtype)

    _sub_dots(h_ref, wbf_ref[...], emit)


def _qm_memattn(h, w_in, gq_m, kvm, layer):
    s = h.shape[0]
    off = COL_QM // TN
    tm = TM_BIG
    return pl.pallas_call(
        _qm_kernel,
        out_shape=jax.ShapeDtypeStruct((s, WIDTH_M), BF16),
        grid=(1, s // tm),
        in_specs=[pl.BlockSpec((tm, D_MODEL), lambda n, m: (m, 0)),
                  pl.BlockSpec((None, D_MODEL, TN), lambda n, m: (layer, 0, off)),
                  pl.BlockSpec((None, 1, HEAD_DIM_M), lambda n, m: (layer, 0, 0)),
                  pl.BlockSpec((None, N_MEM, WIDTH_M), lambda n, m: (0, 0, 0)),
                  pl.BlockSpec((None, N_MEM, WIDTH_M), lambda n, m: (1, 0, 0))],
        out_specs=pl.BlockSpec((tm, WIDTH_M), lambda n, m: (m, 0)),
        scratch_shapes=[pltpu.VMEM((D_MODEL, TN), BF16)],
        compiler_params=_params("arbitrary", "arbitrary"),
        name="qm_memattn",
    )(h, w_in, gq_m, kvm, kvm)


def _gates_kernel(h_ref, w_ref, o_ref, wbf_ref):
    _cast_weight_once(w_ref, wbf_ref)
    def emit(rows, acc):
        o_ref[rows, :] = jax.nn.sigmoid(acc).astype(o_ref.dtype)

    _sub_dots(h_ref, wbf_ref[...], emit)


def _gates_proj(h, w_in, layer):
    s = h.shape[0]
    off = COL_GATES // TN
    n_tiles = 3 * D_MODEL // TN
    return pl.pallas_call(
        _gates_kernel,
        out_shape=jax.ShapeDtypeStruct((s, 3 * D_MODEL), BF16),
        grid=(n_tiles, s // TM_BIG),
        in_specs=[pl.BlockSpec((TM_BIG, D_MODEL), lambda n, m: (m, 0)),
                  pl.BlockSpec((None, D_MODEL, TN), lambda n, m: (layer, 0, off + n))],
        out_specs=pl.BlockSpec((TM_BIG, TN), lambda n, m: (m, n)),
        scratch_shapes=[pltpu.VMEM((D_MODEL, TN), BF16)],
        compiler_params=_params("arbitrary", "arbitrary"),
        name="gates_proj",
    )(h, w_in)


MOBA_HALF = 2
MOBA_HEADS = 2
MOBA_QBLOCKS = 4
M_INIT = -3e38


def _moba_block(i, q_aug, q_next, kaug_ref, vt_ref, put_out, s_ref, s2_ref, snext_ref,
                acc_ref, may_start_group):
    blk, dh = MOBA_BLOCK, HEAD_DIM_A
    heads = range(MOBA_HEADS)
    half = MOBA_HALF
    half_rows = half * blk
    key_minus_qry = (lax.broadcasted_iota(jnp.int32, (half_rows, blk), 0)
                     - lax.broadcasted_iota(jnp.int32, (half_rows, blk), 1))

    def score(hh, first_blk):
        rows = pl.ds(pl.multiple_of(first_blk * blk, half_rows), half_rows)
        return jnp.dot(kaug_ref[hh, rows, :], q_aug(hh), preferred_element_type=F32)

    def attend(hh, m, s, first_blk, causal):
        if causal:
            s = jnp.where(key_minus_qry <= (i - first_blk) * blk, s, NEG_INF)
        m_new = jnp.maximum(m, jnp.max(s, axis=0, keepdims=True))
        alpha = jnp.exp(m - m_new)
        p = jnp.exp(s - m_new)
        v_t = jnp.concatenate([vt_ref[hh, first_blk + g] for g in range(half)], axis=1)
        acc_ref[hh] = alpha * acc_ref[hh] + jnp.dot(v_t, p.astype(BF16),
                                                    preferred_element_type=F32)
        return m_new

    def trip(t, carry, last):
        first = t * (2 * half)
        if last:
            for hh in heads:
                s2_ref[hh] = score(hh, first + half)
            for hh in heads:
                snext_ref[hh] = jnp.dot(kaug_ref[hh, 0:half_rows, :], q_next(hh),
                                        preferred_element_type=F32)
            carry = tuple(attend(hh, carry[hh], s_ref[hh], first, True) for hh in heads)

            def second_half(c):
                return tuple(attend(hh, c[hh], s2_ref[hh], first + half, True) for hh in heads)

            return lax.cond(i >= first + half, second_half, lambda c: c, carry)
        s_second = [score(hh, first + half) for hh in heads]
        carry = [attend(hh, carry[hh], s_ref[hh], first, False) for hh in heads]
        for hh in heads:
            s_ref[hh] = score(hh, first + 2 * half)
        carry = [attend(hh, carry[hh], s_second[hh], first + half, False) for hh in heads]
        return tuple(carry)

    if may_start_group:
        @pl.when(i == 0)
        def _():
            for hh in heads:
                s_ref[hh] = score(hh, 0)

    for hh in heads:
        acc_ref[hh] = jnp.zeros((dh + SUM_ROWS, blk), F32)
    state = tuple(jnp.full((1, blk), M_INIT, F32) for _ in heads)
    last_trip = i // (2 * half)
    state = lax.fori_loop(0, last_trip // 2,
                          lambda u, c: trip(2 * u + 1, trip(2 * u, c, False), False), state)
    state = lax.fori_loop(0, last_trip % 2, lambda _, c: trip(last_trip - 1, c, False), state)
    state = trip(last_trip, state, True)
    for hh in heads:
        acc = acc_ref[hh]
        put_out(hh, (acc[0:dh] / acc[dh:dh + 1]).T)


def _moba_kernel(qaug_ref, qnext_ref, kaug_ref, vt_ref, o_ref, s_ref, s2_ref, snext_ref,
                 acc_ref):
    blk, dh = MOBA_BLOCK, HEAD_DIM_A
    assert MOBA_QBLOCKS % 2 == 0
    for sub in range(MOBA_QBLOCKS):
        i = pl.program_id(1) * MOBA_QBLOCKS + sub
        lanes = slice(sub * blk, (sub + 1) * blk)
        after = slice((sub + 1) * blk, (sub + 2) * blk)

        def q_aug(hh, lanes=lanes):
            return qaug_ref[hh, :, lanes]

        def q_next(hh, after=after, in_step=sub + 1 < MOBA_QBLOCKS):
            return qaug_ref[hh, :, after] if in_step else qnext_ref[hh]

        def put_out(hh, x, rows=lanes):
            o_ref[rows, hh * dh:(hh + 1) * dh] = x.astype(o_ref.dtype)

        s_cur, s_nxt = (s_ref, snext_ref) if sub % 2 == 0 else (snext_ref, s_ref)
        _moba_block(i, q_aug, q_next, kaug_ref, vt_ref, put_out, s_cur, s2_ref, s_nxt,
                    acc_ref, may_start_group=(sub == 0))


def _moba(qaug, kaug, vt):
    s = kaug.shape[1]
    nh, nq = MOBA_HEADS, MOBA_QBLOCKS
    steps = N_BLOCKS // nq
    return pl.pallas_call(
        _moba_kernel,
        out_shape=jax.ShapeDtypeStruct((s, WIDTH_A), BF16),
        grid=(N_HEADS_A // nh, steps),
        in_specs=[pl.BlockSpec((nh, 2 * HEAD_DIM_A, nq * MOBA_BLOCK), lambda h, i: (h, 0, i)),
                  pl.BlockSpec((nh, 2 * HEAD_DIM_A, MOBA_BLOCK),
                               lambda h, i: (h, 0, jnp.minimum(nq * (i + 1), N_BLOCKS - 1))),
                  pl.BlockSpec((nh, s, 2 * HEAD_DIM_A), lambda h, i: (h, 0, 0)),
                  pl.BlockSpec((nh, N_BLOCKS, HEAD_DIM_A + SUM_ROWS, MOBA_BLOCK),
                               lambda h, i: (h, 0, 0, 0))],
        out_specs=pl.BlockSpec((nq * MOBA_BLOCK, nh * HEAD_DIM_A), lambda h, i: (i, h)),
        scratch_shapes=[pltpu.VMEM((nh, MOBA_HALF * MOBA_BLOCK, MOBA_BLOCK), F32)] * 3
        + [pltpu.VMEM((nh, HEAD_DIM_A + SUM_ROWS, MOBA_BLOCK), F32)],
        compiler_params=_params("arbitrary", "arbitrary"),
        name="moba_attention",
    )(qaug, qaug, kaug, vt)


MERGE_TN = 1024
MERGE_TM = 512


def _merge_kernel(ya_ref, yb_ref, ym_ref, ga_ref, gb_ref, gm_ref, w_ref, o_ref, wbf_ref):
    _cast_weight_once(w_ref, wbf_ref)
    for r in range(0, ya_ref.shape[0], SUB_ROWS):
        rows = slice(r, r + SUB_ROWS)
        merged = ga_ref[rows, :] * jnp.dot(ya_ref[rows, :], wbf_ref[0],
                                           preferred_element_type=F32)
        merged += gb_ref[rows, :] * jnp.dot(yb_ref[rows, :], wbf_ref[1],
                                            preferred_element_type=F32)
        merged += gm_ref[rows, :] * jnp.dot(ym_ref[rows, :], wbf_ref[2],
                                            preferred_element_type=F32)
        o_ref[rows, :] = merged.astype(o_ref.dtype)


def _merge(ya, yb, ym, gates, w_branch, layer):
    s = ya.shape[0]
    tn = MERGE_TN
    nt = D_MODEL // tn
    tm = MERGE_TM
    act = pl.BlockSpec((tm, WIDTH_A), lambda n, m: (m, 0))
    return pl.pallas_call(
        _merge_kernel,
        out_shape=jax.ShapeDtypeStruct((s, D_MODEL), BF16),
        grid=(nt, s // tm),
        in_specs=[act, act, act,
                  pl.BlockSpec((tm, tn), lambda n, m: (m, n)),
                  pl.BlockSpec((tm, tn), lambda n, m: (m, nt + n)),
                  pl.BlockSpec((tm, tn), lambda n, m: (m, 2 * nt + n)),
                  pl.BlockSpec((None, 3, WIDTH_A, tn), lambda n, m: (layer, 0, 0, n))],
        out_specs=pl.BlockSpec((tm, tn), lambda n, m: (m, n)),
        scratch_shapes=[pltpu.VMEM((3, WIDTH_A, tn), BF16)],
        compiler_params=_params("arbitrary", "arbitrary"),
        name="branch_merge",
    )(ya, yb, ym, gates, gates, gates, w_branch)


def _resid_proj_kernel(a_ref, w_ref, x_ref, o_ref, wbf_ref):
    _cast_weight_once(w_ref, wbf_ref)
    def emit(rows, acc):
        o_ref[rows, :] = x_ref[rows, :] + acc

    _sub_dots(a_ref, wbf_ref[...], emit)


def _resid_proj(a, w, x, layer, tm, tn, name):
    s, k = a.shape
    n_out = w.shape[2]
    return pl.pallas_call(
        _resid_proj_kernel,
        out_shape=jax.ShapeDtypeStruct((s, n_out), F32),
        grid=(n_out // tn, s // tm),
        in_specs=[pl.BlockSpec((tm, k), lambda n, m: (m, 0)),
                  pl.BlockSpec((None, k, tn), lambda n, m: (layer, 0, n)),
                  pl.BlockSpec((tm, tn), lambda n, m: (m, n))],
        out_specs=pl.BlockSpec((tm, tn), lambda n, m: (m, n)),
        scratch_shapes=[pltpu.VMEM((k, tn), BF16)],
        compiler_params=_params("arbitrary", "arbitrary"),
        name=name,
    )(a, w, x)


OUT_TM = 512


def _out_proj_norm_kernel(a_ref, w_ref, x_ref, g_ref, o_ref, h_ref, wbf_ref):
    @pl.when(pl.program_id(0) == 0)
    def _():
        wbf_ref[...] = w_ref[...].astype(BF16)

    def emit(rows, acc):
        x = x_ref[rows, :] + acc
        o_ref[rows, :] = x
        ms = jnp.mean(x * x, axis=-1, keepdims=True)
        h_ref[rows, :] = (x * lax.rsqrt(ms + EPS) * g_ref[...]).astype(h_ref.dtype)

    _sub_dots(a_ref, wbf_ref[...], emit)


def _out_proj_norm(a, w, x, g_all, layer):
    s, k = a.shape
    tm = OUT_TM
    return pl.pallas_call(
        _out_proj_norm_kernel,
        out_shape=(jax.ShapeDtypeStruct((s, D_MODEL), F32),
                   jax.ShapeDtypeStruct((s, D_MODEL), BF16)),
        grid=(s // tm,),
        in_specs=[pl.BlockSpec((tm, k), lambda m: (m, 0)),
                  pl.BlockSpec((None, k, D_MODEL), lambda m: (layer, 0, 0),
                               pipeline_mode=pl.Buffered(1)),
                  pl.BlockSpec((tm, D_MODEL), lambda m: (m, 0)),
                  pl.BlockSpec((None, 1, D_MODEL), lambda m: (layer, 0, 0))],
        out_specs=(pl.BlockSpec((tm, D_MODEL), lambda m: (m, 0)),
                   pl.BlockSpec((tm, D_MODEL), lambda m: (m, 0))),
        scratch_shapes=[pltpu.VMEM((k, D_MODEL), BF16)],
        compiler_params=_params("arbitrary"),
        name="out_proj_norm",
    )(a, w, x, g_all)


FFN_TN = 512
FFN_DOWN_TM = 512
FFN_DOWN_TN = 512


def _ffn_up_kernel(h_ref, wg_ref, wu_ref, o_ref, wgbf_ref, wubf_ref):
    @pl.when(pl.program_id(1) == 0)
    def _():
        wgbf_ref[...] = wg_ref[...].astype(BF16)
        wubf_ref[...] = wu_ref[...].astype(BF16)
    for r in range(0, h_ref.shape[0], SUB_ROWS):
        rows = slice(r, r + SUB_ROWS)
        gt = jnp.dot(h_ref[rows, :], wgbf_ref[...], preferred_element_type=F32)
        up = jnp.dot(h_ref[rows, :], wubf_ref[...], preferred_element_type=F32)
        o_ref[rows, :] = (jax.nn.silu(gt) * up).astype(o_ref.dtype)


def _ffn_up(h, w_gate_up, layer):
    s = h.shape[0]
    tn = FFN_TN
    nt = D_FF // tn
    tm = TM_BIG
    return pl.pallas_call(
        _ffn_up_kernel,
        out_shape=jax.ShapeDtypeStruct((s, D_FF), BF16),
        grid=(nt, s // tm),
        in_specs=[pl.BlockSpec((tm, D_MODEL), lambda n, m: (m, 0)),
                  pl.BlockSpec((None, D_MODEL, tn), lambda n, m: (layer, 0, n)),
                  pl.BlockSpec((None, D_MODEL, tn), lambda n, m: (layer, 0, nt + n))],
        out_specs=pl.BlockSpec((tm, tn), lambda n, m: (m, n)),
        scratch_shapes=[pltpu.VMEM((D_MODEL, tn), BF16), pltpu.VMEM((D_MODEL, tn), BF16)],
        compiler_params=_params("arbitrary", "arbitrary"),
        name="ffn_up",
    )(h, w_gate_up, w_gate_up)


def kernel(x, mem, g_mix, w_in, gq_a, gk_a, g_sgu, w_sgu, b_sgu, gq_m, gk_m, g_mem,
           w_mem_kv, w_branch, w_out, g_ffn, w_gate_up, w_down):
    b, s, d = x.shape
    assert (b, s, d) == (1, SEQ, D_MODEL) and mem.shape == (1, N_MEM, D_MODEL)
    x2 = x.reshape(s, d)
    mem2 = mem.reshape(N_MEM, d)

    def row(p):
        return p.reshape(DEPTH, 1, p.shape[-1])

    g_mix3, g_ffn3, g_mem3, g_sgu3 = row(g_mix), row(g_ffn), row(g_mem), row(g_sgu)
    gq_a3, gk_a3, gq_m3, gk_m3 = row(gq_a), row(gk_a), row(gq_m), row(gk_m)
    b_sgu_t = jnp.swapaxes(b_sgu, 1, 2)
    slopes = 2.0 ** (-8.0 * jnp.arange(1, N_HEADS_A + 1, dtype=F32) / N_HEADS_A)
    slopes = jnp.broadcast_to(slopes[:, None, None], (N_HEADS_A, 1, MOBA_BLOCK))

    for layer in range(DEPTH):
        h, qaug, kaug, vt = _qkv_proj(x2, g_mix3, w_in, gq_a3, gk_a3, slopes, layer)
        yb = _sgu_branch(h, w_in, g_sgu3, w_sgu, b_sgu_t, layer)
        kvm = _memkv_proj(mem2, g_mem3, w_mem_kv, gk_m3, layer)
        ym = _qm_memattn(h, w_in, gq_m3, kvm, layer)
        gates = _gates_proj(h, w_in, layer)
        ya = _moba(qaug, kaug, vt)
        merged = _merge(ya, yb, ym, gates, w_branch, layer)
        x2, hf = _out_proj_norm(merged, w_out, x2, g_ffn3, layer)
        act = _ffn_up(hf, w_gate_up, layer)
        x2 = _resid_proj(act, w_down, x2, layer, FFN_DOWN_TM, FFN_DOWN_TN, "ffn_down")
    return x2.reshape(b, s, d)
```
